```python
import jax, jax.numpy as jnp
from jax import lax
import numpy as np

D_MODEL = 1024
BATCH = 4
SEQ = 4096
DEPTH = 2
DEC_BATCH = 32
DEC_SEQ = 64
PAST_LEN = 4096

CHUNK = 64
N_HEADS = 16
HEAD_DIM = D_MODEL // N_HEADS
D_ATTN = N_HEADS * HEAD_DIM
D_CONV = D_MODEL // 2
CONV_WIDTH = 31
CONV_STATE = CONV_WIDTH - 1
Q_BLOCK = 128
PEER_HEADS = 8
PEER_KEYS = 128
PEER_EXPERTS = PEER_KEYS ** 2
PEER_TOPK = 16
PEER_DKEY = 256
PEER_DHALF = PEER_DKEY // 2
PEER_BLOCK = 256
EPS = 1e-6
COL_GLU = 2 * D_CONV
COL_QKV = 3 * D_ATTN
COL_FGT = N_HEADS
COL_GATE = 2 * D_MODEL
D_IN = COL_GLU + COL_QKV + COL_FGT + COL_GATE

kernel_name = "fox_conformer_peer_streaming_step"


def rms_norm(x, g):
    xf = x.astype(jnp.float32)
    y = xf * lax.rsqrt(jnp.mean(xf * xf, axis=-1, keepdims=True) + EPS)
    return (y * g.astype(jnp.float32)).astype(x.dtype)


def layer_norm(x, g, b):
    xf = x.astype(jnp.float32)
    mu = jnp.mean(xf, axis=-1, keepdims=True)
    d = xf - mu
    var = jnp.mean(d * d, axis=-1, keepdims=True)
    return (d * lax.rsqrt(var + EPS) * g.astype(jnp.float32) + b.astype(jnp.float32)).astype(x.dtype)


def causal_depthwise_conv(u, buf, w, b):
    padded = jnp.concatenate([buf.astype(u.dtype), u], axis=1)
    y = lax.conv_general_dilated(padded, w[:, None, :].astype(u.dtype), window_strides=(1,), padding="VALID",
                                 dimension_numbers=("NWC", "WIO", "NWC"), feature_group_count=u.shape[-1])
    return y + b, padded[:, -CONV_STATE:]


def forgetting_attention(q, k, v, c_q, c_k, q_offset):
    B, Tq, H, Dh = q.shape
    Tk = k.shape[1]
    qb = min(Q_BLOCK, Tq)
    nb = Tq // qb
    scale = Dh ** -0.5
    kpos = jnp.arange(Tk)
    ck_t = jnp.transpose(c_k, (0, 2, 1))

    def block(args):
        qi, ci, pos = args
        s = jnp.einsum("bqhd,bkhd->bhqk", qi, k, preferred_element_type=jnp.float32) * scale
        s = s + (jnp.transpose(ci, (0, 2, 1))[..., :, None] - ck_t[..., None, :])
        s = jnp.where(kpos[None, None, None, :] <= pos[None, None, :, None], s, -jnp.inf)
        p = jax.nn.softmax(s, axis=-1)
        return jnp.einsum("bhqk,bkhd->bqhd", p.astype(v.dtype), v)

    qs = jnp.transpose(q.reshape(B, nb, qb, H, Dh), (1, 0, 2, 3, 4))
    cs = jnp.transpose(c_q.reshape(B, nb, qb, H), (1, 0, 2, 3))
    ps = (q_offset + jnp.arange(Tq)).reshape(nb, qb)
    out = lax.map(block, (qs, cs, ps))
    return jnp.transpose(out, (1, 0, 2, 3, 4)).reshape(B, Tq, H * Dh)


def mixer_branches(h, conv_buf, k_hist, v_hist, logf_hist, w_in, conv_w, conv_b, conv_norm_g, conv_norm_b,
                   w_conv_out, forget_bias, q_norm, k_norm, w_attn_out, w_out):
    B, T, _ = h.shape
    z = h @ w_in
    o1 = COL_GLU
    o2 = o1 + COL_QKV
    o3 = o2 + COL_FGT
    glu, qkv, f_logit, gate_logit = z[..., :o1], z[..., o1:o2], z[..., o2:o3], z[..., o3:]
    u = glu[..., :D_CONV] * jax.nn.sigmoid(glu[..., D_CONV:])
    cv, new_buf = causal_depthwise_conv(u, conv_buf, conv_w, conv_b)
    conv_out = jax.nn.silu(layer_norm(cv, conv_norm_g, conv_norm_b)) @ w_conv_out
    qkv = qkv.reshape(B, T, 3, N_HEADS, HEAD_DIM)
    q = rms_norm(qkv[:, :, 0], q_norm)
    k = rms_norm(qkv[:, :, 1], k_norm)
    v = qkv[:, :, 2]
    logf = jax.nn.log_sigmoid((f_logit + forget_bias).astype(jnp.float32))
    if k_hist is None:
        k_all, v_all, logf_all, offset = k, v, logf, 0
    else:
        k_all = jnp.concatenate([k_hist.astype(k.dtype), k], axis=1)
        v_all = jnp.concatenate([v_hist.astype(v.dtype), v], axis=1)
        logf_all = jnp.concatenate([logf_hist.astype(jnp.float32), logf], axis=1)
        offset = k_hist.shape[1]
    c_all = jnp.cumsum(logf_all, axis=1)
    attn = forgetting_attention(q, k_all, v_all, c_all[:, offset:], c_all, offset)
    attn_out = attn @ w_attn_out
    g = jax.nn.sigmoid(gate_logit)
    merged = g[..., :D_MODEL] * conv_out + g[..., D_MODEL:] * attn_out
    return merged @ w_out, k, v, logf.astype(h.dtype), new_buf


def peer(h, w_q, sub_keys, u_tab, v_tab):
    B, T, D = h.shape
    n = B * T
    npad = (-n) % PEER_BLOCK
    xt = jnp.pad(h.reshape(n, D), ((0, npad), (0, 0)))

    def block(xb):
        t = xb.shape[0]
        q = (xb @ w_q).reshape(t, PEER_HEADS, 2, PEER_DHALF)
        s = jnp.einsum("thpk,hpnk->thpn", q, sub_keys, preferred_element_type=jnp.float32)
        sv, si = lax.top_k(s, PEER_TOPK)
        cand = (sv[:, :, 0, :, None] + sv[:, :, 1, None, :]).reshape(t, PEER_HEADS, PEER_TOPK * PEER_TOPK)
        cidx = (si[:, :, 0, :, None] * PEER_KEYS + si[:, :, 1, None, :]).reshape(t, PEER_HEADS, PEER_TOPK * PEER_TOPK)
        best, pos = lax.top_k(cand, PEER_TOPK)
        eidx = jnp.take_along_axis(cidx, pos, axis=-1)
        gate = jax.nn.softmax(best, axis=-1)
        act = jax.nn.gelu(jnp.einsum("td,thkd->thk", xb, u_tab[eidx]))
        return jnp.einsum("thk,thkd->td", (gate * act).astype(xb.dtype), v_tab[eidx])

    out = lax.map(block, xt.reshape(-1, PEER_BLOCK, D))
    return out.reshape(-1, D)[:n].reshape(B, T, D)


def trunk_layer(x, conv_buf, k_hist, v_hist, logf_hist, norm_mix, norm_ffn, mix_w, peer_w):
    mix, k, v, logf, new_buf = mixer_branches(rms_norm(x, norm_mix), conv_buf, k_hist, v_hist, logf_hist, *mix_w)
    x = x + mix
    x = x + peer(rms_norm(x, norm_ffn), *peer_w)
    return x, k, v, logf, new_buf


def setup_inputs(seed: int = 0) -> dict:
    key = jax.random.key(seed)
    ks = jax.random.split(key, 24)
    L = DEPTH

    def nrm(k, shape, s):
        return jax.random.normal(k, shape, jnp.float32) * s

    return {
        "x_prompt": nrm(ks[0], (BATCH, SEQ, D_MODEL), 1.0),
        "x_sample": nrm(ks[1], (DEC_BATCH, DEC_SEQ, D_MODEL), 1.0),
        "cache_k": nrm(ks[2], (L, DEC_BATCH, PAST_LEN, N_HEADS, HEAD_DIM), 1.0),
        "cache_v": nrm(ks[3], (L, DEC_BATCH, PAST_LEN, N_HEADS, HEAD_DIM), 1.0),
        "cache_logf": jax.nn.log_sigmoid(2.0 + nrm(ks[4], (L, DEC_BATCH, PAST_LEN, N_HEADS), 0.5)),
        "state_conv": nrm(ks[5], (L, DEC_BATCH, CONV_STATE, D_CONV), 0.5),
        "norm_mix": 1.0 + nrm(ks[6], (L, D_MODEL), 0.02),
        "w_in": nrm(ks[7], (L, D_MODEL, D_IN), D_MODEL ** -0.5),
        "conv_w": nrm(ks[8], (L, CONV_WIDTH, D_CONV), CONV_WIDTH ** -0.5),
        "conv_b": nrm(ks[9], (L, D_CONV), 0.02),
        "conv_norm_g": 1.0 + nrm(ks[10], (L, D_CONV), 0.02),
        "conv_norm_b": nrm(ks[11], (L, D_CONV), 0.02),
        "w_conv_out": nrm(ks[12], (L, D_CONV, D_MODEL), D_CONV ** -0.5),
        "forget_bias": 2.0 + nrm(ks[13], (L, N_HEADS), 0.5),
        "q_norm": 1.0 + nrm(ks[14], (L, HEAD_DIM), 0.02),
        "k_norm": 1.0 + nrm(ks[15], (L, HEAD_DIM), 0.02),
        "w_attn_out": nrm(ks[16], (L, D_ATTN, D_MODEL), D_ATTN ** -0.5),
        "w_out": nrm(ks[17], (L, D_MODEL, D_MODEL), D_MODEL ** -0.5),
        "norm_ffn": 1.0 + nrm(ks[18], (L, D_MODEL), 0.02),
        "peer_w_q": nrm(ks[19], (L, D_MODEL, PEER_HEADS * PEER_DKEY), D_MODEL ** -0.5),
        "peer_keys": nrm(ks[20], (L, PEER_HEADS, 2, PEER_KEYS, PEER_DHALF), PEER_DHALF ** -0.5),
        "peer_u": nrm(ks[21], (L, PEER_EXPERTS, D_MODEL), D_MODEL ** -0.5),
        "peer_v": nrm(ks[22], (L, PEER_EXPERTS, D_MODEL), PEER_HEADS ** -0.5),
    }


def reference(x_prompt, x_sample, cache_k, cache_v, cache_logf, state_conv, norm_mix, w_in, conv_w, conv_b,
              conv_norm_g, conv_norm_b, w_conv_out, forget_bias, q_norm, k_norm, w_attn_out, w_out, norm_ffn,
              peer_w_q, peer_keys, peer_u, peer_v):
    yp, ys = x_prompt, x_sample
    zero_buf = jnp.zeros((x_prompt.shape[0], CONV_STATE, D_CONV), x_prompt.dtype)
    kp, vp, lp, cp, ksm, vsm, lsm, csm = [], [], [], [], [], [], [], []
    for l in range(DEPTH):
        mix_w = (w_in[l], conv_w[l], conv_b[l], conv_norm_g[l], conv_norm_b[l], w_conv_out[l],
                 forget_bias[l], q_norm[l], k_norm[l], w_attn_out[l], w_out[l])
        peer_w = (peer_w_q[l], peer_keys[l], peer_u[l], peer_v[l])
        yp, k1, v1, l1, b1 = trunk_layer(yp, zero_buf, None, None, None, norm_mix[l], norm_ffn[l], mix_w, peer_w)
        ys, k2, v2, l2, b2 = trunk_layer(ys, state_conv[l], cache_k[l], cache_v[l], cache_logf[l],
                                         norm_mix[l], norm_ffn[l], mix_w, peer_w)
        kp.append(k1); vp.append(v1); lp.append(l1); cp.append(b1)
        ksm.append(k2); vsm.append(v2); lsm.append(l2); csm.append(b2)
    new_k_prompt = jnp.stack(kp)
    new_v_prompt = jnp.stack(vp)
    new_logf_prompt = jnp.stack(lp)
    new_conv_prompt = jnp.stack(cp)
    new_k_sample = jnp.stack(ksm)
    new_v_sample = jnp.stack(vsm)
    new_logf_sample = jnp.stack(lsm)
    new_conv_sample = jnp.stack(csm)
    return (yp, ys, new_k_prompt, new_v_prompt, new_logf_prompt, new_conv_prompt,
            new_k_sample, new_v_sample, new_logf_sample, new_conv_sample)
```

```python
import functools

import jax
import jax.numpy as jnp
from jax import lax
from jax.experimental import pallas as pl
from jax.experimental.pallas import tpu as pltpu

F32 = jnp.float32
BF16 = jnp.bfloat16

D_MODEL = 1024
N_HEADS = 16
HEAD_DIM = 64
D_CONV = 512
CONV_WIDTH = 31
CONV_STATE = CONV_WIDTH - 1
PEER_HEADS = 8
PEER_KEYS = 128
PEER_TOPK = 16
EPS = 1e-6
LANES = 128
HIST_ROWS = 32
VMEM_LIMIT = 56 * 1024 * 1024


def _cparams(sem):
    return pltpu.CompilerParams(dimension_semantics=sem, vmem_limit_bytes=VMEM_LIMIT)


def _const_spec(shape):
    n = len(shape)
    return pl.BlockSpec(shape, lambda *_: (0,) * n)


def _split_dot(x, w):
    hi = x.astype(BF16)
    r1 = x - hi.astype(F32)
    mid = r1.astype(BF16)
    lo = (r1 - mid.astype(F32)).astype(BF16)
    return (jnp.dot(hi, w, preferred_element_type=F32) + jnp.dot(mid, w, preferred_element_type=F32)
            + jnp.dot(lo, w, preferred_element_type=F32))


def _log_sigmoid(x):
    return jnp.minimum(x, 0.0) - jnp.log(1.0 + jnp.exp(-jnp.abs(x)))


def _inproj_kernel(x_ref, g_ref, wglu_ref, wqkv_ref, wf_ref, wgate_ref, fb_ref, qg_ref, kg_ref, seg_ref, segt_ref,
                   u_ref, qb_ref, k_ref, v_ref, kb_ref, vb_ref, logf_ref, gate_ref):
    x = x_ref[...]
    h = x * lax.rsqrt(jnp.mean(x * x, axis=-1, keepdims=True) + EPS) * g_ref[...]
    hb = h.astype(BF16)
    glu = jnp.dot(hb, wglu_ref[...], preferred_element_type=F32)
    u_ref[...] = glu[:, :D_CONV] * jax.nn.sigmoid(glu[:, D_CONV:])
    qkv = jnp.dot(hb, wqkv_ref[...], preferred_element_type=F32)

    def head_norm(t, gain):
        ss = _split_dot(t * t, seg_ref[...])
        r = lax.rsqrt(ss * (1.0 / HEAD_DIM) + EPS)
        return t * _split_dot(r, segt_ref[...]) * gain

    q = head_norm(qkv[:, :D_MODEL], qg_ref[...])
    k = head_norm(qkv[:, D_MODEL:2 * D_MODEL], kg_ref[...])
    v = qkv[:, 2 * D_MODEL:]
    qb_ref[...] = q.astype(BF16)
    k_ref[...] = k
    kb_ref[...] = k.astype(BF16)
    v_ref[...] = v
    vb_ref[...] = v.astype(BF16)
    f = jnp.dot(hb, wf_ref[...], preferred_element_type=F32) + fb_ref[...]
    logf_ref[...] = _log_sigmoid(f)[:, :N_HEADS]
    gate_ref[...] = jax.nn.sigmoid(jnp.dot(hb, wgate_ref[...], preferred_element_type=F32))


def _inproj(x, w):
    n = x.shape[0]
    tm = min(256, n)
    tok = lambda c: pl.BlockSpec((tm, c), lambda i: (i, 0))
    out_shape = (
        jax.ShapeDtypeStruct((n, D_CONV), F32),
        jax.ShapeDtypeStruct((n, D_MODEL), BF16),
        jax.ShapeDtypeStruct((n, D_MODEL), F32),
        jax.ShapeDtypeStruct((n, D_MODEL), F32),
        jax.ShapeDtypeStruct((n, D_MODEL), BF16),
        jax.ShapeDtypeStruct((n, D_MODEL), BF16),
        jax.ShapeDtypeStruct((n, N_HEADS), F32),
        jax.ShapeDtypeStruct((n, 2 * D_MODEL), F32),
    )
    return pl.pallas_call(
        _inproj_kernel,
        grid=(n // tm,),
        in_specs=[tok(D_MODEL), _const_spec((1, D_MODEL)), _const_spec((D_MODEL, 2 * D_CONV)),
                  _const_spec((D_MODEL, 3 * D_MODEL)), _const_spec((D_MODEL, LANES)),
                  _const_spec((D_MODEL, 2 * D_MODEL)), _const_spec((1, LANES)), _const_spec((1, D_MODEL)),
                  _const_spec((1, D_MODEL)), _const_spec((D_MODEL, LANES)), _const_spec((LANES, D_MODEL))],
        out_specs=(tok(D_CONV), tok(D_MODEL), tok(D_MODEL), tok(D_MODEL), tok(D_MODEL), tok(D_MODEL),
                   tok(N_HEADS), tok(2 * D_MODEL)),
        out_shape=out_shape,
        compiler_params=_cparams(("parallel",)),
        name="inproj",
    )(x, w["norm_mix"], w["w_glu"], w["w_qkv"], w["w_f"], w["w_gate"], w["f_bias"], w["q_gain"], w["k_gain"],
      w["seg"], w["segt"])


CONV_ROWS = 32


def _conv_kernel(u_ref, halo_ref, buf_ref, w_ref, b_ref, lng_ref, lnb_ref, o_ref, pad_ref, *, tt):
    first = pl.program_id(1) == 0
    pad_ref[0:HIST_ROWS, :] = jnp.where(first, buf_ref[...], halo_ref[...])
    pad_ref[HIST_ROWS:HIST_ROWS + tt, :] = u_ref[...]
    off = HIST_ROWS - CONV_STATE
    for r in range(tt // CONV_ROWS):
        base = r * CONV_ROWS + off
        acc = jnp.broadcast_to(b_ref[...], (CONV_ROWS, D_CONV))
        for k in range(CONV_WIDTH):
            acc = acc + w_ref[k:k + 1, :] * pad_ref[base + k:base + k + CONV_ROWS, :]
        mu = jnp.mean(acc, axis=-1, keepdims=True)
        d = acc - mu
        var = jnp.mean(d * d, axis=-1, keepdims=True)
        y = d * lax.rsqrt(var + EPS) * lng_ref[...] + lnb_ref[...]
        o_ref[r * CONV_ROWS:(r + 1) * CONV_ROWS, :] = (y * jax.nn.sigmoid(y)).astype(BF16)


def _conv(u, buf, w):
    b, t, c = u.shape
    tt = min(256, t)
    hb = tt // HIST_ROWS
    return pl.pallas_call(
        functools.partial(_conv_kernel, tt=tt),
        grid=(b, t // tt),
        in_specs=[pl.BlockSpec((None, tt, c), lambda bi, i: (bi, i, 0)),
                  pl.BlockSpec((None, HIST_ROWS, c), lambda bi, i: (bi, jnp.maximum(i * hb - 1, 0), 0)),
                  pl.BlockSpec((None, HIST_ROWS, c), lambda bi, i: (bi, 0, 0)),
                  _const_spec((HIST_ROWS, c)), _const_spec((1, c)), _const_spec((1, c)), _const_spec((1, c))],
        out_specs=pl.BlockSpec((None, tt, c), lambda bi, i: (bi, i, 0)),
        out_shape=jax.ShapeDtypeStruct((b, t, c), BF16),
        scratch_shapes=[pltpu.VMEM((HIST_ROWS + tt, c), F32)],
        compiler_params=_cparams(("parallel", "arbitrary")),
        name="conv",
    )(u, u, buf, w["conv_w"], w["conv_b"], w["ln_g"], w["ln_b"])


def _cumsum_kernel(x_ref, tri_ref, o_ref, *, nblk):
    def body(j, carry):
        sl = pl.ds(pl.multiple_of(j * LANES, LANES), LANES)
        cs = _split_dot(x_ref[:, sl], tri_ref[...]) + carry
        o_ref[:, sl] = cs
        return cs[:, LANES - 1:LANES]

    lax.fori_loop(0, nblk, body, jnp.zeros((x_ref.shape[0], 1), F32))


def _cumsum_lanes(x):
    r, t = x.shape
    tr = min(64, r)
    ii = lax.broadcasted_iota(jnp.int32, (LANES, LANES), 0)
    jj = lax.broadcasted_iota(jnp.int32, (LANES, LANES), 1)
    tri = (ii <= jj).astype(BF16)
    return pl.pallas_call(
        functools.partial(_cumsum_kernel, nblk=t // LANES),
        grid=(r // tr,),
        in_specs=[pl.BlockSpec((tr, t), lambda i: (i, 0)), _const_spec((LANES, LANES))],
        out_specs=pl.BlockSpec((tr, t), lambda i: (i, 0)),
        out_shape=jax.ShapeDtypeStruct((r, t), F32),
        compiler_params=_cparams(("parallel",)),
        name="cumsum",
    )(x, tri)


def _stack_heads(q2):
    lane = lax.broadcasted_iota(jnp.int32, q2.shape, 1)
    zero = jnp.zeros_like(q2)
    return jnp.concatenate([jnp.where(lane < HEAD_DIM, q2, zero), jnp.where(lane >= HEAD_DIM, q2, zero)], axis=0)


def _attn_update(qs, k2, v2, c2, causal, m_ref, l_ref, acc_ref):
    tq = qs.shape[0] // 2
    tk = k2.shape[0]
    s = lax.dot_general(qs, k2, (((1,), (1,)), ((), ())), preferred_element_type=F32)
    s = jnp.concatenate([s[:tq] - c2[0:1, :], s[tq:] - c2[1:2, :]], axis=0)
    if causal:
        row = lax.broadcasted_iota(jnp.int32, (tq, tk), 0)
        col = lax.broadcasted_iota(jnp.int32, (tq, tk), 1)
        keep = jnp.concatenate([col <= row, col <= row], axis=0)
        s = jnp.where(keep, s, -jnp.inf)
    m_old = m_ref[...]
    m_new = jnp.maximum(m_old, jnp.max(s, axis=-1, keepdims=True))
    alpha = jnp.exp(m_old - m_new)
    p = jnp.exp(s - m_new)
    l_ref[...] = alpha * l_ref[...] + jnp.sum(p, axis=-1, keepdims=True)
    acc_ref[...] = alpha * acc_ref[...] + jnp.dot(p.astype(BF16), v2, preferred_element_type=F32)
    m_ref[...] = m_new


def _attn_init(m_ref, l_ref, acc_ref):
    m_ref[...] = jnp.full(m_ref.shape, -jnp.inf, F32)
    l_ref[...] = jnp.zeros(l_ref.shape, F32)
    acc_ref[...] = jnp.zeros(acc_ref.shape, F32)


def _attn_finish(o_ref, l_ref, acc_ref):
    tq = acc_ref.shape[0] // 2
    o = acc_ref[...] / l_ref[...]
    lane = lax.broadcasted_iota(jnp.int32, (tq, LANES), 1)
    o_ref[...] = jnp.where(lane < HEAD_DIM, o[:tq], o[tq:]).astype(o_ref.dtype)


def _attn_prompt_kernel(q_ref, k_ref, v_ref, c_ref, o_ref, m_ref, l_ref, acc_ref):
    qi = pl.program_id(2)
    ki = pl.program_id(3)

    @pl.when(ki == 0)
    def _():
        _attn_init(m_ref, l_ref, acc_ref)

    @pl.when(ki < qi)
    def _():
        _attn_update(_stack_heads(q_ref[...]), k_ref[...], v_ref[...], c_ref[...], False, m_ref, l_ref, acc_ref)

    @pl.when(ki == qi)
    def _():
        _attn_update(_stack_heads(q_ref[...]), k_ref[...], v_ref[...], c_ref[...], True, m_ref, l_ref, acc_ref)
        _attn_finish(o_ref, l_ref, acc_ref)


def _attn_prompt(qb, kb, vb, c):
    b, t, d = qb.shape
    tq = min(512, t)
    nq = t // tq
    pairs = d // LANES
    kv_spec = pl.BlockSpec((None, tq, LANES), lambda bi, p, qi, ki: (bi, jnp.minimum(ki, qi), p))
    return pl.pallas_call(
        _attn_prompt_kernel,
        grid=(b, pairs, nq, nq),
        in_specs=[pl.BlockSpec((None, tq, LANES), lambda bi, p, qi, ki: (bi, qi, p)), kv_spec, kv_spec,
                  pl.BlockSpec((None, None, 2, tq), lambda bi, p, qi, ki: (bi, p, 0, jnp.minimum(ki, qi)))],
        out_specs=pl.BlockSpec((None, tq, LANES), lambda bi, p, qi, ki: (bi, qi, p)),
        out_shape=jax.ShapeDtypeStruct((b, t, d), BF16),
        scratch_shapes=[pltpu.VMEM((2 * tq, 1), F32), pltpu.VMEM((2 * tq, 1), F32), pltpu.VMEM((2 * tq, LANES), F32)],
        compiler_params=_cparams(("parallel", "parallel", "parallel", "arbitrary")),
        name="attn_prompt",
    )(qb, kb, vb, c)


def _attn_sample_kernel(q_ref, kh_ref, vh_ref, ch_ref, kn_ref, vn_ref, cn_ref, o_ref, m_ref, l_ref, acc_ref, *, nk):
    ki = pl.program_id(2)

    @pl.when(ki == 0)
    def _():
        _attn_init(m_ref, l_ref, acc_ref)

    @pl.when(ki < nk)
    def _():
        _attn_update(_stack_heads(q_ref[...]), kh_ref[...].astype(BF16), vh_ref[...].astype(BF16), ch_ref[...],
                     False, m_ref, l_ref, acc_ref)

    @pl.when(ki == nk)
    def _():
        _attn_update(_stack_heads(q_ref[...]), kn_ref[...], vn_ref[...], cn_ref[...], True, m_ref, l_ref, acc_ref)
        _attn_finish(o_ref, l_ref, acc_ref)


def _attn_sample(qb, kb, vb, cache_k, cache_v, layer, c_hist, c_new):
    b, t, d = qb.shape
    past = cache_k.shape[2]
    tk = min(512, past)
    nk = past // tk
    pairs = d // LANES
    new_spec = pl.BlockSpec((None, t, LANES), lambda bi, p, ki: (bi, 0, p))
    hist_spec = pl.BlockSpec((None, None, tk, LANES), lambda bi, p, ki: (layer, bi, jnp.minimum(ki, nk - 1), p))
    return pl.pallas_call(
        functools.partial(_attn_sample_kernel, nk=nk),
        grid=(b, pairs, nk + 1),
        in_specs=[new_spec, hist_spec, hist_spec,
                  pl.BlockSpec((None, None, 2, tk), lambda bi, p, ki: (bi, p, 0, jnp.minimum(ki, nk - 1))),
                  new_spec, new_spec, pl.BlockSpec((None, None, 2, t), lambda bi, p, ki: (bi, p, 0, 0))],
        out_specs=new_spec,
        out_shape=jax.ShapeDtypeStruct((b, t, d), BF16),
        scratch_shapes=[pltpu.VMEM((2 * t, 1), F32), pltpu.VMEM((2 * t, 1), F32), pltpu.VMEM((2 * t, LANES), F32)],
        compiler_params=_cparams(("parallel", "parallel", "arbitrary")),
        name="attn_sample",
    )(qb, cache_k, cache_v, c_hist, kb, vb, c_new)


def _outproj_kernel(attn_ref, cact_ref, gate_ref, x_ref, wco_ref, wao_ref, wo_ref, g_ref, wpq_ref, keys_ref,
                    x1_ref, h2_ref, st_ref):
    conv_out = jnp.dot(cact_ref[...], wco_ref[...], preferred_element_type=F32)
    attn_out = jnp.dot(attn_ref[...], wao_ref[...], preferred_element_type=F32)
    merged = gate_ref[:, :D_MODEL] * conv_out + gate_ref[:, D_MODEL:] * attn_out
    x1 = x_ref[...] + jnp.dot(merged.astype(BF16), wo_ref[...], preferred_element_type=F32)
    x1_ref[...] = x1
    h2 = (x1 * lax.rsqrt(jnp.mean(x1 * x1, axis=-1, keepdims=True) + EPS) * g_ref[...]).astype(BF16)
    h2_ref[...] = h2
    qp = jnp.dot(h2, wpq_ref[...], preferred_element_type=F32).astype(BF16)
    for hp in range(2 * PEER_HEADS):
        st_ref[hp] = lax.dot_general(keys_ref[hp], qp[:, hp * LANES:(hp + 1) * LANES], (((1,), (1,)), ((), ())),
                                     preferred_element_type=F32)


def _outproj(attn, cact, gate, x, w):
    n = x.shape[0]
    tm = min(256, n)
    tok = lambda c: pl.BlockSpec((tm, c), lambda i: (i, 0))
    nhp = 2 * PEER_HEADS
    return pl.pallas_call(
        _outproj_kernel,
        grid=(n // tm,),
        in_specs=[tok(D_MODEL), tok(D_CONV), tok(2 * D_MODEL), tok(D_MODEL),
                  _const_spec((D_CONV, D_MODEL)), _const_spec((D_MODEL, D_MODEL)), _const_spec((D_MODEL, D_MODEL)),
                  _const_spec((1, D_MODEL)), _const_spec((D_MODEL, nhp * LANES)),
                  _const_spec((nhp, PEER_KEYS, LANES))],
        out_specs=(tok(D_MODEL), tok(D_MODEL), pl.BlockSpec((nhp, PEER_KEYS, tm), lambda i: (0, 0, i))),
        out_shape=(jax.ShapeDtypeStruct((n, D_MODEL), F32), jax.ShapeDtypeStruct((n, D_MODEL), BF16),
                   jax.ShapeDtypeStruct((nhp, PEER_KEYS, n), F32)),
        compiler_params=_cparams(("parallel",)),
        name="outproj",
    )(attn, cact, gate, x, w["w_conv_out"], w["w_attn_out"], w["w_out"], w["norm_ffn"], w["peer_w_q"], w["peer_keys"])


def _top_ranks(s):
    nrow = s.shape[0]
    iota = lax.broadcasted_iota(jnp.int32, s.shape, 0).astype(F32)
    rank = jnp.full(s.shape, float(PEER_TOPK), F32)
    vals = []
    for r in range(PEER_TOPK):
        m = jnp.max(s, axis=0, keepdims=True)
        first = jnp.min(jnp.where(s == m, iota, float(nrow)), axis=0, keepdims=True)
        sel = iota == first
        rank = jnp.where(sel, float(r), rank)
        s = jnp.where(sel, -jnp.inf, s)
        vals.append(m)
    return rank, vals


def _topk_kernel(st_ref, r2_ref, nn_ref, e2_ref, cc_ref):
    def body(h, carry):
        s1 = st_ref[2 * h]
        s2 = st_ref[2 * h + 1]
        rank1, v1 = _top_ranks(s1)
        rank2, v2 = _top_ranks(s2)
        v2a = jnp.concatenate(v2, axis=0)
        cand = jnp.concatenate([v1[a] + v2a for a in range(PEER_TOPK)], axis=0)
        rankc, _ = _top_ranks(cand)
        chosen = rankc < float(PEER_TOPK)
        z = jnp.sum(jnp.where(chosen, jnp.exp(cand - cand[0:1, :]), 0.0), axis=0, keepdims=True)
        nn = jnp.zeros(s1.shape, F32)
        for a in range(PEER_TOPK):
            blk = chosen[a * PEER_TOPK:(a + 1) * PEER_TOPK, :]
            n_a = jnp.sum(jnp.where(blk, 1.0, 0.0), axis=0, keepdims=True)
            nn = jnp.where(rank1 == float(a), n_a, nn)
        top = float(PEER_TOPK)
        r2_ref[h] = rank2.astype(BF16)
        nn_ref[h] = nn
        e2_ref[h] = jnp.where(rank2 < top, jnp.exp(s2 - v2[0]), 0.0).astype(BF16)
        cc_ref[h] = jnp.where(rank1 < top, jnp.exp(s1 - v1[0]), 0.0) / z
        return carry

    lax.fori_loop(0, PEER_HEADS, body, 0)


def _topk(st):
    nhp, nkeys, n = st.shape
    tl = LANES
    out_spec = pl.BlockSpec((PEER_HEADS, nkeys, tl), lambda i: (0, 0, i))
    sds = lambda dt: jax.ShapeDtypeStruct((PEER_HEADS, nkeys, n), dt)
    return pl.pallas_call(
        _topk_kernel,
        grid=(n // tl,),
        in_specs=[pl.BlockSpec((nhp, nkeys, tl), lambda i: (0, 0, i))],
        out_specs=(out_spec, out_spec, out_spec, out_spec),
        out_shape=(sds(BF16), sds(F32), sds(BF16), sds(F32)),
        compiler_params=_cparams(("parallel",)),
        name="peer_topk",
    )(st)


def _gelu_tanh(x):
    return 0.5 * x * (1.0 + jnp.tanh(0.7978845608028654 * (x + 0.044715 * (x * x * x))))


def _peer_kernel(h2_ref, x1_ref, u_ref, vt_ref, r2_ref, e2_ref, nn_ref, cc_ref, o_ref, acc_ref, gw_ref, *, ni):
    ei = pl.program_id(1)

    @pl.when(ei == 0)
    def _():
        acc_ref[...] = jnp.zeros(acc_ref.shape, F32)

    a_t = lax.dot_general(u_ref[...], h2_ref[...], (((1,), (1,)), ((), ())), preferred_element_type=F32)
    for i in range(ni):
        w = None
        for h in range(PEER_HEADS):
            keep = r2_ref[h] < nn_ref[h, i:i + 1, :].astype(BF16)
            term = jnp.where(keep, e2_ref[h] * cc_ref[h, i:i + 1, :].astype(BF16), jnp.zeros((), BF16))
            w = term if w is None else w + term
        g = _gelu_tanh(a_t[i * PEER_KEYS:(i + 1) * PEER_KEYS, :])
        gw_ref[i * PEER_KEYS:(i + 1) * PEER_KEYS, :] = (g * w.astype(F32)).astype(BF16)
    acc_ref[...] += jnp.dot(vt_ref[...], gw_ref[...], preferred_element_type=F32)

    @pl.when(ei == pl.num_programs(1) - 1)
    def _():
        o_ref[...] = x1_ref[...] + acc_ref[...].T


def _peer(h2, x1, r2, nn, e2, cc, u_tab, vt_tab):
    n, d = h2.shape
    ne = u_tab.shape[0]
    tm = min(512, n)
    te = 1024
    ni = te // PEER_KEYS
    tok_spec = pl.BlockSpec((tm, d), lambda ti, ei: (ti, 0))
    key_spec = pl.BlockSpec((PEER_HEADS, PEER_KEYS, tm), lambda ti, ei: (0, 0, ti))
    row_spec = pl.BlockSpec((PEER_HEADS, ni, tm), lambda ti, ei: (0, ei, ti))
    return pl.pallas_call(
        functools.partial(_peer_kernel, ni=ni),
        grid=(n // tm, ne // te),
        in_specs=[tok_spec, tok_spec, pl.BlockSpec((te, d), lambda ti, ei: (ei, 0)),
                  pl.BlockSpec((d, te), lambda ti, ei: (0, ei)), key_spec, key_spec, row_spec, row_spec],
        out_specs=tok_spec,
        out_shape=jax.ShapeDtypeStruct((n, d), F32),
        scratch_shapes=[pltpu.VMEM((d, tm), F32), pltpu.VMEM((te, tm), BF16)],
        compiler_params=_cparams(("parallel", "arbitrary")),
        name="peer_experts",
    )(h2, x1, u_tab, vt_tab, r2, e2, nn, cc)


def _prep_weights(l, norm_mix, w_in, conv_w, conv_b, conv_norm_g, conv_norm_b, w_conv_out, forget_bias, q_norm,
                  k_norm, w_attn_out, w_out, norm_ffn, peer_w_q, peer_keys, peer_u, peer_v):
    o1 = 2 * D_CONV
    o2 = o1 + 3 * D_MODEL
    o3 = o2 + N_HEADS
    wl = w_in[l]
    head_of_col = jnp.arange(D_MODEL) // HEAD_DIM
    seg = (head_of_col[:, None] == jnp.arange(LANES)[None, :]).astype(BF16)
    row = lambda a: a.reshape(1, -1).astype(F32)
    return {
        "norm_mix": row(norm_mix[l]),
        "w_glu": wl[:, :o1].astype(BF16),
        "w_qkv": wl[:, o1:o2].astype(BF16),
        "w_f": jnp.pad(wl[:, o2:o3], ((0, 0), (0, LANES - N_HEADS))).astype(BF16),
        "w_gate": wl[:, o3:].astype(BF16),
        "f_bias": jnp.pad(row(forget_bias[l]), ((0, 0), (0, LANES - N_HEADS))),
        "q_gain": row(jnp.tile(q_norm[l], N_HEADS) * (HEAD_DIM ** -0.5)),
        "k_gain": row(jnp.tile(k_norm[l], N_HEADS)),
        "seg": seg,
        "segt": seg.T,
        "conv_w": jnp.pad(conv_w[l], ((0, HIST_ROWS - CONV_WIDTH), (0, 0))),
        "conv_b": row(conv_b[l]),
        "ln_g": row(conv_norm_g[l]),
        "ln_b": row(conv_norm_b[l]),
        "w_conv_out": w_conv_out[l].astype(BF16),
        "w_attn_out": w_attn_out[l].astype(BF16),
        "w_out": w_out[l].astype(BF16),
        "norm_ffn": row(norm_ffn[l]),
        "peer_w_q": peer_w_q[l].astype(BF16),
        "peer_keys": jnp.pad(peer_keys[l].reshape(2 * PEER_HEADS, PEER_KEYS, -1).astype(BF16),
                             ((0, 0), (0, 0), (0, LANES - peer_keys.shape[-1]))),
        "peer_u": peer_u[l].astype(BF16),
        "peer_vt": peer_v[l].T.astype(BF16),
    }


def _layer_group(x3, conv_state, cache, w):
    b, t, d = x3.shape
    n = b * t
    assert t >= CONV_STATE
    x = x3.reshape(n, d)
    u, qb, k, v, kb, vb, logf, gate = _inproj(x, w)
    u3 = u.reshape(b, t, D_CONV)
    if conv_state is None:
        buf = jnp.zeros((b, HIST_ROWS, D_CONV), F32)
    else:
        buf = jnp.pad(conv_state, ((0, 0), (HIST_ROWS - CONV_STATE, 0), (0, 0)))
    cact = _conv(u3, buf, w)

    lf = jnp.transpose(logf.reshape(b, t, N_HEADS), (0, 2, 1))
    if cache is not None:
        cache_k, cache_v, cache_logf, layer = cache
        lf = jnp.concatenate([jnp.transpose(cache_logf[layer].astype(F32), (0, 2, 1)), lf], axis=2)
    total = lf.shape[2]
    padded = -(-total // LANES) * LANES
    c = _cumsum_lanes(jnp.pad(lf, ((0, 0), (0, 0), (0, padded - total))).reshape(b * N_HEADS, padded))
    c = c.reshape(b, N_HEADS // 2, 2, padded)
    q3, k3, v3 = (a.reshape(b, t, d) for a in (qb, kb, vb))
    if cache is None:
        attn = _attn_prompt(q3, k3, v3, c[..., :t])
    else:
        past = total - t
        ck = cache_k.reshape(cache_k.shape[0], b, past, d)
        cv = cache_v.reshape(cache_v.shape[0], b, past, d)
        attn = _attn_sample(q3, k3, v3, ck, cv, layer, c[..., :past], c[..., past:total])

    x1, h2, st = _outproj(attn.reshape(n, d), cact.reshape(n, D_CONV), gate, x, w)
    r2, nn, e2, cc = _topk(st)
    y = _peer(h2, x1, r2, nn, e2, cc, w["peer_u"], w["peer_vt"])
    return (y.reshape(b, t, d), k.reshape(b, t, N_HEADS, HEAD_DIM), v.reshape(b, t, N_HEADS, HEAD_DIM),
            logf.reshape(b, t, N_HEADS), u3[:, t - CONV_STATE:, :])


def kernel(x_prompt, x_sample, cache_k, cache_v, cache_logf, state_conv, norm_mix, w_in, conv_w, conv_b, conv_norm_g, conv_norm_b, w_conv_out, forget_bias, q_norm, k_norm, w_attn_out, w_out, norm_ffn, peer_w_q, peer_keys, peer_u, peer_v):
    depth = w_in.shape[0]
    yp, ys = x_prompt, x_sample
    outs_p, outs_s = [], []
    for l in range(depth):
        w = _prep_weights(l, norm_mix, w_in, conv_w, conv_b, conv_norm_g, conv_norm_b, w_conv_out, forget_bias,
                          q_norm, k_norm, w_attn_out, w_out, norm_ffn, peer_w_q, peer_keys, peer_u, peer_v)
        yp, *rest_p = _layer_group(yp, None, None, w)
        ys, *rest_s = _layer_group(ys, state_conv[l], (cache_k, cache_v, cache_logf, l), w)
        outs_p.append(rest_p)
        outs_s.append(rest_s)
    stack = lambda outs, i: jnp.stack([o[i] for o in outs])
    return (yp, ys, stack(outs_p, 0), stack(outs_p, 1), stack(outs_p, 2), stack(outs_p, 3),
            stack(outs_s, 0), stack(outs_s, 1), stack(outs_s, 2), stack(outs_s, 3))
```

```python
import functools
import math

import jax
import jax.numpy as jnp
from jax import lax
from jax.experimental import pallas as pl
from jax.experimental.pallas import tpu as pltpu

F32 = jnp.float32
BF16 = jnp.bfloat16

D_MODEL = 1024
N_HEADS = 16
HEAD_DIM = 64
D_CONV = 512
CONV_WIDTH = 31
CONV_STATE = CONV_WIDTH - 1
PEER_HEADS = 8
PEER_KEYS = 128
PEER_TOPK = 16
EPS = 1e-6
LOG2E = math.log2(math.e)
LANES = 128
HIST_ROWS = 32
BIAS_PIECES = 3
VMEM_LIMIT = 56 * 1024 * 1024


def _cparams(sem):
    return pltpu.CompilerParams(dimension_semantics=sem, vmem_limit_bytes=VMEM_LIMIT)


def _const_spec(shape):
    n = len(shape)
    return pl.BlockSpec(shape, lambda *_: (0,) * n)


def _split3(x):
    hi = x.astype(BF16)
    r1 = x - hi.astype(F32)
    mid = r1.astype(BF16)
    lo = (r1 - mid.astype(F32)).astype(BF16)
    return hi, mid, lo


def _split_dot(x, w):
    return sum(jnp.dot(p, w, preferred_element_type=F32) for p in _split3(x))


def _log_sigmoid(x):
    return jnp.minimum(x, 0.0) - jnp.log(1.0 + jnp.exp(-jnp.abs(x)))


def _inproj_kernel(x_ref, g_ref, wglu_ref, wqkv_ref, wf_ref, wgate_ref, fb_ref, qg_ref, kg_ref, seg_ref, segt_ref,
                   u_ref, qb_ref, qt_ref, k_ref, v_ref, kb_ref, vb_ref, vt_ref, logf_ref, gate_ref):
    x = x_ref[...]
    h = x * lax.rsqrt(jnp.mean(x * x, axis=-1, keepdims=True) + EPS) * g_ref[...]
    hb = h.astype(BF16)
    glu = jnp.dot(hb, wglu_ref[...], preferred_element_type=F32)
    u_ref[...] = glu[:, :D_CONV] * jax.nn.sigmoid(glu[:, D_CONV:])
    qkv = jnp.dot(hb, wqkv_ref[...], preferred_element_type=F32)

    def head_norm(t, gain):
        ss = _split_dot(t * t, seg_ref[...])
        r = lax.rsqrt(ss * (1.0 / HEAD_DIM) + EPS)
        return t * _split_dot(r, segt_ref[...]) * gain

    q = head_norm(qkv[:, :D_MODEL], qg_ref[...])
    k = head_norm(qkv[:, D_MODEL:2 * D_MODEL], kg_ref[...])
    v = qkv[:, 2 * D_MODEL:]
    qb_ref[...] = q.astype(BF16)
    qt_ref[...] = q.T.astype(BF16)
    k_ref[...] = k
    kb_ref[...] = k.astype(BF16)
    v_ref[...] = v
    vb_ref[...] = v.astype(BF16)
    vt_ref[...] = v.T.astype(BF16)
    f = jnp.dot(hb, wf_ref[...], preferred_element_type=F32) + fb_ref[...]
    logf_ref[...] = _log_sigmoid(f)[:, :N_HEADS]
    gate_ref[...] = jax.nn.sigmoid(jnp.dot(hb, wgate_ref[...], preferred_element_type=F32))


def _inproj(x, w):
    n = x.shape[0]
    tm = min(256, n)
    tok = lambda c: pl.BlockSpec((tm, c), lambda i: (i, 0))
    tok_t = pl.BlockSpec((D_MODEL, tm), lambda i: (0, i))
    sds = jax.ShapeDtypeStruct
    out_shape = (
        sds((n, D_CONV), F32),
        sds((n, D_MODEL), BF16),
        sds((D_MODEL, n), BF16),
        sds((n, D_MODEL), F32),
        sds((n, D_MODEL), F32),
        sds((n, D_MODEL), BF16),
        sds((n, D_MODEL), BF16),
        sds((D_MODEL, n), BF16),
        sds((n, N_HEADS), F32),
        sds((n, 2 * D_MODEL), F32),
    )
    return pl.pallas_call(
        _inproj_kernel,
        grid=(n // tm,),
        in_specs=[tok(D_MODEL), _const_spec((1, D_MODEL)), _const_spec((D_MODEL, 2 * D_CONV)),
                  _const_spec((D_MODEL, 3 * D_MODEL)), _const_spec((D_MODEL, LANES)),
                  _const_spec((D_MODEL, 2 * D_MODEL)), _const_spec((1, LANES)), _const_spec((1, D_MODEL)),
                  _const_spec((1, D_MODEL)), _const_spec((D_MODEL, LANES)), _const_spec((LANES, D_MODEL))],
        out_specs=(tok(D_CONV), tok(D_MODEL), tok_t, tok(D_MODEL), tok(D_MODEL), tok(D_MODEL), tok(D_MODEL), tok_t,
                   tok(N_HEADS), tok(2 * D_MODEL)),
        out_shape=out_shape,
        compiler_params=_cparams(("parallel",)),
        name="inproj",
    )(x, w["norm_mix"], w["w_glu"], w["w_qkv"], w["w_f"], w["w_gate"], w["f_bias"], w["q_gain"], w["k_gain"],
      w["seg"], w["segt"])


CONV_ROWS = 32


def _conv_kernel(u_ref, halo_ref, buf_ref, w_ref, b_ref, lng_ref, lnb_ref, o_ref, pad_ref, *, tt):
    first = pl.program_id(1) == 0
    pad_ref[0:HIST_ROWS, :] = jnp.where(first, buf_ref[...], halo_ref[...])
    pad_ref[HIST_ROWS:HIST_ROWS + tt, :] = u_ref[...]
    off = HIST_ROWS - CONV_STATE
    for r in range(tt // CONV_ROWS):
        base = r * CONV_ROWS + off
        acc = jnp.broadcast_to(b_ref[...], (CONV_ROWS, D_CONV))
        for k in range(CONV_WIDTH):
            acc = acc + w_ref[k:k + 1, :] * pad_ref[base + k:base + k + CONV_ROWS, :]
        mu = jnp.mean(acc, axis=-1, keepdims=True)
        d = acc - mu
        var = jnp.mean(d * d, axis=-1, keepdims=True)
        y = d * lax.rsqrt(var + EPS) * lng_ref[...] + lnb_ref[...]
        o_ref[r * CONV_ROWS:(r + 1) * CONV_ROWS, :] = (y * jax.nn.sigmoid(y)).astype(BF16)


def _conv(u, buf, w):
    b, t, c = u.shape
    tt = min(256, t)
    hb = tt // HIST_ROWS
    return pl.pallas_call(
        functools.partial(_conv_kernel, tt=tt),
        grid=(b, t // tt),
        in_specs=[pl.BlockSpec((None, tt, c), lambda bi, i: (bi, i, 0)),
                  pl.BlockSpec((None, HIST_ROWS, c), lambda bi, i: (bi, jnp.maximum(i * hb - 1, 0), 0)),
                  pl.BlockSpec((None, HIST_ROWS, c), lambda bi, i: (bi, 0, 0)),
                  _const_spec((HIST_ROWS, c)), _const_spec((1, c)), _const_spec((1, c)), _const_spec((1, c))],
        out_specs=pl.BlockSpec((None, tt, c), lambda bi, i: (bi, i, 0)),
        out_shape=jax.ShapeDtypeStruct((b, t, c), BF16),
        scratch_shapes=[pltpu.VMEM((HIST_ROWS + tt, c), F32)],
        compiler_params=_cparams(("parallel", "arbitrary")),
        name="conv",
    )(u, u, buf, w["conv_w"], w["conv_b"], w["ln_g"], w["ln_b"])


def _cumsum_pieces_kernel(x_ref, tri_ref, place_ref, o_ref, *, nblk):
    def body(j, carry):
        sl = pl.ds(pl.multiple_of(j * LANES, LANES), LANES)
        cs = carry
        for piece in _split3(x_ref[sl, :]):
            cs = cs + jnp.dot(tri_ref[...], piece, preferred_element_type=F32)
        out = None
        for j2, piece in enumerate(_split3(cs * (-LOG2E))):
            term = jnp.dot(piece, place_ref[j2], preferred_element_type=F32)
            out = term if out is None else out + term
        o_ref[sl, :] = out.astype(BF16)
        return cs[LANES - 1:LANES, :]

    lax.fori_loop(0, nblk, body, jnp.zeros((1, x_ref.shape[1]), F32))


def _cumsum_pieces(logf):
    b, t, h = logf.shape
    ii = lax.broadcasted_iota(jnp.int32, (LANES, LANES), 0)
    jj = lax.broadcasted_iota(jnp.int32, (LANES, LANES), 1)
    tri = (jj <= ii).astype(BF16)
    head = jnp.arange(h)
    col = jnp.arange(D_MODEL)
    place = jnp.stack([(col[None, :] == ((head // 2) * LANES + (head % 2) * BIAS_PIECES + j)[:, None]).astype(BF16)
                       for j in range(BIAS_PIECES)])
    return pl.pallas_call(
        functools.partial(_cumsum_pieces_kernel, nblk=t // LANES),
        grid=(b,),
        in_specs=[pl.BlockSpec((None, t, h), lambda i: (i, 0, 0)), _const_spec((LANES, LANES)),
                  _const_spec((BIAS_PIECES, h, D_MODEL))],
        out_specs=pl.BlockSpec((None, t, D_MODEL), lambda i: (i, 0, 0)),
        out_shape=jax.ShapeDtypeStruct((b, t, D_MODEL), BF16),
        compiler_params=_cparams(("parallel",)),
        name="cumsum_pieces",
    )(logf, tri, place)


COL_TILE = 256


def _ones_rows(tq):
    r = lax.broadcasted_iota(jnp.int32, (LANES, 2 * tq), 0)
    c = lax.broadcasted_iota(jnp.int32, (LANES, 2 * tq), 1)
    ones_a = jnp.where(c < tq, jnp.where(r < BIAS_PIECES, 1.0, 0.0), 0.0)
    ones_b = jnp.where(c >= tq, jnp.where(r >= BIAS_PIECES, jnp.where(r < 2 * BIAS_PIECES, 1.0, 0.0), 0.0), 0.0)
    return (ones_a + ones_b).astype(BF16)


def _attn_init(m_ref, l_ref, acc_ref):
    m_ref[...] = jnp.full(m_ref.shape, -jnp.inf, F32)
    l_ref[...] = jnp.zeros(l_ref.shape, F32)
    acc_ref[...] = jnp.zeros(acc_ref.shape, F32)


def _attn_chunk(kaug, vt_c, wq_ref, m_ref, l_ref, acc_ref, tq, diagonal):
    ncol = wq_ref.shape[1]
    cw = min(COL_TILE, ncol)
    tiles = [slice(ct * cw, (ct + 1) * cw) for ct in range(ncol // cw)]
    logits = [jnp.dot(kaug, wq_ref[:, cs], preferred_element_type=F32) for cs in tiles]
    for ct, cs in enumerate(tiles):
        s = logits[ct]
        if diagonal:
            key = lax.broadcasted_iota(jnp.int32, s.shape, 0)
            qry = lax.broadcasted_iota(jnp.int32, s.shape, 1) + ct * cw
            qry = jnp.where(qry >= tq, qry - tq, qry)
            s = jnp.where(key <= qry, s, -jnp.inf)
        m_old = m_ref[:, cs]
        m_new = jnp.maximum(m_old, jnp.max(s, axis=0, keepdims=True))
        alpha = jnp.exp2(m_old - m_new)
        p = jnp.exp2(s - m_new)
        l_ref[:, cs] = alpha * l_ref[:, cs] + jnp.sum(p, axis=0, keepdims=True)
        acc_ref[:, cs] = alpha * acc_ref[:, cs] + jnp.dot(vt_c, p.astype(BF16), preferred_element_type=F32)
        m_ref[:, cs] = m_new


def _attn_finish(o_ref, l_ref, acc_ref, tq):
    ot = (acc_ref[...] / l_ref[...]).T
    lane = lax.broadcasted_iota(jnp.int32, (tq, LANES), 1)
    o_ref[...] = jnp.where(lane < HEAD_DIM, ot[:tq], ot[tq:]).astype(o_ref.dtype)


def _attn_prompt_kernel(qt_ref, k_ref, cp_ref, vt_ref, o_ref, wq_ref, m_ref, l_ref, acc_ref, *, tq):
    qi = pl.program_id(2)
    qt = qt_ref[...]
    row = lax.broadcasted_iota(jnp.int32, qt.shape, 0)
    zero = jnp.zeros_like(qt)
    wq_ref[0:LANES, :] = jnp.concatenate([jnp.where(row < HEAD_DIM, qt, zero), jnp.where(row >= HEAD_DIM, qt, zero)],
                                         axis=1)
    wq_ref[LANES:2 * LANES, :] = _ones_rows(tq)
    _attn_init(m_ref, l_ref, acc_ref)

    def chunk(kc, diagonal):
        ksl = pl.ds(pl.multiple_of(kc * tq, tq), tq)
        kaug = jnp.concatenate([k_ref[ksl, :], cp_ref[ksl, :]], axis=1)
        _attn_chunk(kaug, vt_ref[:, ksl], wq_ref, m_ref, l_ref, acc_ref, tq, diagonal)

    def body(kc, carry):
        chunk(kc, False)
        return carry

    lax.fori_loop(0, qi, body, 0)
    chunk(qi, True)
    _attn_finish(o_ref, l_ref, acc_ref, tq)


def _attn_prompt(qt, kb, cp, vt, b, t):
    d = qt.shape[0]
    tq = min(512, t)
    nq = t // tq
    pairs = d // LANES
    kv_spec = pl.BlockSpec((None, t, LANES), lambda bi, p, qi: (bi, 0, p))
    return pl.pallas_call(
        functools.partial(_attn_prompt_kernel, tq=tq),
        grid=(b, pairs, nq),
        in_specs=[pl.BlockSpec((LANES, tq), lambda bi, p, qi: (p, bi * nq + qi)), kv_spec, kv_spec,
                  pl.BlockSpec((LANES, t), lambda bi, p, qi: (p, bi))],
        out_specs=pl.BlockSpec((None, tq, LANES), lambda bi, p, qi: (bi, qi, p)),
        out_shape=jax.ShapeDtypeStruct((b, t, d), BF16),
        scratch_shapes=[pltpu.VMEM((2 * LANES, 2 * tq), BF16), pltpu.VMEM((1, 2 * tq), F32),
                        pltpu.VMEM((1, 2 * tq), F32), pltpu.VMEM((LANES, 2 * tq), F32)],
        compiler_params=_cparams(("parallel", "parallel", "arbitrary")),
        name="attn_prompt",
    )(qt, kb, cp, vt)


def _attn_sample_kernel(wq_ref, kh_ref, vh_ref, cp_ref, kn_ref, vnt_ref, o_ref, *, past, t):
    wq = wq_ref[...]
    s_hist = jnp.dot(jnp.concatenate([kh_ref[...], cp_ref[0:past, :]], axis=1), wq, preferred_element_type=F32)
    s_new = jnp.dot(jnp.concatenate([kn_ref[...], cp_ref[past:past + t, :]], axis=1), wq, preferred_element_type=F32)
    key = lax.broadcasted_iota(jnp.int32, s_new.shape, 0)
    qry = lax.broadcasted_iota(jnp.int32, s_new.shape, 1)
    s_new = jnp.where(key <= jnp.where(qry >= t, qry - t, qry), s_new, -jnp.inf)
    m = jnp.maximum(jnp.max(s_hist, axis=0, keepdims=True), jnp.max(s_new, axis=0, keepdims=True))
    p_hist = jnp.exp2(s_hist - m)
    p_new = jnp.exp2(s_new - m)
    denom = jnp.sum(p_hist, axis=0, keepdims=True) + jnp.sum(p_new, axis=0, keepdims=True)
    acc = (jnp.dot(vh_ref[...].T, p_hist.astype(BF16), preferred_element_type=F32)
           + jnp.dot(vnt_ref[...], p_new.astype(BF16), preferred_element_type=F32))
    ot = (acc / denom).T
    lane = lax.broadcasted_iota(jnp.int32, (t, LANES), 1)
    o_ref[...] = jnp.where(lane < HEAD_DIM, ot[:t], ot[t:]).astype(o_ref.dtype)


def _attn_sample(wq, kn, vnt, hist_k, hist_v, layer, cp):
    b, t, d = kn.shape
    past = hist_k.shape[2]
    pairs = d // LANES
    hist_spec = pl.BlockSpec((None, None, past, LANES), lambda bi, p: (layer, bi, 0, p))
    return pl.pallas_call(
        functools.partial(_attn_sample_kernel, past=past, t=t),
        grid=(b, pairs),
        in_specs=[pl.BlockSpec((None, None, 2 * LANES, 2 * t), lambda bi, p: (bi, p, 0, 0)), hist_spec, hist_spec,
                  pl.BlockSpec((None, cp.shape[1], LANES), lambda bi, p: (bi, 0, p)),
                  pl.BlockSpec((None, t, LANES), lambda bi, p: (bi, 0, p)),
                  pl.BlockSpec((None, None, LANES, t), lambda bi, p: (bi, p, 0, 0))],
        out_specs=pl.BlockSpec((None, t, LANES), lambda bi, p: (bi, 0, p)),
        out_shape=jax.ShapeDtypeStruct((b, t, d), BF16),
        compiler_params=_cparams(("parallel", "parallel")),
        name="attn_sample",
    )(wq, hist_k, hist_v, cp, kn, vnt)


def _sample_query_weights(qb, b, t):
    pairs = D_MODEL // LANES
    q = jnp.transpose(qb.reshape(b, t, pairs, 2, HEAD_DIM), (0, 2, 3, 4, 1))
    zero = jnp.zeros_like(q[:, :, 0])
    top = jnp.concatenate([jnp.concatenate([q[:, :, 0], zero], axis=-1),
                           jnp.concatenate([zero, q[:, :, 1]], axis=-1)], axis=-2)
    ones = jnp.broadcast_to(_ones_rows(t), (b, pairs, LANES, 2 * t))
    return jnp.concatenate([top, ones], axis=-2)


def _outproj_kernel(attn_ref, cact_ref, gate_ref, x_ref, wco_ref, wao_ref, wo_ref, g_ref, wpq_ref, keys_ref,
                    x1_ref, h2_ref, st_ref):
    conv_out = jnp.dot(cact_ref[...], wco_ref[...], preferred_element_type=F32)
    attn_out = jnp.dot(attn_ref[...], wao_ref[...], preferred_element_type=F32)
    merged = gate_ref[:, :D_MODEL] * conv_out + gate_ref[:, D_MODEL:] * attn_out
    x1 = x_ref[...] + jnp.dot(merged.astype(BF16), wo_ref[...], preferred_element_type=F32)
    x1_ref[...] = x1
    h2 = (x1 * lax.rsqrt(jnp.mean(x1 * x1, axis=-1, keepdims=True) + EPS) * g_ref[...]).astype(BF16)
    h2_ref[...] = h2
    qp = jnp.dot(h2, wpq_ref[...], preferred_element_type=F32).astype(BF16)
    for hp in range(2 * PEER_HEADS):
        st_ref[hp] = lax.dot_general(keys_ref[hp], qp[:, hp * LANES:(hp + 1) * LANES], (((1,), (1,)), ((), ())),
                                     preferred_element_type=F32)


def _outproj(attn, cact, gate, x, w):
    n = x.shape[0]
    tm = min(256, n)
    tok = lambda c: pl.BlockSpec((tm, c), lambda i: (i, 0))
    nhp = 2 * PEER_HEADS
    return pl.pallas_call(
        _outproj_kernel,
        grid=(n // tm,),
        in_specs=[tok(D_MODEL), tok(D_CONV), tok(2 * D_MODEL), tok(D_MODEL),
                  _const_spec((D_CONV, D_MODEL)), _const_spec((D_MODEL, D_MODEL)), _const_spec((D_MODEL, D_MODEL)),
                  _const_spec((1, D_MODEL)), _const_spec((D_MODEL, nhp * LANES)),
                  _const_spec((nhp, PEER_KEYS, LANES))],
        out_specs=(tok(D_MODEL), tok(D_MODEL), pl.BlockSpec((nhp, PEER_KEYS, tm), lambda i: (0, 0, i))),
        out_shape=(jax.ShapeDtypeStruct((n, D_MODEL), F32), jax.ShapeDtypeStruct((n, D_MODEL), BF16),
                   jax.ShapeDtypeStruct((nhp, PEER_KEYS, n), F32)),
        compiler_params=_cparams(("parallel",)),
        name="outproj",
    )(attn, cact, gate, x, w["w_conv_out"], w["w_attn_out"], w["w_out"], w["norm_ffn"], w["peer_w_q"], w["peer_keys"])


def _top_ranks(s, break_ties):
    nrow = s.shape[0]
    iota = lax.broadcasted_iota(jnp.int32, s.shape, 0).astype(F32)
    rank = jnp.full(s.shape, float(PEER_TOPK), F32)
    vals = []
    for r in range(PEER_TOPK):
        m = jnp.max(s, axis=0, keepdims=True)
        sel = s == m
        if break_ties:
            sel = iota == jnp.min(jnp.where(sel, iota, float(nrow)), axis=0, keepdims=True)
        rank = jnp.where(sel, float(r), rank)
        s = jnp.where(sel, -jnp.inf, s)
        vals.append(m)
    count = jnp.sum(jnp.where(rank < float(PEER_TOPK), 1.0, 0.0), axis=0, keepdims=True)
    return rank, vals, count


def _staircase_counts(sv1, sv2):
    top = float(PEER_TOPK)
    iota = lax.broadcasted_iota(jnp.int32, sv1.shape, 0).astype(F32)
    n = jnp.zeros(sv1.shape, F32)
    head = sv1 + sv2[0:1, :]
    m0 = head[0:1, :]
    z = jnp.zeros(m0.shape, F32)
    for _ in range(PEER_TOPK):
        m = jnp.max(head, axis=0, keepdims=True)
        a_star = jnp.min(jnp.where(head == m, iota, top), axis=0, keepdims=True)
        onehot = iota == a_star
        z = z + jnp.exp(m - m0)
        n = n + jnp.where(onehot, 1.0, 0.0)
        n_star = jnp.sum(jnp.where(onehot, n, 0.0), axis=0, keepdims=True)
        nxt = jnp.sum(jnp.where(iota == n_star, sv2, 0.0), axis=0, keepdims=True)
        nxt = jnp.where(n_star >= top, -jnp.inf, nxt)
        a_val = jnp.sum(jnp.where(onehot, sv1, 0.0), axis=0, keepdims=True)
        head = jnp.where(onehot, a_val + nxt, head)
    return n, z


def _topk_head(st_ref, r2_ref, nn_ref, e2_ref, cc_ref, h, break_ties):
    top = float(PEER_TOPK)
    s1 = st_ref[2 * h]
    s2 = st_ref[2 * h + 1]
    rank1, v1, count1 = _top_ranks(s1, break_ties)
    rank2, v2, count2 = _top_ranks(s2, break_ties)
    n, z = _staircase_counts(jnp.concatenate(v1, axis=0), jnp.concatenate(v2, axis=0))
    nn = jnp.zeros(s1.shape, F32)
    for a in range(PEER_TOPK):
        nn = jnp.where(rank1 == float(a), n[a:a + 1, :], nn)
    r2_ref[h] = rank2.astype(BF16)
    nn_ref[h] = nn
    e2_ref[h] = jnp.where(rank2 < top, jnp.exp(s2 - v2[0]), 0.0).astype(BF16)
    cc_ref[h] = jnp.where(rank1 < top, jnp.exp(s1 - v1[0]), 0.0) / z
    return jnp.max(jnp.abs(count1 - top) + jnp.abs(count2 - top)) > 0.0


def _topk_kernel(st_ref, r2_ref, nn_ref, e2_ref, cc_ref):
    def body(i, carry):
        tied_a = _topk_head(st_ref, r2_ref, nn_ref, e2_ref, cc_ref, 2 * i, False)
        tied_b = _topk_head(st_ref, r2_ref, nn_ref, e2_ref, cc_ref, 2 * i + 1, False)

        @pl.when(jnp.logical_or(tied_a, tied_b))
        def _():
            _topk_head(st_ref, r2_ref, nn_ref, e2_ref, cc_ref, 2 * i, True)
            _topk_head(st_ref, r2_ref, nn_ref, e2_ref, cc_ref, 2 * i + 1, True)

        return carry

    lax.fori_loop(0, PEER_HEADS // 2, body, 0)


def _topk(st):
    nhp, nkeys, n = st.shape
    tl = LANES
    out_spec = pl.BlockSpec((PEER_HEADS, nkeys, tl), lambda i: (0, 0, i))
    sds = lambda dt: jax.ShapeDtypeStruct((PEER_HEADS, nkeys, n), dt)
    return pl.pallas_call(
        _topk_kernel,
        grid=(n // tl,),
        in_specs=[pl.BlockSpec((nhp, nkeys, tl), lambda i: (0, 0, i))],
        out_specs=(out_spec, out_spec, out_spec, out_spec),
        out_shape=(sds(BF16), sds(F32), sds(BF16), sds(F32)),
        compiler_params=_cparams(("parallel",)),
        name="peer_topk",
    )(st)


def _gelu_tanh(x):
    return 0.5 * x * (1.0 + jnp.tanh(0.7978845608028654 * (x + 0.044715 * (x * x * x))))


def _peer_kernel(h2_ref, x1_ref, u_ref, vt_ref, r2_ref, e2_ref, nn_ref, cc_ref, o_ref, acc_ref, gw_ref, *, ni):
    ei = pl.program_id(1)

    @pl.when(ei == 0)
    def _():
        acc_ref[...] = jnp.zeros(acc_ref.shape, F32)

    a_t = lax.dot_general(u_ref[...], h2_ref[...], (((1,), (1,)), ((), ())), preferred_element_type=F32)
    for i in range(ni):
        w = None
        for h in range(PEER_HEADS):
            keep = r2_ref[h] < nn_ref[h, i:i + 1, :].astype(BF16)
            term = jnp.where(keep, e2_ref[h] * cc_ref[h, i:i + 1, :].astype(BF16), jnp.zeros((), BF16))
            w = term if w is None else w + term
        g = _gelu_tanh(a_t[i * PEER_KEYS:(i + 1) * PEER_KEYS, :])
        gw_ref[i * PEER_KEYS:(i + 1) * PEER_KEYS, :] = (g * w.astype(F32)).astype(BF16)
    acc_ref[...] += jnp.dot(vt_ref[...], gw_ref[...], preferred_element_type=F32)

    @pl.when(ei == pl.num_programs(1) - 1)
    def _():
        o_ref[...] = x1_ref[...] + acc_ref[...].T


def _peer(h2, x1, r2, nn, e2, cc, u_tab, vt_tab):
    n, d = h2.shape
    ne = u_tab.shape[0]
    tm = min(512, n)
    te = 1024
    ni = te // PEER_KEYS
    tok_spec = pl.BlockSpec((tm, d), lambda ti, ei: (ti, 0))
    key_spec = pl.BlockSpec((PEER_HEADS, PEER_KEYS, tm), lambda ti, ei: (0, 0, ti))
    row_spec = pl.BlockSpec((PEER_HEADS, ni, tm), lambda ti, ei: (0, ei, ti))
    return pl.pallas_call(
        functools.partial(_peer_kernel, ni=ni),
        grid=(n // tm, ne // te),
        in_specs=[tok_spec, tok_spec, pl.BlockSpec((te, d), lambda ti, ei: (ei, 0)),
                  pl.BlockSpec((d, te), lambda ti, ei: (0, ei)), key_spec, key_spec, row_spec, row_spec],
        out_specs=tok_spec,
        out_shape=jax.ShapeDtypeStruct((n, d), F32),
        scratch_shapes=[pltpu.VMEM((d, tm), F32), pltpu.VMEM((te, tm), BF16)],
        compiler_params=_cparams(("parallel", "arbitrary")),
        name="peer_experts",
    )(h2, x1, u_tab, vt_tab, r2, e2, nn, cc)


def _prep_weights(l, norm_mix, w_in, conv_w, conv_b, conv_norm_g, conv_norm_b, w_conv_out, forget_bias, q_norm,
                  k_norm, w_attn_out, w_out, norm_ffn, peer_w_q, peer_keys, peer_u, peer_v):
    o1 = 2 * D_CONV
    o2 = o1 + 3 * D_MODEL
    o3 = o2 + N_HEADS
    wl = w_in[l]
    head_of_col = jnp.arange(D_MODEL) // HEAD_DIM
    seg = (head_of_col[:, None] == jnp.arange(LANES)[None, :]).astype(BF16)
    row = lambda a: a.reshape(1, -1).astype(F32)
    return {
        "norm_mix": row(norm_mix[l]),
        "w_glu": wl[:, :o1].astype(BF16),
        "w_qkv": wl[:, o1:o2].astype(BF16),
        "w_f": jnp.pad(wl[:, o2:o3], ((0, 0), (0, LANES - N_HEADS))).astype(BF16),
        "w_gate": wl[:, o3:].astype(BF16),
        "f_bias": jnp.pad(row(forget_bias[l]), ((0, 0), (0, LANES - N_HEADS))),
        "q_gain": row(jnp.tile(q_norm[l], N_HEADS) * (HEAD_DIM ** -0.5 * LOG2E)),
        "k_gain": row(jnp.tile(k_norm[l], N_HEADS)),
        "seg": seg,
        "segt": seg.T,
        "conv_w": jnp.pad(conv_w[l], ((0, HIST_ROWS - CONV_WIDTH), (0, 0))),
        "conv_b": row(conv_b[l]),
        "ln_g": row(conv_norm_g[l]),
        "ln_b": row(conv_norm_b[l]),
        "w_conv_out": w_conv_out[l].astype(BF16),
        "w_attn_out": w_attn_out[l].astype(BF16),
        "w_out": w_out[l].astype(BF16),
        "norm_ffn": row(norm_ffn[l]),
        "peer_w_q": peer_w_q[l].astype(BF16),
        "peer_keys": jnp.pad(peer_keys[l].reshape(2 * PEER_HEADS, PEER_KEYS, -1).astype(BF16),
                             ((0, 0), (0, 0), (0, LANES - peer_keys.shape[-1]))),
        "peer_u": peer_u[l].astype(BF16),
        "peer_vt": peer_v[l].T.astype(BF16),
    }


def _layer_group(x3, conv_state, cache, w):
    b, t, d = x3.shape
    n = b * t
    assert t >= CONV_STATE
    x = x3.reshape(n, d)
    u, qb, qt, k, v, kb, vb, vt, logf, gate = _inproj(x, w)
    u3 = u.reshape(b, t, D_CONV)
    if conv_state is None:
        buf = jnp.zeros((b, HIST_ROWS, D_CONV), F32)
    else:
        buf = jnp.pad(conv_state, ((0, 0), (HIST_ROWS - CONV_STATE, 0), (0, 0)))
    cact = _conv(u3, buf, w)

    logf3 = logf.reshape(b, t, N_HEADS)
    if cache is None:
        cp = _cumsum_pieces(logf3)
        attn = _attn_prompt(qt, kb.reshape(b, t, d), cp, vt, b, t)
    else:
        hist_k, hist_v, cache_logf, layer = cache
        past = hist_k.shape[2]
        total = past + t
        padded = -(-total // LANES) * LANES
        lf = jnp.concatenate([cache_logf[layer].astype(F32), logf3], axis=1)
        cp = _cumsum_pieces(jnp.pad(lf, ((0, 0), (0, padded - total), (0, 0))))
        vnt = jnp.transpose(vb.reshape(b, t, d // LANES, LANES), (0, 2, 3, 1))
        attn = _attn_sample(_sample_query_weights(qb, b, t), kb.reshape(b, t, d), vnt, hist_k, hist_v, layer, cp)

    x1, h2, st = _outproj(attn.reshape(n, d), cact.reshape(n, D_CONV), gate, x, w)
    r2, nn, e2, cc = _topk(st)
    y = _peer(h2, x1, r2, nn, e2, cc, w["peer_u"], w["peer_vt"])
    return (y.reshape(b, t, d), k.reshape(b, t, N_HEADS, HEAD_DIM), v.reshape(b, t, N_HEADS, HEAD_DIM),
            logf3, u3[:, t - CONV_STATE:, :])


def kernel(x_prompt, x_sample, cache_k, cache_v, cache_logf, state_conv, norm_mix, w_in, conv_w, conv_b, conv_norm_g, conv_norm_b, w_conv_out, forget_bias, q_norm, k_norm, w_attn_out, w_out, norm_ffn, peer_w_q, peer_keys, peer_u, peer_v):
    depth = w_in.shape[0]
    yp, ys = x_prompt, x_sample
    outs_p, outs_s = [], []
    flat = cache_k.shape[:3] + (D_MODEL,)
    hist_k = cache_k.astype(BF16).reshape(flat)
    hist_v = cache_v.astype(BF16).reshape(flat)
    for l in range(depth):
        w = _prep_weights(l, norm_mix, w_in, conv_w, conv_b, conv_norm_g, conv_norm_b, w_conv_out, forget_bias,
                          q_norm, k_norm, w_attn_out, w_out, norm_ffn, peer_w_q, peer_keys, peer_u, peer_v)
        yp, *rest_p = _layer_group(yp, None, None, w)
        ys, *rest_s = _layer_group(ys, state_conv[l], (hist_k, hist_v, cache_logf, l), w)
        outs_p.append(rest_p)
        outs_s.append(rest_s)
    stack = lambda outs, i: jnp.stack([o[i] for o in outs])
    return (yp, ys, stack(outs_p, 0), stack(outs_p, 1), stack(outs_p, 2), stack(outs_p, 3),
            stack(outs_s, 0), stack(outs_s, 1), stack(outs_s, 2), stack(outs_s, 3))
```

```python
import functools
import math

import jax
import jax.numpy as jnp
from jax import lax
from jax.experimental import pallas as pl
from jax.experimental.pallas import tpu as pltpu

F32 = jnp.float32
BF16 = jnp.bfloat16

D_MODEL = 1024
N_HEADS = 16
HEAD_DIM = 64
D_CONV = 512
CONV_WIDTH = 31
CONV_STATE = CONV_WIDTH - 1
PEER_HEADS = 8
PEER_KEYS = 128
PEER_TOPK = 16
EPS = 1e-6
LOG2E = math.log2(math.e)
LANES = 128
HIST_ROWS = 32
BIAS_PIECES = 3
VMEM_LIMIT = 56 * 1024 * 1024


def _cparams(sem):
    return pltpu.CompilerParams(dimension_semantics=sem, vmem_limit_bytes=VMEM_LIMIT)


def _const_spec(shape):
    n = len(shape)
    return pl.BlockSpec(shape, lambda *_: (0,) * n)


def _split3(x):
    hi = x.astype(BF16)
    r1 = x - hi.astype(F32)
    mid = r1.astype(BF16)
    lo = (r1 - mid.astype(F32)).astype(BF16)
    return hi, mid, lo


def _split_dot(x, w):
    return sum(jnp.dot(p, w, preferred_element_type=F32) for p in _split3(x))


def _log_sigmoid(x):
    return jnp.minimum(x, 0.0) - jnp.log(1.0 + jnp.exp(-jnp.abs(x)))


def _inproj_kernel(x_ref, g_ref, wglu_ref, wqkv_ref, wf_ref, wgate_ref, fb_ref, qg_ref, kg_ref, seg_ref, segt_ref,
                   u_ref, qb_ref, qt_ref, k_ref, v_ref, kb_ref, vb_ref, vt_ref, logf_ref, gate_ref):
    x = x_ref[...]
    h = x * lax.rsqrt(jnp.mean(x * x, axis=-1, keepdims=True) + EPS) * g_ref[...]
    hb = h.astype(BF16)
    glu = jnp.dot(hb, wglu_ref[...], preferred_element_type=F32)
    u_ref[...] = glu[:, :D_CONV] * jax.nn.sigmoid(glu[:, D_CONV:])
    qkv = jnp.dot(hb, wqkv_ref[...], preferred_element_type=F32)

    def head_norm(t, gain):
        ss = _split_dot(t * t, seg_ref[...])
        r = lax.rsqrt(ss * (1.0 / HEAD_DIM) + EPS)
        return t * _split_dot(r, segt_ref[...]) * gain

    q = head_norm(qkv[:, :D_MODEL], qg_ref[...])
    k = head_norm(qkv[:, D_MODEL:2 * D_MODEL], kg_ref[...])
    v = qkv[:, 2 * D_MODEL:]
    qb_ref[...] = q.astype(BF16)
    qt_ref[...] = q.T.astype(BF16)
    k_ref[...] = k
    kb_ref[...] = k.astype(BF16)
    v_ref[...] = v
    vb_ref[...] = v.astype(BF16)
    vt_ref[...] = v.T.astype(BF16)
    f = jnp.dot(hb, wf_ref[...], preferred_element_type=F32) + fb_ref[...]
    logf_ref[...] = _log_sigmoid(f)[:, :N_HEADS]
    gate_ref[...] = jax.nn.sigmoid(jnp.dot(hb, wgate_ref[...], preferred_element_type=F32))


def _inproj(x, w):
    n = x.shape[0]
    tm = min(256, n)
    tok = lambda c: pl.BlockSpec((tm, c), lambda i: (i, 0))
    tok_t = pl.BlockSpec((D_MODEL, tm), lambda i: (0, i))
    sds = jax.ShapeDtypeStruct
    out_shape = (
        sds((n, D_CONV), F32),
        sds((n, D_MODEL), BF16),
        sds((D_MODEL, n), BF16),
        sds((n, D_MODEL), F32),
        sds((n, D_MODEL), F32),
        sds((n, D_MODEL), BF16),
        sds((n, D_MODEL), BF16),
        sds((D_MODEL, n), BF16),
        sds((n, N_HEADS), F32),
        sds((n, 2 * D_MODEL), F32),
    )
    return pl.pallas_call(
        _inproj_kernel,
        grid=(n // tm,),
        in_specs=[tok(D_MODEL), _const_spec((1, D_MODEL)), _const_spec((D_MODEL, 2 * D_CONV)),
                  _const_spec((D_MODEL, 3 * D_MODEL)), _const_spec((D_MODEL, LANES)),
                  _const_spec((D_MODEL, 2 * D_MODEL)), _const_spec((1, LANES)), _const_spec((1, D_MODEL)),
                  _const_spec((1, D_MODEL)), _const_spec((D_MODEL, LANES)), _const_spec((LANES, D_MODEL))],
        out_specs=(tok(D_CONV), tok(D_MODEL), tok_t, tok(D_MODEL), tok(D_MODEL), tok(D_MODEL), tok(D_MODEL), tok_t,
                   tok(N_HEADS), tok(2 * D_MODEL)),
        out_shape=out_shape,
        compiler_params=_cparams(("parallel",)),
        name="inproj",
    )(x, w["norm_mix"], w["w_glu"], w["w_qkv"], w["w_f"], w["w_gate"], w["f_bias"], w["q_gain"], w["k_gain"],
      w["seg"], w["segt"])


CONV_ROWS = 32


def _conv_kernel(u_ref, halo_ref, buf_ref, w_ref, b_ref, lng_ref, lnb_ref, o_ref, pad_ref, *, tt):
    first = pl.program_id(1) == 0
    pad_ref[0:HIST_ROWS, :] = jnp.where(first, buf_ref[...], halo_ref[...])
    pad_ref[HIST_ROWS:HIST_ROWS + tt, :] = u_ref[...]
    off = HIST_ROWS - CONV_STATE
    for r in range(tt // CONV_ROWS):
        base = r * CONV_ROWS + off
        acc = jnp.broadcast_to(b_ref[...], (CONV_ROWS, D_CONV))
        for k in range(CONV_WIDTH):
            acc = acc + w_ref[k:k + 1, :] * pad_ref[base + k:base + k + CONV_ROWS, :]
        mu = jnp.mean(acc, axis=-1, keepdims=True)
        d = acc - mu
        var = jnp.mean(d * d, axis=-1, keepdims=True)
        y = d * lax.rsqrt(var + EPS) * lng_ref[...] + lnb_ref[...]
        o_ref[r * CONV_ROWS:(r + 1) * CONV_ROWS, :] = (y * jax.nn.sigmoid(y)).astype(BF16)


def _conv(u, buf, w):
    b, t, c = u.shape
    tt = min(256, t)
    hb = tt // HIST_ROWS
    return pl.pallas_call(
        functools.partial(_conv_kernel, tt=tt),
        grid=(b, t // tt),
        in_specs=[pl.BlockSpec((None, tt, c), lambda bi, i: (bi, i, 0)),
                  pl.BlockSpec((None, HIST_ROWS, c), lambda bi, i: (bi, jnp.maximum(i * hb - 1, 0), 0)),
                  pl.BlockSpec((None, HIST_ROWS, c), lambda bi, i: (bi, 0, 0)),
                  _const_spec((HIST_ROWS, c)), _const_spec((1, c)), _const_spec((1, c)), _const_spec((1, c))],
        out_specs=pl.BlockSpec((None, tt, c), lambda bi, i: (bi, i, 0)),
        out_shape=jax.ShapeDtypeStruct((b, t, c), BF16),
        scratch_shapes=[pltpu.VMEM((HIST_ROWS + tt, c), F32)],
        compiler_params=_cparams(("parallel", "arbitrary")),
        name="conv",
    )(u, u, buf, w["conv_w"], w["conv_b"], w["ln_g"], w["ln_b"])


def _cumsum_pieces_kernel(x_ref, tri_ref, place_ref, o_ref, *, nblk):
    def body(j, carry):
        sl = pl.ds(pl.multiple_of(j * LANES, LANES), LANES)
        local = sum(jnp.dot(tri_ref[...], piece, preferred_element_type=F32) for piece in _split3(x_ref[sl, :]))
        cs = local + carry
        out = None
        for j2, piece in enumerate(_split3(cs * (-LOG2E))):
            term = jnp.dot(piece, place_ref[j2], preferred_element_type=F32)
            out = term if out is None else out + term
        o_ref[sl, :] = out.astype(BF16)
        return cs[LANES - 1:LANES, :]

    unroll = next(u for u in (4, 3, 2, 1) if nblk % u == 0)
    lax.fori_loop(0, nblk, body, jnp.zeros((1, x_ref.shape[1]), F32), unroll=unroll)


def _cumsum_pieces(logf):
    b, t, h = logf.shape
    ii = lax.broadcasted_iota(jnp.int32, (LANES, LANES), 0)
    jj = lax.broadcasted_iota(jnp.int32, (LANES, LANES), 1)
    tri = (jj <= ii).astype(BF16)
    head = jnp.arange(h)
    col = jnp.arange(LANES)
    place = jnp.stack([(col[None, :] == (head * BIAS_PIECES + j)[:, None]).astype(BF16)
                       for j in range(BIAS_PIECES)])
    return pl.pallas_call(
        functools.partial(_cumsum_pieces_kernel, nblk=t // LANES),
        grid=(b,),
        in_specs=[pl.BlockSpec((None, t, h), lambda i: (i, 0, 0)), _const_spec((LANES, LANES)),
                  _const_spec((BIAS_PIECES, h, LANES))],
        out_specs=pl.BlockSpec((None, t, LANES), lambda i: (i, 0, 0)),
        out_shape=jax.ShapeDtypeStruct((b, t, LANES), BF16),
        compiler_params=_cparams(("parallel",)),
        name="cumsum_pieces",
    )(logf, tri, place)


COL_TILE = 256


def _ones_rows(tq, pair):
    r = lax.broadcasted_iota(jnp.int32, (LANES, 2 * tq), 0) - 2 * BIAS_PIECES * pair
    c = lax.broadcasted_iota(jnp.int32, (LANES, 2 * tq), 1)
    first = jnp.where(c < tq, 0, BIAS_PIECES)
    return jnp.where(r >= first, jnp.where(r < first + BIAS_PIECES, 1.0, 0.0), 0.0).astype(BF16)


def _attn_init(m_ref, l_ref, acc_ref):
    m_ref[...] = jnp.full(m_ref.shape, -jnp.inf, F32)
    l_ref[...] = jnp.zeros(l_ref.shape, F32)
    acc_ref[...] = jnp.zeros(acc_ref.shape, F32)


def _attn_chunk(kaug, vt_c, wq_ref, m_ref, l_ref, acc_ref, tq, diagonal):
    ncol = wq_ref.shape[1]
    cw = min(COL_TILE, ncol)
    tiles = [slice(ct * cw, (ct + 1) * cw) for ct in range(ncol // cw)]
    logits = [jnp.dot(kaug, wq_ref[:, cs], preferred_element_type=F32) for cs in tiles]
    for ct, cs in enumerate(tiles):
        s = logits[ct]
        if diagonal:
            key = lax.broadcasted_iota(jnp.int32, s.shape, 0)
            qry = lax.broadcasted_iota(jnp.int32, s.shape, 1) + ct * cw
            qry = jnp.where(qry >= tq, qry - tq, qry)
            s = jnp.where(key <= qry, s, -jnp.inf)
        m_old = m_ref[:, cs]
        m_new = jnp.maximum(m_old, jnp.max(s, axis=0, keepdims=True))
        alpha = jnp.exp2(m_old - m_new)
        p = jnp.exp2(s - m_new)
        l_ref[:, cs] = alpha * l_ref[:, cs] + jnp.sum(p, axis=0, keepdims=True)
        acc_ref[:, cs] = alpha * acc_ref[:, cs] + jnp.dot(vt_c, p.astype(BF16), preferred_element_type=F32)
        m_ref[:, cs] = m_new


def _attn_finish(o_ref, l_ref, acc_ref, tq):
    ot = (acc_ref[...] / l_ref[...]).T
    lane = lax.broadcasted_iota(jnp.int32, (tq, LANES), 1)
    o_ref[...] = jnp.where(lane < HEAD_DIM, ot[:tq], ot[tq:]).astype(o_ref.dtype)


def _attn_prompt_kernel(qt_ref, k_ref, cp_ref, vt_ref, o_ref, wq_ref, m_ref, l_ref, acc_ref, *, tq):
    qi = pl.program_id(2)
    qt = qt_ref[...]
    row = lax.broadcasted_iota(jnp.int32, qt.shape, 0)
    zero = jnp.zeros_like(qt)
    wq_ref[0:LANES, :] = jnp.concatenate([jnp.where(row < HEAD_DIM, qt, zero), jnp.where(row >= HEAD_DIM, qt, zero)],
                                         axis=1)
    wq_ref[LANES:2 * LANES, :] = _ones_rows(tq, pl.program_id(1))
    _attn_init(m_ref, l_ref, acc_ref)

    def chunk(kc, diagonal):
        ksl = pl.ds(pl.multiple_of(kc * tq, tq), tq)
        kaug = jnp.concatenate([k_ref[ksl, :], cp_ref[ksl, :]], axis=1)
        _attn_chunk(kaug, vt_ref[:, ksl], wq_ref, m_ref, l_ref, acc_ref, tq, diagonal)

    def body(kc, carry):
        chunk(kc, False)
        return carry

    lax.fori_loop(0, qi, body, 0)
    chunk(qi, True)
    _attn_finish(o_ref, l_ref, acc_ref, tq)


def _attn_prompt(qt, kb, cp, vt, b, t):
    d = qt.shape[0]
    tq = min(512, t)
    nq = t // tq
    pairs = d // LANES
    return pl.pallas_call(
        functools.partial(_attn_prompt_kernel, tq=tq),
        grid=(b, pairs, nq),
        in_specs=[pl.BlockSpec((LANES, tq), lambda bi, p, qi: (p, bi * nq + qi)),
                  pl.BlockSpec((None, t, LANES), lambda bi, p, qi: (bi, 0, p)),
                  pl.BlockSpec((None, t, LANES), lambda bi, p, qi: (bi, 0, 0)),
                  pl.BlockSpec((LANES, t), lambda bi, p, qi: (p, bi))],
        out_specs=pl.BlockSpec((None, tq, LANES), lambda bi, p, qi: (bi, qi, p)),
        out_shape=jax.ShapeDtypeStruct((b, t, d), BF16),
        scratch_shapes=[pltpu.VMEM((2 * LANES, 2 * tq), BF16), pltpu.VMEM((1, 2 * tq), F32),
                        pltpu.VMEM((1, 2 * tq), F32), pltpu.VMEM((LANES, 2 * tq), F32)],
        compiler_params=_cparams(("parallel", "parallel", "arbitrary")),
        name="attn_prompt",
    )(qt, kb, cp, vt)


def _attn_sample_kernel(wq_ref, kh_ref, vh_ref, cp_ref, kn_ref, vnt_ref, o_ref, *, past, t):
    wq = wq_ref[...]
    s_hist = jnp.dot(jnp.concatenate([kh_ref[...], cp_ref[0:past, :]], axis=1), wq, preferred_element_type=F32)
    s_new = jnp.dot(jnp.concatenate([kn_ref[...], cp_ref[past:past + t, :]], axis=1), wq, preferred_element_type=F32)
    key = lax.broadcasted_iota(jnp.int32, s_new.shape, 0)
    qry = lax.broadcasted_iota(jnp.int32, s_new.shape, 1)
    s_new = jnp.where(key <= jnp.where(qry >= t, qry - t, qry), s_new, -jnp.inf)
    m = jnp.maximum(jnp.max(s_hist, axis=0, keepdims=True), jnp.max(s_new, axis=0, keepdims=True))
    p_hist = jnp.exp2(s_hist - m)
    p_new = jnp.exp2(s_new - m)
    denom = jnp.sum(p_hist, axis=0, keepdims=True) + jnp.sum(p_new, axis=0, keepdims=True)
    acc = (jnp.dot(vh_ref[...].T, p_hist.astype(BF16), preferred_element_type=F32)
           + jnp.dot(vnt_ref[...], p_new.astype(BF16), preferred_element_type=F32))
    ot = (acc / denom).T
    lane = lax.broadcasted_iota(jnp.int32, (t, LANES), 1)
    o_ref[...] = jnp.where(lane < HEAD_DIM, ot[:t], ot[t:]).astype(o_ref.dtype)


def _attn_sample(wq, kn, vnt, hist_k, hist_v, layer, cp):
    b, t, d = kn.shape
    past = hist_k.shape[2]
    pairs = d // LANES
    hist_spec = pl.BlockSpec((None, None, past, LANES), lambda bi, p: (layer, bi, 0, p))
    return pl.pallas_call(
        functools.partial(_attn_sample_kernel, past=past, t=t),
        grid=(b, pairs),
        in_specs=[pl.BlockSpec((None, None, 2 * LANES, 2 * t), lambda bi, p: (bi, p, 0, 0)), hist_spec, hist_spec,
                  pl.BlockSpec((None, cp.shape[1], LANES), lambda bi, p: (bi, 0, 0)),
                  pl.BlockSpec((None, t, LANES), lambda bi, p: (bi, 0, p)),
                  pl.BlockSpec((None, None, LANES, t), lambda bi, p: (bi, p, 0, 0))],
        out_specs=pl.BlockSpec((None, t, LANES), lambda bi, p: (bi, 0, p)),
        out_shape=jax.ShapeDtypeStruct((b, t, d), BF16),
        compiler_params=_cparams(("parallel", "parallel")),
        name="attn_sample",
    )(wq, hist_k, hist_v, cp, kn, vnt)


def _sample_query_weights(qb, b, t):
    pairs = D_MODEL // LANES
    q = jnp.transpose(qb.reshape(b, t, pairs, 2, HEAD_DIM), (0, 2, 3, 4, 1))
    zero = jnp.zeros_like(q[:, :, 0])
    top = jnp.concatenate([jnp.concatenate([q[:, :, 0], zero], axis=-1),
                           jnp.concatenate([zero, q[:, :, 1]], axis=-1)], axis=-2)
    ones = jnp.broadcast_to(jnp.stack([_ones_rows(t, p) for p in range(pairs)]), (b, pairs, LANES, 2 * t))
    return jnp.concatenate([top, ones], axis=-2)


def _outproj_kernel(attn_ref, cact_ref, gate_ref, x_ref, wco_ref, wao_ref, wo_ref, g_ref, wpq_ref, keys_ref,
                    x1_ref, h2_ref, st_ref):
    conv_out = jnp.dot(cact_ref[...], wco_ref[...], preferred_element_type=F32)
    attn_out = jnp.dot(attn_ref[...], wao_ref[...], preferred_element_type=F32)
    merged = gate_ref[:, :D_MODEL] * conv_out + gate_ref[:, D_MODEL:] * attn_out
    x1 = x_ref[...] + jnp.dot(merged.astype(BF16), wo_ref[...], preferred_element_type=F32)
    x1_ref[...] = x1
    h2 = (x1 * lax.rsqrt(jnp.mean(x1 * x1, axis=-1, keepdims=True) + EPS) * g_ref[...]).astype(BF16)
    h2_ref[...] = h2
    qp = jnp.dot(h2, wpq_ref[...], preferred_element_type=F32).astype(BF16)
    for hp in range(2 * PEER_HEADS):
        st_ref[hp] = lax.dot_general(keys_ref[hp], qp[:, hp * LANES:(hp + 1) * LANES], (((1,), (1,)), ((), ())),
                                     preferred_element_type=F32)


def _outproj(attn, cact, gate, x, w):
    n = x.shape[0]
    tm = min(256, n)
    tok = lambda c: pl.BlockSpec((tm, c), lambda i: (i, 0))
    nhp = 2 * PEER_HEADS
    return pl.pallas_call(
        _outproj_kernel,
        grid=(n // tm,),
        in_specs=[tok(D_MODEL), tok(D_CONV), tok(2 * D_MODEL), tok(D_MODEL),
                  _const_spec((D_CONV, D_MODEL)), _const_spec((D_MODEL, D_MODEL)), _const_spec((D_MODEL, D_MODEL)),
                  _const_spec((1, D_MODEL)), _const_spec((D_MODEL, nhp * LANES)),
                  _const_spec((nhp, PEER_KEYS, LANES))],
        out_specs=(tok(D_MODEL), tok(D_MODEL), pl.BlockSpec((nhp, PEER_KEYS, tm), lambda i: (0, 0, i))),
        out_shape=(jax.ShapeDtypeStruct((n, D_MODEL), F32), jax.ShapeDtypeStruct((n, D_MODEL), BF16),
                   jax.ShapeDtypeStruct((nhp, PEER_KEYS, n), F32)),
        compiler_params=_cparams(("parallel",)),
        name="outproj",
    )(attn, cact, gate, x, w["w_conv_out"], w["w_attn_out"], w["w_out"], w["norm_ffn"], w["peer_w_q"], w["peer_keys"])


def _top_ranks(s, break_ties):
    nrow = s.shape[0]
    iota = lax.broadcasted_iota(jnp.int32, s.shape, 0).astype(F32)
    rank = jnp.full(s.shape, float(PEER_TOPK), F32)
    vals = []
    for r in range(PEER_TOPK):
        m = jnp.max(s, axis=0, keepdims=True)
        sel = s == m
        if break_ties:
            sel = iota == jnp.min(jnp.where(sel, iota, float(nrow)), axis=0, keepdims=True)
        rank = jnp.where(sel, float(r), rank)
        s = jnp.where(sel, -jnp.inf, s)
        vals.append(m)
    count = jnp.sum(jnp.where(rank < float(PEER_TOPK), 1.0, 0.0), axis=0, keepdims=True)
    return rank, vals, count


def _staircase_counts(sv1, sv2):
    top = float(PEER_TOPK)
    iota = lax.broadcasted_iota(jnp.int32, sv1.shape, 0).astype(F32)
    n = jnp.zeros(sv1.shape, F32)
    head = sv1 + sv2[0:1, :]
    m0 = head[0:1, :]
    z = jnp.zeros(m0.shape, F32)
    for _ in range(PEER_TOPK):
        m = jnp.max(head, axis=0, keepdims=True)
        a_star = jnp.min(jnp.where(head == m, iota, top), axis=0, keepdims=True)
        onehot = iota == a_star
        z = z + jnp.exp(m - m0)
        n = n + jnp.where(onehot, 1.0, 0.0)
        n_star = jnp.sum(jnp.where(onehot, n, 0.0), axis=0, keepdims=True)
        nxt = jnp.sum(jnp.where(iota == n_star, sv2, 0.0), axis=0, keepdims=True)
        nxt = jnp.where(n_star >= top, -jnp.inf, nxt)
        a_val = jnp.sum(jnp.where(onehot, sv1, 0.0), axis=0, keepdims=True)
        head = jnp.where(onehot, a_val + nxt, head)
    return n, z


def _topk_head(st_ref, r2_ref, nn_ref, e2_ref, cc_ref, h, break_ties):
    top = float(PEER_TOPK)
    s1 = st_ref[2 * h]
    s2 = st_ref[2 * h + 1]
    rank1, v1, count1 = _top_ranks(s1, break_ties)
    rank2, v2, count2 = _top_ranks(s2, break_ties)
    n, z = _staircase_counts(jnp.concatenate(v1, axis=0), jnp.concatenate(v2, axis=0))
    nn = jnp.zeros(s1.shape, F32)
    for a in range(PEER_TOPK):
        nn = jnp.where(rank1 == float(a), n[a:a + 1, :], nn)
    r2_ref[h] = rank2.astype(BF16)
    nn_ref[h] = nn
    e2_ref[h] = jnp.where(rank2 < top, jnp.exp(s2 - v2[0]), 0.0).astype(BF16)
    cc_ref[h] = jnp.where(rank1 < top, jnp.exp(s1 - v1[0]), 0.0) / z
    return jnp.max(jnp.abs(count1 - top) + jnp.abs(count2 - top)) > 0.0


def _topk_kernel(st_ref, r2_ref, nn_ref, e2_ref, cc_ref):
    def body(i, carry):
        tied_a = _topk_head(st_ref, r2_ref, nn_ref, e2_ref, cc_ref, 2 * i, False)
        tied_b = _topk_head(st_ref, r2_ref, nn_ref, e2_ref, cc_ref, 2 * i + 1, False)

        @pl.when(jnp.logical_or(tied_a, tied_b))
        def _():
            _topk_head(st_ref, r2_ref, nn_ref, e2_ref, cc_ref, 2 * i, True)
            _topk_head(st_ref, r2_ref, nn_ref, e2_ref, cc_ref, 2 * i + 1, True)

        return carry

    lax.fori_loop(0, PEER_HEADS // 2, body, 0)


def _topk(st):
    nhp, nkeys, n = st.shape
    tl = LANES
    out_spec = pl.BlockSpec((PEER_HEADS, nkeys, tl), lambda i: (0, 0, i))
    sds = lambda dt: jax.ShapeDtypeStruct((PEER_HEADS, nkeys, n), dt)
    return pl.pallas_call(
        _topk_kernel,
        grid=(n // tl,),
        in_specs=[pl.BlockSpec((nhp, nkeys, tl), lambda i: (0, 0, i))],
        out_specs=(out_spec, out_spec, out_spec, out_spec),
        out_shape=(sds(BF16), sds(F32), sds(BF16), sds(F32)),
        compiler_params=_cparams(("parallel",)),
        name="peer_topk",
    )(st)


GELU_A = -2.0 * 0.7978845608028654 * LOG2E
GELU_B = GELU_A * 0.044715


def _gelu_tanh(x):
    return x / (1.0 + jnp.exp2(x * (GELU_A + GELU_B * (x * x))))


def _peer_kernel(h2_ref, x1_ref, u_ref, vt_ref, r2_ref, e2_ref, nn_ref, cc_ref, o_ref, acc_ref, a_ref, gw_ref, w_ref,
                 *, ni, ne):
    s = pl.program_id(0)

    @pl.when(s == 0)
    def _():
        a_ref[...] = jnp.zeros(a_ref.shape, F32)
        gw_ref[...] = jnp.zeros(gw_ref.shape, BF16)

    @pl.when(jnp.logical_or(s < 2, lax.rem(jnp.maximum(s - 2, 0), ne) == 0))
    def _():
        acc_ref[...] = jnp.zeros(acc_ref.shape, F32)

    slot_new = lax.rem(s, 2)
    slot_mid = 1 - slot_new
    tm = h2_ref.shape[0]

    def rows_bf16(row):
        tile = jnp.broadcast_to(row, (16, tm)).astype(BF16)
        return jnp.concatenate([tile] * (PEER_KEYS // 16), axis=0)

    for i in range(ni):
        w = None
        for h in range(PEER_HEADS):
            keep = r2_ref[h] < rows_bf16(nn_ref[h, i:i + 1, :])
            term = jnp.where(keep, e2_ref[h] * rows_bf16(cc_ref[h, i:i + 1, :]), jnp.zeros((), BF16))
            w = term if w is None else w + term
        w_ref[i * PEER_KEYS:(i + 1) * PEER_KEYS, :] = w
    a_ref[slot_new] = lax.dot_general(u_ref[...], h2_ref[...], (((1,), (1,)), ((), ())),
                                      preferred_element_type=F32)
    acc_ref[...] += jnp.dot(vt_ref[...], gw_ref[slot_new], preferred_element_type=F32)
    for i in range(ni):
        rows = slice(i * PEER_KEYS, (i + 1) * PEER_KEYS)
        gw_ref[slot_mid, rows, :] = _gelu_tanh(a_ref[slot_mid, rows, :]).astype(BF16) * w_ref[rows, :]

    @pl.when(jnp.logical_and(s >= 2, lax.rem(jnp.maximum(s - 2, 0), ne) == ne - 1))
    def _():
        o_ref[...] = x1_ref[...] + acc_ref[...].T


def _peer(h2, x1, r2, nn, e2, cc, u_tab, vt_tab):
    n, d = h2.shape
    tm = min(512, n)
    te = 1024
    ne = u_tab.shape[0] // te
    ni = te // PEER_KEYS
    steps = (n // tm) * ne
    pair = lambda s, lag: jnp.clip(s - lag, 0, steps - 1)
    tile = lambda s, lag: pair(s, lag) // ne
    expert = lambda s, lag: lax.rem(pair(s, lag), ne)
    key_spec = pl.BlockSpec((PEER_HEADS, PEER_KEYS, tm), lambda s: (0, 0, tile(s, 1)))
    row_spec = pl.BlockSpec((PEER_HEADS, ni, tm), lambda s: (0, expert(s, 1), tile(s, 1)))
    out_spec = pl.BlockSpec((tm, d), lambda s: (tile(s, 2), 0))
    return pl.pallas_call(
        functools.partial(_peer_kernel, ni=ni, ne=ne),
        grid=(steps + 2,),
        in_specs=[pl.BlockSpec((tm, d), lambda s: (tile(s, 0), 0)), out_spec,
                  pl.BlockSpec((te, d), lambda s: (expert(s, 0), 0)),
                  pl.BlockSpec((d, te), lambda s: (0, expert(s, 2))), key_spec, key_spec, row_spec, row_spec],
        out_specs=out_spec,
        out_shape=jax.ShapeDtypeStruct((n, d), F32),
        scratch_shapes=[pltpu.VMEM((d, tm), F32), pltpu.VMEM((2, te, tm), F32), pltpu.VMEM((2, te, tm), BF16),
                        pltpu.VMEM((te, tm), BF16)],
        compiler_params=_cparams(("arbitrary",)),
        name="peer_experts",
    )(h2, x1, u_tab, vt_tab, r2, e2, nn, cc)


def _prep_weights(l, norm_mix, w_in, conv_w, conv_b, conv_norm_g, conv_norm_b, w_conv_out, forget_bias, q_norm,
                  k_norm, w_attn_out, w_out, norm_ffn, peer_w_q, peer_keys, peer_u, peer_v):
    o1 = 2 * D_CONV
    o2 = o1 + 3 * D_MODEL
    o3 = o2 + N_HEADS
    wl = w_in[l]
    head_of_col = jnp.arange(D_MODEL) // HEAD_DIM
    seg = (head_of_col[:, None] == jnp.arange(LANES)[None, :]).astype(BF16)
    row = lambda a: a.reshape(1, -1).astype(F32)
    return {
        "norm_mix": row(norm_mix[l]),
        "w_glu": wl[:, :o1].astype(BF16),
        "w_qkv": wl[:, o1:o2].astype(BF16),
        "w_f": jnp.pad(wl[:, o2:o3], ((0, 0), (0, LANES - N_HEADS))).astype(BF16),
        "w_gate": wl[:, o3:].astype(BF16),
        "f_bias": jnp.pad(row(forget_bias[l]), ((0, 0), (0, LANES - N_HEADS))),
        "q_gain": row(jnp.tile(q_norm[l], N_HEADS) * (HEAD_DIM ** -0.5 * LOG2E)),
        "k_gain": row(jnp.tile(k_norm[l], N_HEADS)),
        "seg": seg,
        "segt": seg.T,
        "conv_w": jnp.pad(conv_w[l], ((0, HIST_ROWS - CONV_WIDTH), (0, 0))),
        "conv_b": row(conv_b[l]),
        "ln_g": row(conv_norm_g[l]),
        "ln_b": row(conv_norm_b[l]),
        "w_conv_out": w_conv_out[l].astype(BF16),
        "w_attn_out": w_attn_out[l].astype(BF16),
        "w_out": w_out[l].astype(BF16),
        "norm_ffn": row(norm_ffn[l]),
        "peer_w_q": peer_w_q[l].astype(BF16),
        "peer_keys": jnp.pad(peer_keys[l].reshape(2 * PEER_HEADS, PEER_KEYS, -1).astype(BF16),
                             ((0, 0), (0, 0), (0, LANES - peer_keys.shape[-1]))),
        "peer_u": peer_u[l].astype(BF16),
        "peer_vt": peer_v[l].T.astype(BF16),
    }


def _layer_group(x3, conv_state, cache, w):
    b, t, d = x3.shape
    n = b * t
    assert t >= CONV_STATE
    x = x3.reshape(n, d)
    u, qb, qt, k, v, kb, vb, vt, logf, gate = _inproj(x, w)
    u3 = u.reshape(b, t, D_CONV)
    if conv_state is None:
        buf = jnp.zeros((b, HIST_ROWS, D_CONV), F32)
    else:
        buf = jnp.pad(conv_state, ((0, 0), (HIST_ROWS - CONV_STATE, 0), (0, 0)))
    cact = _conv(u3, buf, w)

    logf3 = logf.reshape(b, t, N_HEADS)
    if cache is None:
        cp = _cumsum_pieces(logf3)
        attn = _attn_prompt(qt, kb.reshape(b, t, d), cp, vt, b, t)
    else:
        hist_k, hist_v, cache_logf, layer = cache
        past = hist_k.shape[2]
        total = past + t
        padded = -(-total // LANES) * LANES
        lf = jnp.concatenate([cache_logf[layer].astype(F32), logf3], axis=1)
        cp = _cumsum_pieces(jnp.pad(lf, ((0, 0), (0, padded - total), (0, 0))))
        vnt = jnp.transpose(vb.reshape(b, t, d // LANES, LANES), (0, 2, 3, 1))
        attn = _attn_sample(_sample_query_weights(qb, b, t), kb.reshape(b, t, d), vnt, hist_k, hist_v, layer, cp)

    x1, h2, st = _outproj(attn.reshape(n, d), cact.reshape(n, D_CONV), gate, x, w)
    r2, nn, e2, cc = _topk(st)
    y = _peer(h2, x1, r2, nn, e2, cc, w["peer_u"], w["peer_vt"])
    return (y.reshape(b, t, d), k.reshape(b, t, N_HEADS, HEAD_DIM), v.reshape(b, t, N_HEADS, HEAD_DIM),
            logf3, u3[:, t - CONV_STATE:, :])


def kernel(x_prompt, x_sample, cache_k, cache_v, cache_logf, state_conv, norm_mix, w_in, conv_w, conv_b, conv_norm_g, conv_norm_b, w_conv_out, forget_bias, q_norm, k_norm, w_attn_out, w_out, norm_ffn, peer_w_q, peer_keys, peer_u, peer_v):
    depth = w_in.shape[0]
    yp, ys = x_prompt, x_sample
    outs_p, outs_s = [], []
    flat = cache_k.shape[:3] + (D_MODEL,)
    hist_k = cache_k.reshape(flat).astype(BF16)
    hist_v = cache_v.reshape(flat).astype(BF16)
    for l in range(depth):
        w = _prep_weights(l, norm_mix, w_in, conv_w, conv_b, conv_norm_g, conv_norm_b, w_conv_out, forget_bias,
                          q_norm, k_norm, w_attn_out, w_out, norm_ffn, peer_w_q, peer_keys, peer_u, peer_v)
        yp, *rest_p = _layer_group(yp, None, None, w)
        ys, *rest_s = _layer_group(ys, state_conv[l], (hist_k, hist_v, cache_logf, l), w)
        outs_p.append(rest_p)
        outs_s.append(rest_s)
    stack = lambda outs, i: jnp.stack([o[i] for o in outs])
    return (yp, ys, stack(outs_p, 0), stack(outs_p, 1), stack(outs_p, 2), stack(outs_p, 3),
            stack(outs_s, 0), stack(outs_s, 1), stack(outs_s, 2), stack(outs_s, 3))
```

```python
import functools
import math

import jax
import jax.numpy as jnp
from jax import lax
from jax.experimental import pallas as pl
from jax.experimental.pallas import tpu as pltpu

F32 = jnp.float32
BF16 = jnp.bfloat16

D_MODEL = 1024
N_HEADS = 16
HEAD_DIM = 64
D_CONV = 512
CONV_WIDTH = 31
CONV_STATE = CONV_WIDTH - 1
PEER_HEADS = 8
PEER_KEYS = 128
PEER_TOPK = 16
EPS = 1e-6
LOG2E = math.log2(math.e)
LANES = 128
HIST_ROWS = 32
BIAS_PIECES = 3
VMEM_LIMIT = 56 * 1024 * 1024


def _cparams(sem):
    return pltpu.CompilerParams(dimension_semantics=sem, vmem_limit_bytes=VMEM_LIMIT)


def _const_spec(shape):
    n = len(shape)
    return pl.BlockSpec(shape, lambda *_: (0,) * n)


def _split3(x):
    hi = x.astype(BF16)
    r1 = x - hi.astype(F32)
    mid = r1.astype(BF16)
    lo = (r1 - mid.astype(F32)).astype(BF16)
    return hi, mid, lo


def _split_dot(x, w):
    return sum(jnp.dot(p, w, preferred_element_type=F32) for p in _split3(x))


def _log_sigmoid(x):
    return jnp.minimum(x, 0.0) - jnp.log(1.0 + jnp.exp(-jnp.abs(x)))


def _inproj_kernel(x_ref, g_ref, wglu_ref, wqkv_ref, wf_ref, wgate_ref, fb_ref, qg_ref, kg_ref, seg_ref, segt_ref,
                   u_ref, qb_ref, qt_ref, k_ref, v_ref, kb_ref, vb_ref, vt_ref, logf_ref, gate_ref):
    x = x_ref[...]
    h = x * lax.rsqrt(jnp.mean(x * x, axis=-1, keepdims=True) + EPS) * g_ref[...]
    hb = h.astype(BF16)
    glu = jnp.dot(hb, wglu_ref[...], preferred_element_type=F32)
    u_ref[...] = glu[:, :D_CONV] * jax.nn.sigmoid(glu[:, D_CONV:])
    qkv = jnp.dot(hb, wqkv_ref[...], preferred_element_type=F32)

    def head_norm(t, gain):
        ss = _split_dot(t * t, seg_ref[...])
        r = lax.rsqrt(ss * (1.0 / HEAD_DIM) + EPS)
        return t * _split_dot(r, segt_ref[...]) * gain

    q = head_norm(qkv[:, :D_MODEL], qg_ref[...])
    k = head_norm(qkv[:, D_MODEL:2 * D_MODEL], kg_ref[...])
    v = qkv[:, 2 * D_MODEL:]
    qb_ref[...] = q.astype(BF16)
    qt_ref[...] = q.T.astype(BF16)
    k_ref[...] = k
    kb_ref[...] = k.astype(BF16)
    v_ref[...] = v
    vb_ref[...] = v.astype(BF16)
    vt_ref[...] = v.T.astype(BF16)
    f = jnp.dot(hb, wf_ref[...], preferred_element_type=F32) + fb_ref[...]
    logf_ref[...] = _log_sigmoid(f)[:, :N_HEADS]
    gate_ref[...] = jax.nn.sigmoid(jnp.dot(hb, wgate_ref[...], preferred_element_type=F32))


def _inproj(x, w):
    n = x.shape[0]
    tm = min(256, n)
    tok = lambda c: pl.BlockSpec((tm, c), lambda i: (i, 0))
    tok_t = pl.BlockSpec((D_MODEL, tm), lambda i: (0, i))
    sds = jax.ShapeDtypeStruct
    out_shape = (
        sds((n, D_CONV), F32),
        sds((n, D_MODEL), BF16),
        sds((D_MODEL, n), BF16),
        sds((n, D_MODEL), F32),
        sds((n, D_MODEL), F32),
        sds((n, D_MODEL), BF16),
        sds((n, D_MODEL), BF16),
        sds((D_MODEL, n), BF16),
        sds((n, N_HEADS), F32),
        sds((n, 2 * D_MODEL), F32),
    )
    return pl.pallas_call(
        _inproj_kernel,
        grid=(n // tm,),
        in_specs=[tok(D_MODEL), _const_spec((1, D_MODEL)), _const_spec((D_MODEL, 2 * D_CONV)),
                  _const_spec((D_MODEL, 3 * D_MODEL)), _const_spec((D_MODEL, LANES)),
                  _const_spec((D_MODEL, 2 * D_MODEL)), _const_spec((1, LANES)), _const_spec((1, D_MODEL)),
                  _const_spec((1, D_MODEL)), _const_spec((D_MODEL, LANES)), _const_spec((LANES, D_MODEL))],
        out_specs=(tok(D_CONV), tok(D_MODEL), tok_t, tok(D_MODEL), tok(D_MODEL), tok(D_MODEL), tok(D_MODEL), tok_t,
                   tok(N_HEADS), tok(2 * D_MODEL)),
        out_shape=out_shape,
        compiler_params=_cparams(("parallel",)),
        name="inproj",
    )(x, w["norm_mix"], w["w_glu"], w["w_qkv"], w["w_f"], w["w_gate"], w["f_bias"], w["q_gain"], w["k_gain"],
      w["seg"], w["segt"])


CONV_ROWS = 32


def _conv_kernel(u_ref, halo_ref, buf_ref, w_ref, b_ref, lng_ref, lnb_ref, o_ref, pad_ref, *, tt):
    first = pl.program_id(1) == 0
    pad_ref[0:HIST_ROWS, :] = jnp.where(first, buf_ref[...], halo_ref[...])
    pad_ref[HIST_ROWS:HIST_ROWS + tt, :] = u_ref[...]
    off = HIST_ROWS - CONV_STATE
    for r in range(tt // CONV_ROWS):
        base = r * CONV_ROWS + off
        acc = jnp.broadcast_to(b_ref[...], (CONV_ROWS, D_CONV))
        for k in range(CONV_WIDTH):
            acc = acc + w_ref[k:k + 1, :] * pad_ref[base + k:base + k + CONV_ROWS, :]
        mu = jnp.mean(acc, axis=-1, keepdims=True)
        d = acc - mu
        var = jnp.mean(d * d, axis=-1, keepdims=True)
        y = d * lax.rsqrt(var + EPS) * lng_ref[...] + lnb_ref[...]
        o_ref[r * CONV_ROWS:(r + 1) * CONV_ROWS, :] = (y * jax.nn.sigmoid(y)).astype(BF16)


def _conv(u, buf, w):
    b, t, c = u.shape
    tt = min(256, t)
    hb = tt // HIST_ROWS
    return pl.pallas_call(
        functools.partial(_conv_kernel, tt=tt),
        grid=(b, t // tt),
        in_specs=[pl.BlockSpec((None, tt, c), lambda bi, i: (bi, i, 0)),
                  pl.BlockSpec((None, HIST_ROWS, c), lambda bi, i: (bi, jnp.maximum(i * hb - 1, 0), 0)),
                  pl.BlockSpec((None, HIST_ROWS, c), lambda bi, i: (bi, 0, 0)),
                  _const_spec((HIST_ROWS, c)), _const_spec((1, c)), _const_spec((1, c)), _const_spec((1, c))],
        out_specs=pl.BlockSpec((None, tt, c), lambda bi, i: (bi, i, 0)),
        out_shape=jax.ShapeDtypeStruct((b, t, c), BF16),
        scratch_shapes=[pltpu.VMEM((HIST_ROWS + tt, c), F32)],
        compiler_params=_cparams(("parallel", "arbitrary")),
        name="conv",
    )(u, u, buf, w["conv_w"], w["conv_b"], w["ln_g"], w["ln_b"])


def _cumsum_pieces_kernel(x_ref, tri_ref, place_ref, o_ref, *, nblk):
    def body(j, carry):
        sl = pl.ds(pl.multiple_of(j * LANES, LANES), LANES)
        local = sum(jnp.dot(tri_ref[...], piece, preferred_element_type=F32) for piece in _split3(x_ref[sl, :]))
        cs = local + carry
        out = None
        for j2, piece in enumerate(_split3(cs * (-LOG2E))):
            term = jnp.dot(piece, place_ref[j2], preferred_element_type=F32)
            out = term if out is None else out + term
        o_ref[sl, :] = out.astype(BF16)
        return cs[LANES - 1:LANES, :]

    unroll = next(u for u in (4, 3, 2, 1) if nblk % u == 0)
    lax.fori_loop(0, nblk, body, jnp.zeros((1, x_ref.shape[1]), F32), unroll=unroll)


def _cumsum_pieces(logf):
    b, t, h = logf.shape
    ii = lax.broadcasted_iota(jnp.int32, (LANES, LANES), 0)
    jj = lax.broadcasted_iota(jnp.int32, (LANES, LANES), 1)
    tri = (jj <= ii).astype(BF16)
    head = jnp.arange(h)
    col = jnp.arange(LANES)
    place = jnp.stack([(col[None, :] == (head * BIAS_PIECES + j)[:, None]).astype(BF16)
                       for j in range(BIAS_PIECES)])
    return pl.pallas_call(
        functools.partial(_cumsum_pieces_kernel, nblk=t // LANES),
        grid=(b,),
        in_specs=[pl.BlockSpec((None, t, h), lambda i: (i, 0, 0)), _const_spec((LANES, LANES)),
                  _const_spec((BIAS_PIECES, h, LANES))],
        out_specs=pl.BlockSpec((None, t, LANES), lambda i: (i, 0, 0)),
        out_shape=jax.ShapeDtypeStruct((b, t, LANES), BF16),
        compiler_params=_cparams(("parallel",)),
        name="cumsum_pieces",
    )(logf, tri, place)


COL_TILE = 256


def _ones_rows(tq, pair):
    r = lax.broadcasted_iota(jnp.int32, (LANES, 2 * tq), 0) - 2 * BIAS_PIECES * pair
    c = lax.broadcasted_iota(jnp.int32, (LANES, 2 * tq), 1)
    first = jnp.where(c < tq, 0, BIAS_PIECES)
    return jnp.where(r >= first, jnp.where(r < first + BIAS_PIECES, 1.0, 0.0), 0.0).astype(BF16)


def _attn_init(m_ref, l_ref, acc_ref):
    m_ref[...] = jnp.full(m_ref.shape, -jnp.inf, F32)
    l_ref[...] = jnp.zeros(l_ref.shape, F32)
    acc_ref[...] = jnp.zeros(acc_ref.shape, F32)


def _attn_chunk(kaug, vt_c, wq_ref, m_ref, l_ref, acc_ref, tq, diagonal):
    ncol = wq_ref.shape[1]
    cw = min(COL_TILE, ncol)
    tiles = [slice(ct * cw, (ct + 1) * cw) for ct in range(ncol // cw)]
    logits = [jnp.dot(kaug, wq_ref[:, cs], preferred_element_type=F32) for cs in tiles]
    for ct, cs in enumerate(tiles):
        s = logits[ct]
        if diagonal:
            key = lax.broadcasted_iota(jnp.int32, s.shape, 0)
            qry = lax.broadcasted_iota(jnp.int32, s.shape, 1) + ct * cw
            qry = jnp.where(qry >= tq, qry - tq, qry)
            s = jnp.where(key <= qry, s, -jnp.inf)
        m_old = m_ref[:, cs]
        m_new = jnp.maximum(m_old, jnp.max(s, axis=0, keepdims=True))
        alpha = jnp.exp2(m_old - m_new)
        p = jnp.exp2(s - m_new)
        l_ref[:, cs] = alpha * l_ref[:, cs] + jnp.sum(p, axis=0, keepdims=True)
        acc_ref[:, cs] = alpha * acc_ref[:, cs] + jnp.dot(vt_c, p.astype(BF16), preferred_element_type=F32)
        m_ref[:, cs] = m_new


def _attn_finish(o_ref, l_ref, acc_ref, tq):
    ot = (acc_ref[...] / l_ref[...]).T
    lane = lax.broadcasted_iota(jnp.int32, (tq, LANES), 1)
    o_ref[...] = jnp.where(lane < HEAD_DIM, ot[:tq], ot[tq:]).astype(o_ref.dtype)


def _attn_prompt_kernel(qt_ref, k_ref, cp_ref, vt_ref, o_ref, wq_ref, m_ref, l_ref, acc_ref, *, tq):
    qi = pl.program_id(2)
    qt = qt_ref[...]
    row = lax.broadcasted_iota(jnp.int32, qt.shape, 0)
    zero = jnp.zeros_like(qt)
    wq_ref[0:LANES, :] = jnp.concatenate([jnp.where(row < HEAD_DIM, qt, zero), jnp.where(row >= HEAD_DIM, qt, zero)],
                                         axis=1)
    wq_ref[LANES:2 * LANES, :] = _ones_rows(tq, pl.program_id(1))
    _attn_init(m_ref, l_ref, acc_ref)

    def chunk(kc, diagonal):
        ksl = pl.ds(pl.multiple_of(kc * tq, tq), tq)
        kaug = jnp.concatenate([k_ref[ksl, :], cp_ref[ksl, :]], axis=1)
        _attn_chunk(kaug, vt_ref[:, ksl], wq_ref, m_ref, l_ref, acc_ref, tq, diagonal)

    def body(kc, carry):
        chunk(kc, False)
        return carry

    lax.fori_loop(0, qi, body, 0)
    chunk(qi, True)
    _attn_finish(o_ref, l_ref, acc_ref, tq)


def _attn_prompt(qt, kb, cp, vt, b, t):
    d = qt.shape[0]
    tq = min(512, t)
    nq = t // tq
    pairs = d // LANES
    return pl.pallas_call(
        functools.partial(_attn_prompt_kernel, tq=tq),
        grid=(b, pairs, nq),
        in_specs=[pl.BlockSpec((LANES, tq), lambda bi, p, qi: (p, bi * nq + qi)),
                  pl.BlockSpec((None, t, LANES), lambda bi, p, qi: (bi, 0, p)),
                  pl.BlockSpec((None, t, LANES), lambda bi, p, qi: (bi, 0, 0)),
                  pl.BlockSpec((LANES, t), lambda bi, p, qi: (p, bi))],
        out_specs=pl.BlockSpec((None, tq, LANES), lambda bi, p, qi: (bi, qi, p)),
        out_shape=jax.ShapeDtypeStruct((b, t, d), BF16),
        scratch_shapes=[pltpu.VMEM((2 * LANES, 2 * tq), BF16), pltpu.VMEM((1, 2 * tq), F32),
                        pltpu.VMEM((1, 2 * tq), F32), pltpu.VMEM((LANES, 2 * tq), F32)],
        compiler_params=_cparams(("parallel", "parallel", "arbitrary")),
        name="attn_prompt",
    )(qt, kb, cp, vt)


def _attn_sample_kernel(wq_ref, kh_ref, vh_ref, cp_ref, kn_ref, vnt_ref, o_ref, *, past, t):
    wq = wq_ref[...]
    s_hist = jnp.dot(jnp.concatenate([kh_ref[...], cp_ref[0:past, :]], axis=1), wq, preferred_element_type=F32)
    s_new = jnp.dot(jnp.concatenate([kn_ref[...], cp_ref[past:past + t, :]], axis=1), wq, preferred_element_type=F32)
    key = lax.broadcasted_iota(jnp.int32, s_new.shape, 0)
    qry = lax.broadcasted_iota(jnp.int32, s_new.shape, 1)
    s_new = jnp.where(key <= jnp.where(qry >= t, qry - t, qry), s_new, -jnp.inf)
    m = jnp.maximum(jnp.max(s_hist, axis=0, keepdims=True), jnp.max(s_new, axis=0, keepdims=True))
    p_hist = jnp.exp2(s_hist - m)
    p_new = jnp.exp2(s_new - m)
    denom = jnp.sum(p_hist, axis=0, keepdims=True) + jnp.sum(p_new, axis=0, keepdims=True)
    acc = (jnp.dot(vh_ref[...].T, p_hist.astype(BF16), preferred_element_type=F32)
           + jnp.dot(vnt_ref[...], p_new.astype(BF16), preferred_element_type=F32))
    ot = (acc / denom).T
    lane = lax.broadcasted_iota(jnp.int32, (t, LANES), 1)
    o_ref[...] = jnp.where(lane < HEAD_DIM, ot[:t], ot[t:]).astype(o_ref.dtype)


def _attn_sample(wq, kn, vnt, hist_k, hist_v, layer, cp):
    b, t, d = kn.shape
    past = hist_k.shape[2]
    pairs = d // LANES
    hist_spec = pl.BlockSpec((None, None, past, LANES), lambda bi, p: (layer, bi, 0, p))
    return pl.pallas_call(
        functools.partial(_attn_sample_kernel, past=past, t=t),
        grid=(b, pairs),
        in_specs=[pl.BlockSpec((None, None, 2 * LANES, 2 * t), lambda bi, p: (bi, p, 0, 0)), hist_spec, hist_spec,
                  pl.BlockSpec((None, cp.shape[1], LANES), lambda bi, p: (bi, 0, 0)),
                  pl.BlockSpec((None, t, LANES), lambda bi, p: (bi, 0, p)),
                  pl.BlockSpec((None, None, LANES, t), lambda bi, p: (bi, p, 0, 0))],
        out_specs=pl.BlockSpec((None, t, LANES), lambda bi, p: (bi, 0, p)),
        out_shape=jax.ShapeDtypeStruct((b, t, d), BF16),
        compiler_params=_cparams(("parallel", "parallel")),
        name="attn_sample",
    )(wq, hist_k, hist_v, cp, kn, vnt)


def _sample_query_weights(qb, b, t):
    pairs = D_MODEL // LANES
    q = jnp.transpose(qb.reshape(b, t, pairs, 2, HEAD_DIM), (0, 2, 3, 4, 1))
    zero = jnp.zeros_like(q[:, :, 0])
    top = jnp.concatenate([jnp.concatenate([q[:, :, 0], zero], axis=-1),
                           jnp.concatenate([zero, q[:, :, 1]], axis=-1)], axis=-2)
    ones = jnp.broadcast_to(jnp.stack([_ones_rows(t, p) for p in range(pairs)]), (b, pairs, LANES, 2 * t))
    return jnp.concatenate([top, ones], axis=-2)


def _outproj_kernel(attn_ref, cact_ref, gate_ref, x_ref, wco_ref, wao_ref, wo_ref, g_ref, wpq_ref, keys_ref,
                    x1_ref, h2_ref, st_ref):
    conv_out = jnp.dot(cact_ref[...], wco_ref[...], preferred_element_type=F32)
    attn_out = jnp.dot(attn_ref[...], wao_ref[...], preferred_element_type=F32)
    merged = gate_ref[:, :D_MODEL] * conv_out + gate_ref[:, D_MODEL:] * attn_out
    x1 = x_ref[...] + jnp.dot(merged.astype(BF16), wo_ref[...], preferred_element_type=F32)
    x1_ref[...] = x1
    h2f = x1 * lax.rsqrt(jnp.mean(x1 * x1, axis=-1, keepdims=True) + EPS) * g_ref[...]
    h2 = h2f.astype(BF16)
    h2_ref[...] = h2f.T.astype(BF16)
    qp =jnp.dot(h2, wpq_ref[...], preferred_element_type=F32).astype(BF16)
    for hp in range(2 * PEER_HEADS):
        st_ref[hp] = lax.dot_general(keys_ref[hp], qp[:, hp * LANES:(hp + 1) * LANES], (((1,), (1,)), ((), ())),
                                     preferred_element_type=F32)


def _outproj(attn, cact, gate, x, w):
    n = x.shape[0]
    tm = min(256, n)
    tok = lambda c: pl.BlockSpec((tm, c), lambda i: (i, 0))
    nhp = 2 * PEER_HEADS
    return pl.pallas_call(
        _outproj_kernel,
        grid=(n // tm,),
        in_specs=[tok(D_MODEL), tok(D_CONV), tok(2 * D_MODEL), tok(D_MODEL),
                  _const_spec((D_CONV, D_MODEL)), _const_spec((D_MODEL, D_MODEL)), _const_spec((D_MODEL, D_MODEL)),
                  _const_spec((1, D_MODEL)), _const_spec((D_MODEL, nhp * LANES)),
                  _const_spec((nhp, PEER_KEYS, LANES))],
        out_specs=(tok(D_MODEL), pl.BlockSpec((D_MODEL, tm), lambda i: (0, i)),
                   pl.BlockSpec((nhp, PEER_KEYS, tm), lambda i: (0, 0, i))),
        out_shape=(jax.ShapeDtypeStruct((n, D_MODEL), F32), jax.ShapeDtypeStruct((D_MODEL, n), BF16),
                   jax.ShapeDtypeStruct((nhp, PEER_KEYS, n), F32)),
        compiler_params=_cparams(("parallel",)),
        name="outproj",
    )(attn, cact, gate, x, w["w_conv_out"], w["w_attn_out"], w["w_out"], w["norm_ffn"], w["peer_w_q"], w["peer_keys"])


def _top_ranks(s, break_ties):
    nrow = s.shape[0]
    iota = lax.broadcasted_iota(jnp.int32, s.shape, 0).astype(F32)
    rank = jnp.full(s.shape, float(PEER_TOPK), F32)
    vals = []
    for r in range(PEER_TOPK):
        m = jnp.max(s, axis=0, keepdims=True)
        sel = s == m
        if break_ties:
            sel = iota == jnp.min(jnp.where(sel, iota, float(nrow)), axis=0, keepdims=True)
        rank = jnp.where(sel, float(r), rank)
        s = jnp.where(sel, -jnp.inf, s)
        vals.append(m)
    count = jnp.sum(jnp.where(rank < float(PEER_TOPK), 1.0, 0.0), axis=0, keepdims=True)
    return rank, vals, count


def _staircase_counts(sv1, sv2):
    top = float(PEER_TOPK)
    iota = lax.broadcasted_iota(jnp.int32, sv1.shape, 0).astype(F32)
    n = jnp.zeros(sv1.shape, F32)
    head = sv1 + sv2[0:1, :]
    m0 = head[0:1, :]
    z = jnp.zeros(m0.shape, F32)
    for _ in range(PEER_TOPK):
        m = jnp.max(head, axis=0, keepdims=True)
        a_star = jnp.min(jnp.where(head == m, iota, top), axis=0, keepdims=True)
        onehot = iota == a_star
        z = z + jnp.exp(m - m0)
        n = n + jnp.where(onehot, 1.0, 0.0)
        n_star = jnp.sum(jnp.where(onehot, n, 0.0), axis=0, keepdims=True)
        nxt = jnp.sum(jnp.where(iota == n_star, sv2, 0.0), axis=0, keepdims=True)
        nxt = jnp.where(n_star >= top, -jnp.inf, nxt)
        a_val = jnp.sum(jnp.where(onehot, sv1, 0.0), axis=0, keepdims=True)
        head = jnp.where(onehot, a_val + nxt, head)
    return n, z


def _topk_head(st_ref, r2_ref, nn_ref, e2_ref, cc_ref, h, break_ties):
    top = float(PEER_TOPK)
    s1 = st_ref[2 * h]
    s2 = st_ref[2 * h + 1]
    rank1, v1, count1 = _top_ranks(s1, break_ties)
    rank2, v2, count2 = _top_ranks(s2, break_ties)
    n, z = _staircase_counts(jnp.concatenate(v1, axis=0), jnp.concatenate(v2, axis=0))
    nn = jnp.zeros(s1.shape, F32)
    for a in range(PEER_TOPK):
        nn = jnp.where(rank1 == float(a), n[a:a + 1, :], nn)
    r2_ref[h] = rank2.astype(BF16)
    nn_ref[h] = nn
    e2_ref[h] = jnp.where(rank2 < top, jnp.exp(s2 - v2[0]), 0.0).astype(BF16)
    cc_ref[h] = jnp.where(rank1 < top, jnp.exp(s1 - v1[0]), 0.0) / z
    return jnp.max(jnp.abs(count1 - top) + jnp.abs(count2 - top)) > 0.0


def _topk_kernel(st_ref, r2_ref, nn_ref, e2_ref, cc_ref):
    def body(i, carry):
        tied_a = _topk_head(st_ref, r2_ref, nn_ref, e2_ref, cc_ref, 2 * i, False)
        tied_b = _topk_head(st_ref, r2_ref, nn_ref, e2_ref, cc_ref, 2 * i + 1, False)

        @pl.when(jnp.logical_or(tied_a, tied_b))
        def _():
            _topk_head(st_ref, r2_ref, nn_ref, e2_ref, cc_ref, 2 * i, True)
            _topk_head(st_ref, r2_ref, nn_ref, e2_ref, cc_ref, 2 * i + 1, True)

        return carry

    lax.fori_loop(0, PEER_HEADS // 2, body, 0)


def _topk(st):
    nhp, nkeys, n = st.shape
    tl = LANES
    out_spec = pl.BlockSpec((PEER_HEADS, nkeys, tl), lambda i: (0, 0, i))
    sds = lambda dt: jax.ShapeDtypeStruct((PEER_HEADS, nkeys, n), dt)
    return pl.pallas_call(
        _topk_kernel,
        grid=(n // tl,),
        in_specs=[pl.BlockSpec((nhp, nkeys, tl), lambda i: (0, 0, i))],
        out_specs=(out_spec, out_spec, out_spec, out_spec),
        out_shape=(sds(BF16), sds(F32), sds(BF16), sds(F32)),
        compiler_params=_cparams(("parallel",)),
        name="peer_topk",
    )(st)


GELU_A = -2.0 * 0.7978845608028654 * LOG2E
GELU_B = GELU_A * 0.044715


def _gelu_tanh(x):
    return x / (1.0 + jnp.exp2(x * (GELU_A + GELU_B * (x * x))))


def _peer_kernel(h2t_ref, x1_ref, u_ref, vt_ref, r2_ref, e2_ref, nn_ref, cc_ref, o_ref, acc_ref, a_ref, gw_ref, w_ref,
                 *, ni, ne, steps):
    s = pl.program_id(0)

    @pl.when(s == 0)
    def _():
        a_ref[...] = jnp.zeros(a_ref.shape, F32)
        gw_ref[...] = jnp.zeros(gw_ref.shape, BF16)

    @pl.when(jnp.logical_or(s < 2, lax.rem(jnp.maximum(s - 2, 0), ne) == 0))
    def _():
        acc_ref[...] = jnp.zeros(acc_ref.shape, F32)

    slot_new = lax.rem(s, 2)
    slot_mid = 1 - slot_new
    tm = h2t_ref.shape[1]
    tiles_per_block = nn_ref.shape[1] // ni
    row0 = lax.rem(lax.rem(jnp.clip(s - 1, 0, steps - 1), ne), tiles_per_block) * ni

    def rows_bf16(ref, h, i):
        tile = jnp.broadcast_to(ref[h, pl.ds(row0 + i, 1), :], (16, tm)).astype(BF16)
        return jnp.concatenate([tile] * (PEER_KEYS // 16), axis=0)

    for i in range(ni):
        w = None
        for h in range(PEER_HEADS):
            keep = r2_ref[h] < rows_bf16(nn_ref, h, i)
            term = jnp.where(keep, e2_ref[h] * rows_bf16(cc_ref, h, i), jnp.zeros((), BF16))
            w = term if w is None else w + term
        w_ref[i * PEER_KEYS:(i + 1) * PEER_KEYS, :] = w
    a_ref[slot_new] = jnp.dot(u_ref[...], h2t_ref[...], preferred_element_type=F32)
    acc_ref[...] += jnp.dot(vt_ref[...], gw_ref[slot_new], preferred_element_type=F32)
    for i in range(ni):
        rows = slice(i * PEER_KEYS, (i + 1) * PEER_KEYS)
        gw_ref[slot_mid, rows, :] = _gelu_tanh(a_ref[slot_mid, rows, :]).astype(BF16) * w_ref[rows, :]

    @pl.when(jnp.logical_and(s >= 2, lax.rem(jnp.maximum(s - 2, 0), ne) == ne - 1))
    def _():
        o_ref[...] = x1_ref[...] + acc_ref[...].T


SUBLANES = 8


def _peer(h2t, x1, r2, nn, e2, cc, u_tab, vt_tab):
    d, n = h2t.shape
    tm = min(1024, n)
    te = 512
    ne = u_tab.shape[0] // te
    ni = te // PEER_KEYS
    steps = (n // tm) * ne
    pair = lambda s, lag: jnp.clip(s - lag, 0, steps - 1)
    tile = lambda s, lag: pair(s, lag) // ne
    expert = lambda s, lag: lax.rem(pair(s, lag), ne)
    row_block = max(ni, SUBLANES)
    key_spec = pl.BlockSpec((PEER_HEADS, PEER_KEYS, tm), lambda s: (0, 0, tile(s, 1)))
    row_spec = pl.BlockSpec((PEER_HEADS, row_block, tm), lambda s: (0, expert(s, 1) // (row_block // ni), tile(s, 1)))
    out_spec = pl.BlockSpec((tm, d), lambda s: (tile(s, 2), 0))
    return pl.pallas_call(
        functools.partial(_peer_kernel, ni=ni, ne=ne, steps=steps),
        grid=(steps + 2,),
        in_specs=[pl.BlockSpec((d, tm), lambda s: (0, tile(s, 0))), out_spec,
                  pl.BlockSpec((te, d), lambda s: (expert(s, 0), 0)),
                  pl.BlockSpec((d, te), lambda s: (0, expert(s, 2))), key_spec, key_spec, row_spec, row_spec],
        out_specs=out_spec,
        out_shape=jax.ShapeDtypeStruct((n, d), F32),
        scratch_shapes=[pltpu.VMEM((d, tm), F32), pltpu.VMEM((2, te, tm), F32), pltpu.VMEM((2, te, tm), BF16),
                        pltpu.VMEM((te, tm), BF16)],
        compiler_params=_cparams(("arbitrary",)),
        name="peer_experts",
    )(h2t, x1, u_tab, vt_tab, r2, e2, nn, cc)


def _prep_weights(l, norm_mix, w_in, conv_w, conv_b, conv_norm_g, conv_norm_b, w_conv_out, forget_bias, q_norm,
                  k_norm, w_attn_out, w_out, norm_ffn, peer_w_q, peer_keys, peer_u, peer_v):
    o1 = 2 * D_CONV
    o2 = o1 + 3 * D_MODEL
    o3 = o2 + N_HEADS
    wl = w_in[l]
    head_of_col = jnp.arange(D_MODEL) // HEAD_DIM
    seg = (head_of_col[:, None] == jnp.arange(LANES)[None, :]).astype(BF16)
    row = lambda a: a.reshape(1, -1).astype(F32)
    return {
        "norm_mix": row(norm_mix[l]),
        "w_glu": wl[:, :o1].astype(BF16),
        "w_qkv": wl[:, o1:o2].astype(BF16),
        "w_f": jnp.pad(wl[:, o2:o3], ((0, 0), (0, LANES - N_HEADS))).astype(BF16),
        "w_gate": wl[:, o3:].astype(BF16),
        "f_bias": jnp.pad(row(forget_bias[l]), ((0, 0), (0, LANES - N_HEADS))),
        "q_gain": row(jnp.tile(q_norm[l], N_HEADS) * (HEAD_DIM ** -0.5 * LOG2E)),
        "k_gain": row(jnp.tile(k_norm[l], N_HEADS)),
        "seg": seg,
        "segt": seg.T,
        "conv_w": jnp.pad(conv_w[l], ((0, HIST_ROWS - CONV_WIDTH), (0, 0))),
        "conv_b": row(conv_b[l]),
        "ln_g": row(conv_norm_g[l]),
        "ln_b": row(conv_norm_b[l]),
        "w_conv_out": w_conv_out[l].astype(BF16),
        "w_attn_out": w_attn_out[l].astype(BF16),
        "w_out": w_out[l].astype(BF16),
        "norm_ffn": row(norm_ffn[l]),
        "peer_w_q": peer_w_q[l].astype(BF16),
        "peer_keys": jnp.pad(peer_keys[l].reshape(2 * PEER_HEADS, PEER_KEYS, -1).astype(BF16),
                             ((0, 0), (0, 0), (0, LANES - peer_keys.shape[-1]))),
        "peer_u": peer_u[l].astype(BF16),
        "peer_vt": peer_v[l].T.astype(BF16),
    }


def _layer_group(x3, conv_state, cache, w):
    b, t, d = x3.shape
    n = b * t
    assert t >= CONV_STATE
    x = x3.reshape(n, d)
    u, qb, qt, k, v, kb, vb, vt, logf, gate = _inproj(x, w)
    u3 = u.reshape(b, t, D_CONV)
    if conv_state is None:
        buf = jnp.zeros((b, HIST_ROWS, D_CONV), F32)
    else:
        buf = jnp.pad(conv_state, ((0, 0), (HIST_ROWS - CONV_STATE, 0), (0, 0)))
    cact = _conv(u3, buf, w)

    logf3 = logf.reshape(b, t, N_HEADS)
    if cache is None:
        cp = _cumsum_pieces(logf3)
        attn = _attn_prompt(qt, kb.reshape(b, t, d), cp, vt, b, t)
    else:
        hist_k, hist_v, cache_logf, layer = cache
        past = hist_k.shape[2]
        total = past + t
        padded = -(-total // LANES) * LANES
        lf = jnp.concatenate([cache_logf[layer].astype(F32), logf3], axis=1)
        cp = _cumsum_pieces(jnp.pad(lf, ((0, 0), (0, padded - total), (0, 0))))
        vnt = jnp.transpose(vb.reshape(b, t, d // LANES, LANES), (0, 2, 3, 1))
        attn = _attn_sample(_sample_query_weights(qb, b, t), kb.reshape(b, t, d), vnt, hist_k, hist_v, layer, cp)

    x1, h2, st = _outproj(attn.reshape(n, d), cact.reshape(n, D_CONV), gate, x, w)
    r2, nn, e2, cc = _topk(st)
    y = _peer(h2, x1, r2, nn, e2, cc, w["peer_u"], w["peer_vt"])
    return (y.reshape(b, t, d), k.reshape(b, t, N_HEADS, HEAD_DIM), v.reshape(b, t, N_HEADS, HEAD_DIM),
            logf3, u3[:, t - CONV_STATE:, :])


def kernel(x_prompt, x_sample, cache_k, cache_v, cache_logf, state_conv, norm_mix, w_in, conv_w, conv_b, conv_norm_g, conv_norm_b, w_conv_out, forget_bias, q_norm, k_norm, w_attn_out, w_out, norm_ffn, peer_w_q, peer_keys, peer_u, peer_v):
    depth = w_in.shape[0]
    yp, ys = x_prompt, x_sample
    outs_p, outs_s = [], []
    flat = cache_k.shape[:3] + (D_MODEL,)
    hist_k = cache_k.reshape(flat).astype(BF16)
    hist_v = cache_v.reshape(flat).astype(BF16)
    for l in range(depth):
        w = _prep_weights(l, norm_mix, w_in, conv_w, conv_b, conv_norm_g, conv_norm_b, w_conv_out, forget_bias,
                          q_norm, k_norm, w_attn_out, w_out, norm_ffn, peer_w_q, peer_keys, peer_u, peer_v)
        yp, *rest_p = _layer_group(yp, None, None, w)
        ys, *rest_s = _layer_group(ys, state_conv[l], (hist_k, hist_v, cache_logf, l), w)
        outs_p.append(rest_p)
        outs_s.append(rest_s)
    stack = lambda outs, i: jnp.stack([o[i] for o in outs])
    return (yp, ys, stack(outs_p, 0), stack(outs_p, 1), stack(outs_p, 2), stack(outs_p, 3),
            stack(outs_s, 0), stack(outs_s, 1), stack(outs_s, 2), stack(outs_s, 3))
```

```python
import functools
import math

import jax
import jax.numpy as jnp
from jax import lax
from jax.experimental import pallas as pl
from jax.experimental.pallas import tpu as pltpu

F32 = jnp.float32
BF16 = jnp.bfloat16

D_MODEL = 1024
N_HEADS = 16
HEAD_DIM = 64
D_CONV = 512
CONV_WIDTH = 31
CONV_STATE = CONV_WIDTH - 1
PEER_HEADS = 8
PEER_KEYS = 128
PEER_TOPK = 16
EPS = 1e-6
LOG2E = math.log2(math.e)
LANES = 128
HIST_ROWS = 32
BIAS_PIECES = 3
VMEM_LIMIT = 56 * 1024 * 1024


def _cparams(sem):
    return pltpu.CompilerParams(dimension_semantics=sem, vmem_limit_bytes=VMEM_LIMIT)


def _const_spec(shape):
    n = len(shape)
    return pl.BlockSpec(shape, lambda *_: (0,) * n)


def _split3(x):
    hi = x.astype(BF16)
    r1 = x - hi.astype(F32)
    mid = r1.astype(BF16)
    lo = (r1 - mid.astype(F32)).astype(BF16)
    return hi, mid, lo


def _split_dot(x, w):
    return sum(jnp.dot(p, w, preferred_element_type=F32) for p in _split3(x))


def _log_sigmoid(x):
    return jnp.minimum(x, 0.0) - jnp.log(1.0 + jnp.exp(-jnp.abs(x)))


def _inproj_kernel(x_ref, g_ref, wglu_ref, wqkv_ref, wf_ref, wgate_ref, fb_ref, qg_ref, kg_ref, seg_ref, segt_ref,
                   u_ref, qb_ref, qt_ref, k_ref, v_ref, kb_ref, vb_ref, vt_ref, logf_ref, gate_ref):
    x = x_ref[...]
    h = x * lax.rsqrt(jnp.mean(x * x, axis=-1, keepdims=True) + EPS) * g_ref[...]
    hb = h.astype(BF16)
    glu = jnp.dot(hb, wglu_ref[...], preferred_element_type=F32)
    u_ref[...] = glu[:, :D_CONV] * jax.nn.sigmoid(glu[:, D_CONV:])
    qkv = jnp.dot(hb, wqkv_ref[...], preferred_element_type=F32)

    def head_norm(t, gain):
        ss = _split_dot(t * t, seg_ref[...])
        r = lax.rsqrt(ss * (1.0 / HEAD_DIM) + EPS)
        return t * _split_dot(r, segt_ref[...]) * gain

    q = head_norm(qkv[:, :D_MODEL], qg_ref[...])
    k = head_norm(qkv[:, D_MODEL:2 * D_MODEL], kg_ref[...])
    v = qkv[:, 2 * D_MODEL:]
    qb_ref[...] = q.astype(BF16)
    qt_ref[...] = q.T.astype(BF16)
    k_ref[...] = k.reshape(k_ref.shape)
    kb_ref[...] = k.astype(BF16)
    v_ref[...] = v.reshape(v_ref.shape)
    vb_ref[...] = v.astype(BF16)
    vt_ref[...] = v.T.astype(BF16)
    f = jnp.dot(hb, wf_ref[...], preferred_element_type=F32) + fb_ref[...]
    logf_ref[...] = _log_sigmoid(f)[:, :N_HEADS]
    gate_ref[...] = jax.nn.sigmoid(jnp.dot(hb, wgate_ref[...], preferred_element_type=F32))


def _inproj(x, w):
    n = x.shape[0]
    tm = min(256, n)
    tok = lambda c: pl.BlockSpec((tm, c), lambda i: (i, 0))
    tok_t = pl.BlockSpec((D_MODEL, tm), lambda i: (0, i))
    tok_heads = pl.BlockSpec((tm, N_HEADS, HEAD_DIM), lambda i: (i, 0, 0))
    sds = jax.ShapeDtypeStruct
    out_shape = (
        sds((n, D_CONV), F32),
        sds((n, D_MODEL), BF16),
        sds((D_MODEL, n), BF16),
        sds((n, N_HEADS, HEAD_DIM), F32),
        sds((n, N_HEADS, HEAD_DIM), F32),
        sds((n, D_MODEL), BF16),
        sds((n, D_MODEL), BF16),
        sds((D_MODEL, n), BF16),
        sds((n, N_HEADS), F32),
        sds((n, 2 * D_MODEL), F32),
    )
    return pl.pallas_call(
        _inproj_kernel,
        grid=(n // tm,),
        in_specs=[tok(D_MODEL), _const_spec((1, D_MODEL)), _const_spec((D_MODEL, 2 * D_CONV)),
                  _const_spec((D_MODEL, 3 * D_MODEL)), _const_spec((D_MODEL, LANES)),
                  _const_spec((D_MODEL, 2 * D_MODEL)), _const_spec((1, LANES)), _const_spec((1, D_MODEL)),
                  _const_spec((1, D_MODEL)), _const_spec((D_MODEL, LANES)), _const_spec((LANES, D_MODEL))],
        out_specs=(tok(D_CONV), tok(D_MODEL), tok_t, tok_heads, tok_heads, tok(D_MODEL), tok(D_MODEL), tok_t,
                   tok(N_HEADS), tok(2 * D_MODEL)),
        out_shape=out_shape,
        compiler_params=_cparams(("parallel",)),
        name="inproj",
    )(x, w["norm_mix"], w["w_glu"], w["w_qkv"], w["w_f"], w["w_gate"], w["f_bias"], w["q_gain"], w["k_gain"],
      w["seg"], w["segt"])


CONV_ROWS = 32
SUBLANE_TILE = 8


def _conv_kernel(u_ref, halo_ref, buf_ref, w_ref, b_ref, lng_ref, lnb_ref, o_ref, pad_ref, *, tt):
    first = pl.program_id(1) == 0
    pad_ref[0, 0:HIST_ROWS, :] = jnp.where(first, buf_ref[...], halo_ref[...])
    pad_ref[0, HIST_ROWS:HIST_ROWS + tt, :] = u_ref[...]
    span = HIST_ROWS + tt - SUBLANE_TILE
    for s in range(1, SUBLANE_TILE):
        pad_ref[s, 0:span, :] = pad_ref[0, s:s + span, :]
    off = HIST_ROWS - CONV_STATE
    for r in range(tt // CONV_ROWS):
        acc = jnp.broadcast_to(b_ref[...], (CONV_ROWS, D_CONV))
        for k in range(CONV_WIDTH):
            shift = (off + k) % SUBLANE_TILE
            start = r * CONV_ROWS + (off + k) - shift
            acc = acc + w_ref[k:k + 1, :] * pad_ref[shift, start:start + CONV_ROWS, :]
        mu = jnp.mean(acc, axis=-1, keepdims=True)
        d = acc - mu
        var = jnp.mean(d * d, axis=-1, keepdims=True)
        y = d * lax.rsqrt(var + EPS) * lng_ref[...] + lnb_ref[...]
        o_ref[r * CONV_ROWS:(r + 1) * CONV_ROWS, :] = (y * jax.nn.sigmoid(y)).astype(BF16)


def _conv(u, buf, w):
    b, t, c = u.shape
    tt = min(256, t)
    hb = tt // HIST_ROWS
    return pl.pallas_call(
        functools.partial(_conv_kernel, tt=tt),
        grid=(b, t // tt),
        in_specs=[pl.BlockSpec((None, tt, c), lambda bi, i: (bi, i, 0)),
                  pl.BlockSpec((None, HIST_ROWS, c), lambda bi, i: (bi, jnp.maximum(i * hb - 1, 0), 0)),
                  pl.BlockSpec((None, HIST_ROWS, c), lambda bi, i: (bi, 0, 0)),
                  _const_spec((HIST_ROWS, c)), _const_spec((1, c)), _const_spec((1, c)), _const_spec((1, c))],
        out_specs=pl.BlockSpec((None, tt, c), lambda bi, i: (bi, i, 0)),
        out_shape=jax.ShapeDtypeStruct((b, t, c), BF16),
        scratch_shapes=[pltpu.VMEM((SUBLANE_TILE, HIST_ROWS + tt, c), F32)],
        compiler_params=_cparams(("parallel", "arbitrary")),
        name="conv",
    )(u, u, buf, w["conv_w"], w["conv_b"], w["ln_g"], w["ln_b"])


def _cumsum_pieces_kernel(x_ref, tri_ref, place_ref, o_ref, *, nblk):
    def body(j, carry):
        sl = pl.ds(pl.multiple_of(j * LANES, LANES), LANES)
        local = sum(jnp.dot(tri_ref[...], piece, preferred_element_type=F32) for piece in _split3(x_ref[sl, :]))
        cs = local + carry
        out = None
        for j2, piece in enumerate(_split3(cs * (-LOG2E))):
            term = jnp.dot(piece, place_ref[j2], preferred_element_type=F32)
            out = term if out is None else out + term
        o_ref[sl, :] = out.astype(BF16)
        return cs[LANES - 1:LANES, :]

    unroll = next(u for u in (4, 3, 2, 1) if nblk % u == 0)
    lax.fori_loop(0, nblk, body, jnp.zeros((1, x_ref.shape[1]), F32), unroll=unroll)


def _cumsum_pieces(logf):
    b, t, h = logf.shape
    ii = lax.broadcasted_iota(jnp.int32, (LANES, LANES), 0)
    jj = lax.broadcasted_iota(jnp.int32, (LANES, LANES), 1)
    tri = (jj <= ii).astype(BF16)
    head = jnp.arange(h)
    col = jnp.arange(LANES)
    place = jnp.stack([(col[None, :] == (head * BIAS_PIECES + j)[:, None]).astype(BF16)
                       for j in range(BIAS_PIECES)])
    return pl.pallas_call(
        functools.partial(_cumsum_pieces_kernel, nblk=t // LANES),
        grid=(b,),
        in_specs=[pl.BlockSpec((None, t, h), lambda i: (i, 0, 0)), _const_spec((LANES, LANES)),
                  _const_spec((BIAS_PIECES, h, LANES))],
        out_specs=pl.BlockSpec((None, t, LANES), lambda i: (i, 0, 0)),
        out_shape=jax.ShapeDtypeStruct((b, t, LANES), BF16),
        compiler_params=_cparams(("parallel",)),
        name="cumsum_pieces",
    )(logf, tri, place)


COL_TILE = 256


def _ones_rows(tq, pair):
    r = lax.broadcasted_iota(jnp.int32, (LANES, 2 * tq), 0) - 2 * BIAS_PIECES * pair
    c = lax.broadcasted_iota(jnp.int32, (LANES, 2 * tq), 1)
    first = jnp.where(c < tq, 0, BIAS_PIECES)
    return jnp.where(r >= first, jnp.where(r < first + BIAS_PIECES, 1.0, 0.0), 0.0).astype(BF16)


def _attn_init(m_ref, l_ref, acc_ref):
    m_ref[...] = jnp.full(m_ref.shape, -jnp.inf, F32)
    l_ref[...] = jnp.zeros(l_ref.shape, F32)
    acc_ref[...] = jnp.zeros(acc_ref.shape, F32)


def _attn_chunk(kaug, vt_c, wq_ref, m_ref, l_ref, acc_ref, tq, diagonal):
    ncol = wq_ref.shape[1]
    cw = min(COL_TILE, ncol)
    tiles = [slice(ct * cw, (ct + 1) * cw) for ct in range(ncol // cw)]
    logits = [jnp.dot(kaug, wq_ref[:, cs], preferred_element_type=F32) for cs in tiles]
    for ct, cs in enumerate(tiles):
        s = logits[ct]
        if diagonal:
            key = lax.broadcasted_iota(jnp.int32, s.shape, 0)
            qry = lax.broadcasted_iota(jnp.int32, s.shape, 1) + ct * cw
            qry = jnp.where(qry >= tq, qry - tq, qry)
            s = jnp.where(key <= qry, s, -jnp.inf)
        m_old = m_ref[:, cs]
        m_new = jnp.maximum(m_old, jnp.max(s, axis=0, keepdims=True))
        alpha = jnp.exp2(m_old - m_new)
        p = jnp.exp2(s - m_new)
        l_ref[:, cs] = alpha * l_ref[:, cs] + jnp.sum(p, axis=0, keepdims=True)
        acc_ref[:, cs] = alpha * acc_ref[:, cs] + jnp.dot(vt_c, p.astype(BF16), preferred_element_type=F32)
        m_ref[:, cs] = m_new


def _attn_finish(o_ref, l_ref, acc_ref, tq):
    ot = (acc_ref[...] / l_ref[...]).T
    lane = lax.broadcasted_iota(jnp.int32, (tq, LANES), 1)
    o_ref[...] = jnp.where(lane < HEAD_DIM, ot[:tq], ot[tq:]).astype(o_ref.dtype)


def _attn_prompt_kernel(qt_ref, k_ref, cp_ref, vt_ref, o_ref, wq_ref, m_ref, l_ref, acc_ref, *, tq):
    qi = pl.program_id(2)
    qt = qt_ref[...]
    row = lax.broadcasted_iota(jnp.int32, qt.shape, 0)
    zero = jnp.zeros_like(qt)
    wq_ref[0:LANES, :] = jnp.concatenate([jnp.where(row < HEAD_DIM, qt, zero), jnp.where(row >= HEAD_DIM, qt, zero)],
                                         axis=1)
    wq_ref[LANES:2 * LANES, :] = _ones_rows(tq, pl.program_id(1))
    _attn_init(m_ref, l_ref, acc_ref)

    def chunk(kc, diagonal):
        ksl = pl.ds(pl.multiple_of(kc * tq, tq), tq)
        kaug = jnp.concatenate([k_ref[ksl, :], cp_ref[ksl, :]], axis=1)
        _attn_chunk(kaug, vt_ref[:, ksl], wq_ref, m_ref, l_ref, acc_ref, tq, diagonal)

    def body(kc, carry):
        chunk(kc, False)
        return carry

    lax.fori_loop(0, qi, body, 0)
    chunk(qi, True)
    _attn_finish(o_ref, l_ref, acc_ref, tq)


def _attn_prompt(qt, kb, cp, vt, b, t):
    d = qt.shape[0]
    tq = min(512, t)
    nq = t // tq
    pairs = d // LANES
    return pl.pallas_call(
        functools.partial(_attn_prompt_kernel, tq=tq),
        grid=(b, pairs, nq),
        in_specs=[pl.BlockSpec((LANES, tq), lambda bi, p, qi: (p, bi * nq + qi)),
                  pl.BlockSpec((None, t, LANES), lambda bi, p, qi: (bi, 0, p)),
                  pl.BlockSpec((None, t, LANES), lambda bi, p, qi: (bi, 0, 0)),
                  pl.BlockSpec((LANES, t), lambda bi, p, qi: (p, bi))],
        out_specs=pl.BlockSpec((None, tq, LANES), lambda bi, p, qi: (bi, qi, p)),
        out_shape=jax.ShapeDtypeStruct((b, t, d), BF16),
        scratch_shapes=[pltpu.VMEM((2 * LANES, 2 * tq), BF16), pltpu.VMEM((1, 2 * tq), F32),
                        pltpu.VMEM((1, 2 * tq), F32), pltpu.VMEM((LANES, 2 * tq), F32)],
        compiler_params=_cparams(("parallel", "parallel", "arbitrary")),
        name="attn_prompt",
    )(qt, kb, cp, vt)


def _attn_sample_kernel(wq_ref, kh_ref, vh_ref, cp_ref, kn_ref, vnt_ref, o_ref, *, past, t):
    wq = wq_ref[...]
    s_hist = jnp.dot(jnp.concatenate([kh_ref[...], cp_ref[0:past, :]], axis=1), wq, preferred_element_type=F32)
    s_new = jnp.dot(jnp.concatenate([kn_ref[...], cp_ref[past:past + t, :]], axis=1), wq, preferred_element_type=F32)
    key = lax.broadcasted_iota(jnp.int32, s_new.shape, 0)
    qry = lax.broadcasted_iota(jnp.int32, s_new.shape, 1)
    s_new = jnp.where(key <= jnp.where(qry >= t, qry - t, qry), s_new, -jnp.inf)
    m = jnp.maximum(jnp.max(s_hist, axis=0, keepdims=True), jnp.max(s_new, axis=0, keepdims=True))
    p_hist = jnp.exp2(s_hist - m)
    p_new = jnp.exp2(s_new - m)
    denom = jnp.sum(p_hist, axis=0, keepdims=True) + jnp.sum(p_new, axis=0, keepdims=True)
    acc = (jnp.dot(vh_ref[...].T, p_hist.astype(BF16), preferred_element_type=F32)
           + jnp.dot(vnt_ref[...], p_new.astype(BF16), preferred_element_type=F32))
    ot = (acc / denom).T
    lane = lax.broadcasted_iota(jnp.int32, (t, LANES), 1)
    o_ref[...] = jnp.where(lane < HEAD_DIM, ot[:t], ot[t:]).astype(o_ref.dtype)


def _attn_sample(wq, kn, vnt, hist_k, hist_v, layer, cp):
    b, t, d = kn.shape
    past = hist_k.shape[2]
    pairs = d // LANES
    hist_spec = pl.BlockSpec((None, None, past, LANES), lambda bi, p: (layer, bi, 0, p))
    return pl.pallas_call(
        functools.partial(_attn_sample_kernel, past=past, t=t),
        grid=(b, pairs),
        in_specs=[pl.BlockSpec((None, None, 2 * LANES, 2 * t), lambda bi, p: (bi, p, 0, 0)), hist_spec, hist_spec,
                  pl.BlockSpec((None, cp.shape[1], LANES), lambda bi, p: (bi, 0, 0)),
                  pl.BlockSpec((None, t, LANES), lambda bi, p: (bi, 0, p)),
                  pl.BlockSpec((None, None, LANES, t), lambda bi, p: (bi, p, 0, 0))],
        out_specs=pl.BlockSpec((None, t, LANES), lambda bi, p: (bi, 0, p)),
        out_shape=jax.ShapeDtypeStruct((b, t, d), BF16),
        compiler_params=_cparams(("parallel", "parallel")),
        name="attn_sample",
    )(wq, hist_k, hist_v, cp, kn, vnt)


def _flatten_heads_kernel(x_ref, o_ref):
    o_ref[...] = x_ref[...].reshape(o_ref.shape).astype(o_ref.dtype)


def _flatten_heads(cache):
    nl, b, p, h, dh = cache.shape
    tk = min(512, p)
    return pl.pallas_call(
        _flatten_heads_kernel,
        grid=(nl, b, p // tk),
        in_specs=[pl.BlockSpec((None, None, tk, h, dh), lambda l, bi, i: (l, bi, i, 0, 0))],
        out_specs=pl.BlockSpec((None, None, tk, h * dh), lambda l, bi, i: (l, bi, i, 0)),
        out_shape=jax.ShapeDtypeStruct((nl, b, p, h * dh), BF16),
        compiler_params=_cparams(("parallel", "parallel", "parallel")),
        name="flatten_heads",
    )(cache)


def _sample_query_weights(qb, b, t):
    pairs = D_MODEL // LANES
    q = jnp.transpose(qb.reshape(b, t, pairs, 2, HEAD_DIM), (0, 2, 3, 4, 1))
    zero = jnp.zeros_like(q[:, :, 0])
    top = jnp.concatenate([jnp.concatenate([q[:, :, 0], zero], axis=-1),
                           jnp.concatenate([zero, q[:, :, 1]], axis=-1)], axis=-2)
    ones = jnp.broadcast_to(jnp.stack([_ones_rows(t, p) for p in range(pairs)]), (b, pairs, LANES, 2 * t))
    return jnp.concatenate([top, ones], axis=-2)


def _outproj_kernel(attn_ref, cact_ref, gate_ref, x_ref, wco_ref, wao_ref, wo_ref, g_ref, wpq_ref, keys_ref,
                    x1_ref, h2_ref, st_ref):
    conv_out = jnp.dot(cact_ref[...], wco_ref[...], preferred_element_type=F32)
    attn_out = jnp.dot(attn_ref[...], wao_ref[...], preferred_element_type=F32)
    merged = gate_ref[:, :D_MODEL] * conv_out + gate_ref[:, D_MODEL:] * attn_out
    x1 = x_ref[...] + jnp.dot(merged.astype(BF16), wo_ref[...], preferred_element_type=F32)
    x1_ref[...] = x1
    h2f = x1 * lax.rsqrt(jnp.mean(x1 * x1, axis=-1, keepdims=True) + EPS) * g_ref[...]
    h2 = h2f.astype(BF16)
    h2_ref[...] = h2f.T.astype(BF16)
    qp =jnp.dot(h2, wpq_ref[...], preferred_element_type=F32).astype(BF16)
    for hp in range(2 * PEER_HEADS):
        st_ref[hp] = lax.dot_general(keys_ref[hp], qp[:, hp * LANES:(hp + 1) * LANES], (((1,), (1,)), ((), ())),
                                     preferred_element_type=F32)


def _outproj(attn, cact, gate, x, w):
    n = x.shape[0]
    tm = min(256, n)
    tok = lambda c: pl.BlockSpec((tm, c), lambda i: (i, 0))
    nhp = 2 * PEER_HEADS
    return pl.pallas_call(
        _outproj_kernel,
        grid=(n // tm,),
        in_specs=[tok(D_MODEL), tok(D_CONV), tok(2 * D_MODEL), tok(D_MODEL),
                  _const_spec((D_CONV, D_MODEL)), _const_spec((D_MODEL, D_MODEL)), _const_spec((D_MODEL, D_MODEL)),
                  _const_spec((1, D_MODEL)), _const_spec((D_MODEL, nhp * LANES)),
                  _const_spec((nhp, PEER_KEYS, LANES))],
        out_specs=(tok(D_MODEL), pl.BlockSpec((D_MODEL, tm), lambda i: (0, i)),
                   pl.BlockSpec((nhp, PEER_KEYS, tm), lambda i: (0, 0, i))),
        out_shape=(jax.ShapeDtypeStruct((n, D_MODEL), F32), jax.ShapeDtypeStruct((D_MODEL, n), BF16),
                   jax.ShapeDtypeStruct((nhp, PEER_KEYS, n), F32)),
        compiler_params=_cparams(("parallel",)),
        name="outproj",
    )(attn, cact, gate, x, w["w_conv_out"], w["w_attn_out"], w["w_out"], w["norm_ffn"], w["peer_w_q"], w["peer_keys"])


def _top_ranks(s, break_ties):
    nrow = s.shape[0]
    iota = lax.broadcasted_iota(jnp.int32, s.shape, 0).astype(F32)
    rank = jnp.full(s.shape, float(PEER_TOPK), F32)
    vals = []
    for r in range(PEER_TOPK):
        m = jnp.max(s, axis=0, keepdims=True)
        sel = s == m
        if break_ties:
            sel = iota == jnp.min(jnp.where(sel, iota, float(nrow)), axis=0, keepdims=True)
        rank = jnp.where(sel, float(r), rank)
        s = jnp.where(sel, -jnp.inf, s)
        vals.append(m)
    count = jnp.sum(jnp.where(rank < float(PEER_TOPK), 1.0, 0.0), axis=0, keepdims=True)
    return rank, vals, count


def _staircase_counts(sv1, sv2):
    top = float(PEER_TOPK)
    iota = lax.broadcasted_iota(jnp.int32, sv1.shape, 0).astype(F32)
    n = jnp.zeros(sv1.shape, F32)
    head = sv1 + sv2[0:1, :]
    m0 = head[0:1, :]
    z = jnp.zeros(m0.shape, F32)
    for _ in range(PEER_TOPK):
        m = jnp.max(head, axis=0, keepdims=True)
        a_star = jnp.min(jnp.where(head == m, iota, top), axis=0, keepdims=True)
        onehot = iota == a_star
        z = z + jnp.exp(m - m0)
        n = n + jnp.where(onehot, 1.0, 0.0)
        n_star = jnp.sum(jnp.where(onehot, n, 0.0), axis=0, keepdims=True)
        nxt = jnp.sum(jnp.where(iota == n_star, sv2, 0.0), axis=0, keepdims=True)
        nxt = jnp.where(n_star >= top, -jnp.inf, nxt)
        a_val = jnp.sum(jnp.where(onehot, sv1, 0.0), axis=0, keepdims=True)
        head = jnp.where(onehot, a_val + nxt, head)
    return n, z


def _topk_head(st_ref, r2_ref, nn_ref, e2_ref, cc_ref, h, break_ties):
    top = float(PEER_TOPK)
    s1 = st_ref[2 * h]
    s2 = st_ref[2 * h + 1]
    rank1, v1, count1 = _top_ranks(s1, break_ties)
    rank2, v2, count2 = _top_ranks(s2, break_ties)
    n, z = _staircase_counts(jnp.concatenate(v1, axis=0), jnp.concatenate(v2, axis=0))
    nn = jnp.zeros(s1.shape, F32)
    for a in range(PEER_TOPK):
        nn = jnp.where(rank1 == float(a), n[a:a + 1, :], nn)
    r2_ref[h] = rank2.astype(BF16)
    nn_ref[h] = nn
    e2_ref[h] = jnp.where(rank2 < top, jnp.exp(s2 - v2[0]), 0.0).astype(BF16)
    cc_ref[h] = jnp.where(rank1 < top, jnp.exp(s1 - v1[0]), 0.0) / z
    return jnp.max(jnp.abs(count1 - top) + jnp.abs(count2 - top)) > 0.0


def _topk_kernel(st_ref, r2_ref, nn_ref, e2_ref, cc_ref):
    def body(i, carry):
        tied_a = _topk_head(st_ref, r2_ref, nn_ref, e2_ref, cc_ref, 2 * i, False)
        tied_b = _topk_head(st_ref, r2_ref, nn_ref, e2_ref, cc_ref, 2 * i + 1, False)

        @pl.when(jnp.logical_or(tied_a, tied_b))
        def _():
            _topk_head(st_ref, r2_ref, nn_ref, e2_ref, cc_ref, 2 * i, True)
            _topk_head(st_ref, r2_ref, nn_ref, e2_ref, cc_ref, 2 * i + 1, True)

        return carry

    lax.fori_loop(0, PEER_HEADS // 2, body, 0)


def _topk(st):
    nhp, nkeys, n = st.shape
    tl = LANES
    out_spec = pl.BlockSpec((PEER_HEADS, nkeys, tl), lambda i: (0, 0, i))
    sds = lambda dt: jax.ShapeDtypeStruct((PEER_HEADS, nkeys, n), dt)
    return pl.pallas_call(
        _topk_kernel,
        grid=(n // tl,),
        in_specs=[pl.BlockSpec((nhp, nkeys, tl), lambda i: (0, 0, i))],
        out_specs=(out_spec, out_spec, out_spec, out_spec),
        out_shape=(sds(BF16), sds(F32), sds(BF16), sds(F32)),
        compiler_params=_cparams(("parallel",)),
        name="peer_topk",
    )(st)


GELU_A = -2.0 * 0.7978845608028654 * LOG2E
GELU_B = GELU_A * 0.044715


def _gelu_tanh(x):
    return x / (1.0 + jnp.exp2(x * (GELU_A + GELU_B * (x * x))))


def _peer_kernel(h2t_ref, x1_ref, u_ref, vt_ref, r2_ref, e2_ref, nn_ref, cc_ref, o_ref, acc_ref, a_ref, gw_ref, w_ref,
                 *, ni, ne, steps):
    s = pl.program_id(0)

    @pl.when(s == 0)
    def _():
        a_ref[...] = jnp.zeros(a_ref.shape, F32)
        gw_ref[...] = jnp.zeros(gw_ref.shape, BF16)

    @pl.when(jnp.logical_or(s < 2, lax.rem(jnp.maximum(s - 2, 0), ne) == 0))
    def _():
        acc_ref[...] = jnp.zeros(acc_ref.shape, F32)

    slot_new = lax.rem(s, 2)
    slot_mid = 1 - slot_new
    tm = h2t_ref.shape[1]
    tiles_per_block = nn_ref.shape[1] // ni
    row0 = lax.rem(lax.rem(jnp.clip(s - 1, 0, steps - 1), ne), tiles_per_block) * ni

    def rows_bf16(ref, h, i):
        tile = jnp.broadcast_to(ref[h, pl.ds(row0 + i, 1), :], (16, tm)).astype(BF16)
        return jnp.concatenate([tile] * (PEER_KEYS // 16), axis=0)

    for i in range(ni):
        w = None
        for h in range(PEER_HEADS):
            keep = r2_ref[h] < rows_bf16(nn_ref, h, i)
            term = jnp.where(keep, e2_ref[h] * rows_bf16(cc_ref, h, i), jnp.zeros((), BF16))
            w = term if w is None else w + term
        w_ref[i * PEER_KEYS:(i + 1) * PEER_KEYS, :] = w
    a_ref[slot_new] = jnp.dot(u_ref[...], h2t_ref[...], preferred_element_type=F32)
    acc_ref[...] += jnp.dot(vt_ref[...], gw_ref[slot_new], preferred_element_type=F32)
    for i in range(ni):
        rows = slice(i * PEER_KEYS, (i + 1) * PEER_KEYS)
        gw_ref[slot_mid, rows, :] = _gelu_tanh(a_ref[slot_mid, rows, :]).astype(BF16) * w_ref[rows, :]

    @pl.when(jnp.logical_and(s >= 2, lax.rem(jnp.maximum(s - 2, 0), ne) == ne - 1))
    def _():
        o_ref[...] = x1_ref[...] + acc_ref[...].T


def _peer(h2t, x1, r2, nn, e2, cc, u_tab, vt_tab):
    d, n = h2t.shape
    tm = min(1024, n)
    te = 512
    ne = u_tab.shape[0] // te
    ni = te // PEER_KEYS
    steps = (n // tm) * ne
    pair = lambda s, lag: jnp.clip(s - lag, 0, steps - 1)
    tile = lambda s, lag: pair(s, lag) // ne
    expert = lambda s, lag: lax.rem(pair(s, lag), ne)
    row_block = max(ni, SUBLANE_TILE)
    key_spec = pl.BlockSpec((PEER_HEADS, PEER_KEYS, tm), lambda s: (0, 0, tile(s, 1)))
    row_spec = pl.BlockSpec((PEER_HEADS, row_block, tm), lambda s: (0, expert(s, 1) // (row_block // ni), tile(s, 1)))
    out_spec = pl.BlockSpec((tm, d), lambda s: (tile(s, 2), 0))
    return pl.pallas_call(
        functools.partial(_peer_kernel, ni=ni, ne=ne, steps=steps),
        grid=(steps + 2,),
        in_specs=[pl.BlockSpec((d, tm), lambda s: (0, tile(s, 0))), out_spec,
                  pl.BlockSpec((te, d), lambda s: (expert(s, 0), 0)),
                  pl.BlockSpec((d, te), lambda s: (0, expert(s, 2))), key_spec, key_spec, row_spec, row_spec],
        out_specs=out_spec,
        out_shape=jax.ShapeDtypeStruct((n, d), F32),
        scratch_shapes=[pltpu.VMEM((d, tm), F32), pltpu.VMEM((2, te, tm), F32), pltpu.VMEM((2, te, tm), BF16),
                        pltpu.VMEM((te, tm), BF16)],
        compiler_params=_cparams(("arbitrary",)),
        name="peer_experts",
    )(h2t, x1, u_tab, vt_tab, r2, e2, nn, cc)


def _prep_weights(l, norm_mix, w_in, conv_w, conv_b, conv_norm_g, conv_norm_b, w_conv_out, forget_bias, q_norm,
                  k_norm, w_attn_out, w_out, norm_ffn, peer_w_q, peer_keys, peer_u, peer_v):
    o1 = 2 * D_CONV
    o2 = o1 + 3 * D_MODEL
    o3 = o2 + N_HEADS
    wl = w_in[l]
    head_of_col = jnp.arange(D_MODEL) // HEAD_DIM
    seg = (head_of_col[:, None] == jnp.arange(LANES)[None, :]).astype(BF16)
    row = lambda a: a.reshape(1, -1).astype(F32)
    return {
        "norm_mix": row(norm_mix[l]),
        "w_glu": wl[:, :o1].astype(BF16),
        "w_qkv": wl[:, o1:o2].astype(BF16),
        "w_f": jnp.pad(wl[:, o2:o3], ((0, 0), (0, LANES - N_HEADS))).astype(BF16),
        "w_gate": wl[:, o3:].astype(BF16),
        "f_bias": jnp.pad(row(forget_bias[l]), ((0, 0), (0, LANES - N_HEADS))),
        "q_gain": row(jnp.tile(q_norm[l], N_HEADS) * (HEAD_DIM ** -0.5 * LOG2E)),
        "k_gain": row(jnp.tile(k_norm[l], N_HEADS)),
        "seg": seg,
        "segt": seg.T,
        "conv_w": jnp.pad(conv_w[l], ((0, HIST_ROWS - CONV_WIDTH), (0, 0))),
        "conv_b": row(conv_b[l]),
        "ln_g": row(conv_norm_g[l]),
        "ln_b": row(conv_norm_b[l]),
        "w_conv_out": w_conv_out[l].astype(BF16),
        "w_attn_out": w_attn_out[l].astype(BF16),
        "w_out": w_out[l].astype(BF16),
        "norm_ffn": row(norm_ffn[l]),
        "peer_w_q": peer_w_q[l].astype(BF16),
        "peer_keys": jnp.pad(peer_keys[l].reshape(2 * PEER_HEADS, PEER_KEYS, -1).astype(BF16),
                             ((0, 0), (0, 0), (0, LANES - peer_keys.shape[-1]))),
        "peer_u": peer_u[l].astype(BF16),
        "peer_vt": peer_v[l].T.astype(BF16),
    }


def _layer_group(x3, conv_state, cache, w):
    b, t, d = x3.shape
    n = b * t
    assert t >= CONV_STATE
    x = x3.reshape(n, d)
    u, qb, qt, k, v, kb, vb, vt, logf, gate = _inproj(x, w)
    u3 = u.reshape(b, t, D_CONV)
    if conv_state is None:
        buf = jnp.zeros((b, HIST_ROWS, D_CONV), F32)
    else:
        buf = jnp.pad(conv_state, ((0, 0), (HIST_ROWS - CONV_STATE, 0), (0, 0)))
    cact = _conv(u3, buf, w)

    logf3 = logf.reshape(b, t, N_HEADS)
    if cache is None:
        cp = _cumsum_pieces(logf3)
        attn = _attn_prompt(qt, kb.reshape(b, t, d), cp, vt, b, t)
    else:
        hist_k, hist_v, cache_logf, layer = cache
        past = hist_k.shape[2]
        total = past + t
        padded = -(-total // LANES) * LANES
        lf = jnp.concatenate([cache_logf[layer].astype(F32), logf3], axis=1)
        cp = _cumsum_pieces(jnp.pad(lf, ((0, 0), (0, padded - total), (0, 0))))
        vnt = jnp.transpose(vb.reshape(b, t, d // LANES, LANES), (0, 2, 3, 1))
        attn = _attn_sample(_sample_query_weights(qb, b, t), kb.reshape(b, t, d), vnt, hist_k, hist_v, layer, cp)

    x1, h2, st = _outproj(attn.reshape(n, d), cact.reshape(n, D_CONV), gate, x, w)
    r2, nn, e2, cc = _topk(st)
    y = _peer(h2, x1, r2, nn, e2, cc, w["peer_u"], w["peer_vt"])
    return (y.reshape(b, t, d), k.reshape(b, t, N_HEADS, HEAD_DIM), v.reshape(b, t, N_HEADS, HEAD_DIM),
            logf3, u3[:, t - CONV_STATE:, :])


def kernel(x_prompt, x_sample, cache_k, cache_v, cache_logf, state_conv, norm_mix, w_in, conv_w, conv_b, conv_norm_g, conv_norm_b, w_conv_out, forget_bias, q_norm, k_norm, w_attn_out, w_out, norm_ffn, peer_w_q, peer_keys, peer_u, peer_v):
    depth = w_in.shape[0]
    yp, ys = x_prompt, x_sample
    outs_p, outs_s = [], []
    hist_k = _flatten_heads(cache_k)
    hist_v = _flatten_heads(cache_v)
    for l in range(depth):
        w = _prep_weights(l, norm_mix, w_in, conv_w, conv_b, conv_norm_g, conv_norm_b, w_conv_out, forget_bias,
                          q_norm, k_norm, w_attn_out, w_out, norm_ffn, peer_w_q, peer_keys, peer_u, peer_v)
        yp, *rest_p = _layer_group(yp, None, None, w)
        ys, *rest_s = _layer_group(ys, state_conv[l], (hist_k, hist_v, cache_logf, l), w)
        outs_p.append(rest_p)
        outs_s.append(rest_s)
    stack = lambda outs, i: jnp.stack([o[i] for o in outs])
    return (yp, ys, stack(outs_p, 0), stack(outs_p, 1), stack(outs_p, 2), stack(outs_p, 3),
            stack(outs_s, 0), stack(outs_s, 1), stack(outs_s, 2), stack(outs_s, 3))
```

```python
import functools
import math

import jax
import jax.numpy as jnp
from jax import lax
from jax.experimental import pallas as pl
from jax.experimental.pallas import tpu as pltpu

F32 = jnp.float32
BF16 = jnp.bfloat16

D_MODEL = 1024
N_HEADS = 16
HEAD_DIM = 64
D_CONV = 512
CONV_WIDTH = 31
CONV_STATE = CONV_WIDTH - 1
PEER_HEADS = 8
PEER_KEYS = 128
PEER_TOPK = 16
EPS = 1e-6
LOG2E = math.log2(math.e)
LANES = 128
HIST_ROWS = 32
BIAS_PIECES = 3
VMEM_LIMIT = 56 * 1024 * 1024


def _cparams(sem):
    return pltpu.CompilerParams(dimension_semantics=sem, vmem_limit_bytes=VMEM_LIMIT)


def _const_spec(shape):
    n = len(shape)
    return pl.BlockSpec(shape, lambda *_: (0,) * n)


def _split3(x):
    hi = x.astype(BF16)
    r1 = x - hi.astype(F32)
    mid = r1.astype(BF16)
    lo = (r1 - mid.astype(F32)).astype(BF16)
    return hi, mid, lo


def _split_dot(x, w):
    return sum(jnp.dot(p, w, preferred_element_type=F32) for p in _split3(x))


def _log_sigmoid(x):
    return jnp.minimum(x, 0.0) - jnp.log(1.0 + jnp.exp(-jnp.abs(x)))


def _inproj_kernel(x_ref, g_ref, wglu_ref, wqkv_ref, wf_ref, wgate_ref, fb_ref, qg_ref, kg_ref, seg_ref, segt_ref,
                   u_ref, qb_ref, qt_ref, k_ref, v_ref, kb_ref, vb_ref, vt_ref, logf_ref, gate_ref):
    x = x_ref[...]
    h = x * lax.rsqrt(jnp.mean(x * x, axis=-1, keepdims=True) + EPS) * g_ref[...]
    hb = h.astype(BF16)
    glu = jnp.dot(hb, wglu_ref[...], preferred_element_type=F32)
    u_ref[...] = glu[:, :D_CONV] * jax.nn.sigmoid(glu[:, D_CONV:])
    qkv = jnp.dot(hb, wqkv_ref[...], preferred_element_type=F32)

    def head_norm(t, gain):
        ss = _split_dot(t * t, seg_ref[...])
        r = lax.rsqrt(ss * (1.0 / HEAD_DIM) + EPS)
        return t * _split_dot(r, segt_ref[...]) * gain

    q = head_norm(qkv[:, :D_MODEL], qg_ref[...])
    k = head_norm(qkv[:, D_MODEL:2 * D_MODEL], kg_ref[...])
    v = qkv[:, 2 * D_MODEL:]
    qb_ref[...] = q.astype(BF16)
    qt_ref[...] = q.T.astype(BF16)
    k_ref[...] = k.reshape(k_ref.shape)
    kb_ref[...] = k.astype(BF16)
    v_ref[...] = v.reshape(v_ref.shape)
    vb_ref[...] = v.astype(BF16)
    vt_ref[...] = v.T.astype(BF16)
    f = jnp.dot(hb, wf_ref[...], preferred_element_type=F32) + fb_ref[...]
    logf_ref[...] = _log_sigmoid(f)[:, :N_HEADS]
    gate_ref[...] = jax.nn.sigmoid(jnp.dot(hb, wgate_ref[...], preferred_element_type=F32))


def _inproj(x, w):
    n = x.shape[0]
    tm = min(256, n)
    tok = lambda c: pl.BlockSpec((tm, c), lambda i: (i, 0))
    tok_t = pl.BlockSpec((D_MODEL, tm), lambda i: (0, i))
    tok_heads = pl.BlockSpec((tm, N_HEADS, HEAD_DIM), lambda i: (i, 0, 0))
    sds = jax.ShapeDtypeStruct
    out_shape = (
        sds((n, D_CONV), F32),
        sds((n, D_MODEL), BF16),
        sds((D_MODEL, n), BF16),
        sds((n, N_HEADS, HEAD_DIM), F32),
        sds((n, N_HEADS, HEAD_DIM), F32),
        sds((n, D_MODEL), BF16),
        sds((n, D_MODEL), BF16),
        sds((D_MODEL, n), BF16),
        sds((n, N_HEADS), F32),
        sds((n, 2 * D_MODEL), F32),
    )
    return pl.pallas_call(
        _inproj_kernel,
        grid=(n // tm,),
        in_specs=[tok(D_MODEL), _const_spec((1, D_MODEL)), _const_spec((D_MODEL, 2 * D_CONV)),
                  _const_spec((D_MODEL, 3 * D_MODEL)), _const_spec((D_MODEL, LANES)),
                  _const_spec((D_MODEL, 2 * D_MODEL)), _const_spec((1, LANES)), _const_spec((1, D_MODEL)),
                  _const_spec((1, D_MODEL)), _const_spec((D_MODEL, LANES)), _const_spec((LANES, D_MODEL))],
        out_specs=(tok(D_CONV), tok(D_MODEL), tok_t, tok_heads, tok_heads, tok(D_MODEL), tok(D_MODEL), tok_t,
                   tok(N_HEADS), tok(2 * D_MODEL)),
        out_shape=out_shape,
        compiler_params=_cparams(("parallel",)),
        name="inproj",
    )(x, w["norm_mix"], w["w_glu"], w["w_qkv"], w["w_f"], w["w_gate"], w["f_bias"], w["q_gain"], w["k_gain"],
      w["seg"], w["segt"])


CONV_ROWS = 32
SUBLANE_TILE = 8


def _conv_kernel(u_ref, halo_ref, buf_ref, w_ref, b_ref, lng_ref, lnb_ref, o_ref, pad_ref, *, tt):
    first = pl.program_id(1) == 0
    pad_ref[0, 0:HIST_ROWS, :] = jnp.where(first, buf_ref[...], halo_ref[...])
    pad_ref[0, HIST_ROWS:HIST_ROWS + tt, :] = u_ref[...]
    span = HIST_ROWS + tt - SUBLANE_TILE
    for s in range(1, SUBLANE_TILE):
        pad_ref[s, 0:span, :] = pad_ref[0, s:s + span, :]
    off = HIST_ROWS - CONV_STATE
    for r in range(tt // CONV_ROWS):
        acc = jnp.broadcast_to(b_ref[...], (CONV_ROWS, D_CONV))
        for k in range(CONV_WIDTH):
            shift = (off + k) % SUBLANE_TILE
            start = r * CONV_ROWS + (off + k) - shift
            acc = acc + w_ref[k:k + 1, :] * pad_ref[shift, start:start + CONV_ROWS, :]
        mu = jnp.mean(acc, axis=-1, keepdims=True)
        d = acc - mu
        var = jnp.mean(d * d, axis=-1, keepdims=True)
        y = d * lax.rsqrt(var + EPS) * lng_ref[...] + lnb_ref[...]
        o_ref[r * CONV_ROWS:(r + 1) * CONV_ROWS, :] = (y * jax.nn.sigmoid(y)).astype(BF16)


def _conv(u, buf, w):
    b, t, c = u.shape
    tt = min(256, t)
    hb = tt // HIST_ROWS
    return pl.pallas_call(
        functools.partial(_conv_kernel, tt=tt),
        grid=(b, t // tt),
        in_specs=[pl.BlockSpec((None, tt, c), lambda bi, i: (bi, i, 0)),
                  pl.BlockSpec((None, HIST_ROWS, c), lambda bi, i: (bi, jnp.maximum(i * hb - 1, 0), 0)),
                  pl.BlockSpec((None, HIST_ROWS, c), lambda bi, i: (bi, 0, 0)),
                  _const_spec((HIST_ROWS, c)), _const_spec((1, c)), _const_spec((1, c)), _const_spec((1, c))],
        out_specs=pl.BlockSpec((None, tt, c), lambda bi, i: (bi, i, 0)),
        out_shape=jax.ShapeDtypeStruct((b, t, c), BF16),
        scratch_shapes=[pltpu.VMEM((SUBLANE_TILE, HIST_ROWS + tt, c), F32)],
        compiler_params=_cparams(("parallel", "arbitrary")),
        name="conv",
    )(u, u, buf, w["conv_w"], w["conv_b"], w["ln_g"], w["ln_b"])


def _cumsum_pieces_kernel(x_ref, tri_ref, place_ref, o_ref, *, nblk):
    def body(j, carry):
        sl = pl.ds(pl.multiple_of(j * LANES, LANES), LANES)
        local = sum(jnp.dot(tri_ref[...], piece, preferred_element_type=F32) for piece in _split3(x_ref[sl, :]))
        cs = local + carry
        out = None
        for j2, piece in enumerate(_split3(cs * (-LOG2E))):
            term = jnp.dot(piece, place_ref[j2], preferred_element_type=F32)
            out = term if out is None else out + term
        o_ref[sl, :] = out.astype(BF16)
        return cs[LANES - 1:LANES, :]

    unroll = next(u for u in (4, 3, 2, 1) if nblk % u == 0)
    lax.fori_loop(0, nblk, body, jnp.zeros((1, x_ref.shape[1]), F32), unroll=unroll)


def _cumsum_pieces(logf):
    b, t, h = logf.shape
    ii = lax.broadcasted_iota(jnp.int32, (LANES, LANES), 0)
    jj = lax.broadcasted_iota(jnp.int32, (LANES, LANES), 1)
    tri = (jj <= ii).astype(BF16)
    head = jnp.arange(h)
    col = jnp.arange(LANES)
    place = jnp.stack([(col[None, :] == (head * BIAS_PIECES + j)[:, None]).astype(BF16)
                       for j in range(BIAS_PIECES)])
    return pl.pallas_call(
        functools.partial(_cumsum_pieces_kernel, nblk=t // LANES),
        grid=(b,),
        in_specs=[pl.BlockSpec((None, t, h), lambda i: (i, 0, 0)), _const_spec((LANES, LANES)),
                  _const_spec((BIAS_PIECES, h, LANES))],
        out_specs=pl.BlockSpec((None, t, LANES), lambda i: (i, 0, 0)),
        out_shape=jax.ShapeDtypeStruct((b, t, LANES), BF16),
        compiler_params=_cparams(("parallel",)),
        name="cumsum_pieces",
    )(logf, tri, place)


COL_TILE = 256


def _ones_rows(tq, pair):
    r = lax.broadcasted_iota(jnp.int32, (LANES, 2 * tq), 0) - 2 * BIAS_PIECES * pair
    c = lax.broadcasted_iota(jnp.int32, (LANES, 2 * tq), 1)
    first = jnp.where(c < tq, 0, BIAS_PIECES)
    return jnp.where(r >= first, jnp.where(r < first + BIAS_PIECES, 1.0, 0.0), 0.0).astype(BF16)


def _attn_init(m_ref, l_ref, acc_ref):
    m_ref[...] = jnp.full(m_ref.shape, -jnp.inf, F32)
    l_ref[...] = jnp.zeros(l_ref.shape, F32)
    acc_ref[...] = jnp.zeros(acc_ref.shape, F32)


def _attn_chunk(kaug, vt_c, wq_ref, m_ref, l_ref, acc_ref, tq, diagonal):
    ncol = wq_ref.shape[1]
    cw = min(COL_TILE, ncol)
    tiles = [slice(ct * cw, (ct + 1) * cw) for ct in range(ncol // cw)]
    logits = [jnp.dot(kaug, wq_ref[:, cs], preferred_element_type=F32) for cs in tiles]
    for ct, cs in enumerate(tiles):
        s = logits[ct]
        if diagonal:
            key = lax.broadcasted_iota(jnp.int32, s.shape, 0)
            qry = lax.broadcasted_iota(jnp.int32, s.shape, 1) + ct * cw
            qry = jnp.where(qry >= tq, qry - tq, qry)
            s = jnp.where(key <= qry, s, -jnp.inf)
        m_old = m_ref[:, cs]
        m_new = jnp.maximum(m_old, jnp.max(s, axis=0, keepdims=True))
        alpha = jnp.exp2(m_old - m_new)
        p = jnp.exp2(s - m_new)
        l_ref[:, cs] = alpha * l_ref[:, cs] + jnp.sum(p, axis=0, keepdims=True)
        acc_ref[:, cs] = alpha * acc_ref[:, cs] + jnp.dot(vt_c, p.astype(BF16), preferred_element_type=F32)
        m_ref[:, cs] = m_new


def _attn_finish(o_ref, l_ref, acc_ref, tq):
    ot = (acc_ref[...] / l_ref[...]).T
    lane = lax.broadcasted_iota(jnp.int32, (tq, LANES), 1)
    o_ref[...] = jnp.where(lane < HEAD_DIM, ot[:tq], ot[tq:]).astype(o_ref.dtype)


def _attn_prompt_kernel(qt_ref, k_ref, cp_ref, vt_ref, o_ref, wq_ref, m_ref, l_ref, acc_ref, *, tq):
    qi = pl.program_id(2)
    qt = qt_ref[...]
    row = lax.broadcasted_iota(jnp.int32, qt.shape, 0)
    zero = jnp.zeros_like(qt)
    wq_ref[0:LANES, :] = jnp.concatenate([jnp.where(row < HEAD_DIM, qt, zero), jnp.where(row >= HEAD_DIM, qt, zero)],
                                         axis=1)
    wq_ref[LANES:2 * LANES, :] = _ones_rows(tq, pl.program_id(1))
    _attn_init(m_ref, l_ref, acc_ref)

    def chunk(kc, diagonal):
        ksl = pl.ds(pl.multiple_of(kc * tq, tq), tq)
        kaug = jnp.concatenate([k_ref[ksl, :], cp_ref[ksl, :]], axis=1)
        _attn_chunk(kaug, vt_ref[:, ksl], wq_ref, m_ref, l_ref, acc_ref, tq, diagonal)

    def body(kc, carry):
        chunk(kc, False)
        return carry

    lax.fori_loop(0, qi, body, 0)
    chunk(qi, True)
    _attn_finish(o_ref, l_ref, acc_ref, tq)


def _attn_prompt(qt, kb, cp, vt, b, t):
    d = qt.shape[0]
    tq = min(512, t)
    nq = t // tq
    pairs = d // LANES
    return pl.pallas_call(
        functools.partial(_attn_prompt_kernel, tq=tq),
        grid=(b, pairs, nq),
        in_specs=[pl.BlockSpec((LANES, tq), lambda bi, p, qi: (p, bi * nq + qi)),
                  pl.BlockSpec((None, t, LANES), lambda bi, p, qi: (bi, 0, p)),
                  pl.BlockSpec((None, t, LANES), lambda bi, p, qi: (bi, 0, 0)),
                  pl.BlockSpec((LANES, t), lambda bi, p, qi: (p, bi))],
        out_specs=pl.BlockSpec((None, tq, LANES), lambda bi, p, qi: (bi, qi, p)),
        out_shape=jax.ShapeDtypeStruct((b, t, d), BF16),
        scratch_shapes=[pltpu.VMEM((2 * LANES, 2 * tq), BF16), pltpu.VMEM((1, 2 * tq), F32),
                        pltpu.VMEM((1, 2 * tq), F32), pltpu.VMEM((LANES, 2 * tq), F32)],
        compiler_params=_cparams(("parallel", "parallel", "arbitrary")),
        name="attn_prompt",
    )(qt, kb, cp, vt)


def _attn_sample_kernel(wq_ref, kh_ref, vh_ref, cp_ref, kn_ref, vnt_ref, o_ref, *, past, t):
    wq = wq_ref[...]
    s_hist = jnp.dot(jnp.concatenate([kh_ref[...], cp_ref[0:past, :]], axis=1), wq, preferred_element_type=F32)
    s_new = jnp.dot(jnp.concatenate([kn_ref[...], cp_ref[past:past + t, :]], axis=1), wq, preferred_element_type=F32)
    key = lax.broadcasted_iota(jnp.int32, s_new.shape, 0)
    qry = lax.broadcasted_iota(jnp.int32, s_new.shape, 1)
    s_new = jnp.where(key <= jnp.where(qry >= t, qry - t, qry), s_new, -jnp.inf)
    m = jnp.maximum(jnp.max(s_hist, axis=0, keepdims=True), jnp.max(s_new, axis=0, keepdims=True))
    p_hist = jnp.exp2(s_hist - m)
    p_new = jnp.exp2(s_new - m)
    denom = jnp.sum(p_hist, axis=0, keepdims=True) + jnp.sum(p_new, axis=0, keepdims=True)
    acc = (jnp.dot(vh_ref[...].T, p_hist.astype(BF16), preferred_element_type=F32)
           + jnp.dot(vnt_ref[...], p_new.astype(BF16), preferred_element_type=F32))
    ot = (acc / denom).T
    lane = lax.broadcasted_iota(jnp.int32, (t, LANES), 1)
    o_ref[...] = jnp.where(lane < HEAD_DIM, ot[:t], ot[t:]).astype(o_ref.dtype)


def _attn_sample(wq, kn, vnt, hist_k, hist_v, layer, cp):
    b, t, d = kn.shape
    past = hist_k.shape[2]
    pairs = d // LANES
    hist_spec = pl.BlockSpec((None, None, past, LANES), lambda bi, p: (layer, bi, 0, p))
    return pl.pallas_call(
        functools.partial(_attn_sample_kernel, past=past, t=t),
        grid=(b, pairs),
        in_specs=[pl.BlockSpec((None, None, 2 * LANES, 2 * t), lambda bi, p: (bi, p, 0, 0)), hist_spec, hist_spec,
                  pl.BlockSpec((None, cp.shape[1], LANES), lambda bi, p: (bi, 0, 0)),
                  pl.BlockSpec((None, t, LANES), lambda bi, p: (bi, 0, p)),
                  pl.BlockSpec((None, None, LANES, t), lambda bi, p: (bi, p, 0, 0))],
        out_specs=pl.BlockSpec((None, t, LANES), lambda bi, p: (bi, 0, p)),
        out_shape=jax.ShapeDtypeStruct((b, t, d), BF16),
        compiler_params=_cparams(("parallel", "parallel")),
        name="attn_sample",
    )(wq, hist_k, hist_v, cp, kn, vnt)


def _sample_query_weights(qb, b, t):
    pairs = D_MODEL // LANES
    q = jnp.transpose(qb.reshape(b, t, pairs, 2, HEAD_DIM), (0, 2, 3, 4, 1))
    zero = jnp.zeros_like(q[:, :, 0])
    top = jnp.concatenate([jnp.concatenate([q[:, :, 0], zero], axis=-1),
                           jnp.concatenate([zero, q[:, :, 1]], axis=-1)], axis=-2)
    ones = jnp.broadcast_to(jnp.stack([_ones_rows(t, p) for p in range(pairs)]), (b, pairs, LANES, 2 * t))
    return jnp.concatenate([top, ones], axis=-2)


def _outproj_kernel(attn_ref, cact_ref, gate_ref, x_ref, wco_ref, wao_ref, wo_ref, g_ref, wpq_ref, keys_ref,
                    x1_ref, h2_ref, st_ref):
    conv_out = jnp.dot(cact_ref[...], wco_ref[...], preferred_element_type=F32)
    attn_out = jnp.dot(attn_ref[...], wao_ref[...], preferred_element_type=F32)
    merged = gate_ref[:, :D_MODEL] * conv_out + gate_ref[:, D_MODEL:] * attn_out
    x1 = x_ref[...] + jnp.dot(merged.astype(BF16), wo_ref[...], preferred_element_type=F32)
    x1_ref[...] = x1
    h2f = x1 * lax.rsqrt(jnp.mean(x1 * x1, axis=-1, keepdims=True) + EPS) * g_ref[...]
    h2 = h2f.astype(BF16)
    h2_ref[...] = h2f.T.astype(BF16)
    qp =jnp.dot(h2, wpq_ref[...], preferred_element_type=F32).astype(BF16)
    for hp in range(2 * PEER_HEADS):
        st_ref[hp] = lax.dot_general(keys_ref[hp], qp[:, hp * LANES:(hp + 1) * LANES], (((1,), (1,)), ((), ())),
                                     preferred_element_type=F32)


def _outproj(attn, cact, gate, x, w):
    n = x.shape[0]
    tm = min(256, n)
    tok = lambda c: pl.BlockSpec((tm, c), lambda i: (i, 0))
    nhp = 2 * PEER_HEADS
    return pl.pallas_call(
        _outproj_kernel,
        grid=(n // tm,),
        in_specs=[tok(D_MODEL), tok(D_CONV), tok(2 * D_MODEL), tok(D_MODEL),
                  _const_spec((D_CONV, D_MODEL)), _const_spec((D_MODEL, D_MODEL)), _const_spec((D_MODEL, D_MODEL)),
                  _const_spec((1, D_MODEL)), _const_spec((D_MODEL, nhp * LANES)),
                  _const_spec((nhp, PEER_KEYS, LANES))],
        out_specs=(tok(D_MODEL), pl.BlockSpec((D_MODEL, tm), lambda i: (0, i)),
                   pl.BlockSpec((nhp, PEER_KEYS, tm), lambda i: (0, 0, i))),
        out_shape=(jax.ShapeDtypeStruct((n, D_MODEL), F32), jax.ShapeDtypeStruct((D_MODEL, n), BF16),
                   jax.ShapeDtypeStruct((nhp, PEER_KEYS, n), F32)),
        compiler_params=_cparams(("parallel",)),
        name="outproj",
    )(attn, cact, gate, x, w["w_conv_out"], w["w_attn_out"], w["w_out"], w["norm_ffn"], w["peer_w_q"], w["peer_keys"])


def _top_ranks(s, break_ties):
    nrow = s.shape[0]
    iota = lax.broadcasted_iota(jnp.int32, s.shape, 0).astype(F32)
    rank = jnp.full(s.shape, float(PEER_TOPK), F32)
    vals = []
    for r in range(PEER_TOPK):
        m = jnp.max(s, axis=0, keepdims=True)
        sel = s == m
        if break_ties:
            sel = iota == jnp.min(jnp.where(sel, iota, float(nrow)), axis=0, keepdims=True)
        rank = jnp.where(sel, float(r), rank)
        s = jnp.where(sel, -jnp.inf, s)
        vals.append(m)
    count = jnp.sum(jnp.where(rank < float(PEER_TOPK), 1.0, 0.0), axis=0, keepdims=True)
    return rank, vals, count


def _staircase_counts(sv1, sv2):
    top = float(PEER_TOPK)
    iota = lax.broadcasted_iota(jnp.int32, sv1.shape, 0).astype(F32)
    n = jnp.zeros(sv1.shape, F32)
    head = sv1 + sv2[0:1, :]
    m0 = head[0:1, :]
    z = jnp.zeros(m0.shape, F32)
    for _ in range(PEER_TOPK):
        m = jnp.max(head, axis=0, keepdims=True)
        a_star = jnp.min(jnp.where(head == m, iota, top), axis=0, keepdims=True)
        onehot = iota == a_star
        z = z + jnp.exp(m - m0)
        n = n + jnp.where(onehot, 1.0, 0.0)
        n_star = jnp.sum(jnp.where(onehot, n, 0.0), axis=0, keepdims=True)
        nxt = jnp.sum(jnp.where(iota == n_star, sv2, 0.0), axis=0, keepdims=True)
        nxt = jnp.where(n_star >= top, -jnp.inf, nxt)
        a_val = jnp.sum(jnp.where(onehot, sv1, 0.0), axis=0, keepdims=True)
        head = jnp.where(onehot, a_val + nxt, head)
    return n, z


def _topk_head(st_ref, r2_ref, nn_ref, e2_ref, cc_ref, h, break_ties):
    top = float(PEER_TOPK)
    s1 = st_ref[2 * h]
    s2 = st_ref[2 * h + 1]
    rank1, v1, count1 = _top_ranks(s1, break_ties)
    rank2, v2, count2 = _top_ranks(s2, break_ties)
    n, z = _staircase_counts(jnp.concatenate(v1, axis=0), jnp.concatenate(v2, axis=0))
    nn = jnp.zeros(s1.shape, F32)
    for a in range(PEER_TOPK):
        nn = jnp.where(rank1 == float(a), n[a:a + 1, :], nn)
    r2_ref[h] = rank2.astype(BF16)
    nn_ref[h] = nn
    e2_ref[h] = jnp.where(rank2 < top, jnp.exp(s2 - v2[0]), 0.0).astype(BF16)
    cc_ref[h] = jnp.where(rank1 < top, jnp.exp(s1 - v1[0]), 0.0) / z
    return jnp.max(jnp.abs(count1 - top) + jnp.abs(count2 - top)) > 0.0


def _topk_kernel(st_ref, r2_ref, nn_ref, e2_ref, cc_ref):
    def body(i, carry):
        tied_a = _topk_head(st_ref, r2_ref, nn_ref, e2_ref, cc_ref, 2 * i, False)
        tied_b = _topk_head(st_ref, r2_ref, nn_ref, e2_ref, cc_ref, 2 * i + 1, False)

        @pl.when(jnp.logical_or(tied_a, tied_b))
        def _():
            _topk_head(st_ref, r2_ref, nn_ref, e2_ref, cc_ref, 2 * i, True)
            _topk_head(st_ref, r2_ref, nn_ref, e2_ref, cc_ref, 2 * i + 1, True)

        return carry

    lax.fori_loop(0, PEER_HEADS // 2, body, 0)


def _topk(st):
    nhp, nkeys, n = st.shape
    tl = LANES
    out_spec = pl.BlockSpec((PEER_HEADS, nkeys, tl), lambda i: (0, 0, i))
    sds = lambda dt: jax.ShapeDtypeStruct((PEER_HEADS, nkeys, n), dt)
    return pl.pallas_call(
        _topk_kernel,
        grid=(n // tl,),
        in_specs=[pl.BlockSpec((nhp, nkeys, tl), lambda i: (0, 0, i))],
        out_specs=(out_spec, out_spec, out_spec, out_spec),
        out_shape=(sds(BF16), sds(F32), sds(BF16), sds(F32)),
        compiler_params=_cparams(("parallel",)),
        name="peer_topk",
    )(st)


GELU_A = -2.0 * 0.7978845608028654 * LOG2E
GELU_B = GELU_A * 0.044715


def _gelu_tanh(x):
    return x / (1.0 + jnp.exp2(x * (GELU_A + GELU_B * (x * x))))


def _peer_kernel(h2t_ref, x1_ref, u_ref, vt_ref, r2_ref, e2_ref, nn_ref, cc_ref, o_ref, acc_ref, a_ref, gw_ref, w_ref,
                 *, ni, ne, steps):
    s = pl.program_id(0)

    @pl.when(s == 0)
    def _():
        a_ref[...] = jnp.zeros(a_ref.shape, F32)
        gw_ref[...] = jnp.zeros(gw_ref.shape, BF16)

    @pl.when(jnp.logical_or(s < 2, lax.rem(jnp.maximum(s - 2, 0), ne) == 0))
    def _():
        acc_ref[...] = jnp.zeros(acc_ref.shape, F32)

    slot_new = lax.rem(s, 2)
    slot_mid = 1 - slot_new
    tm = h2t_ref.shape[1]
    tiles_per_block = nn_ref.shape[1] // ni
    row0 = lax.rem(lax.rem(jnp.clip(s - 1, 0, steps - 1), ne), tiles_per_block) * ni

    def rows_bf16(ref, h, i):
        tile = jnp.broadcast_to(ref[h, pl.ds(row0 + i, 1), :], (16, tm)).astype(BF16)
        return jnp.concatenate([tile] * (PEER_KEYS // 16), axis=0)

    for i in range(ni):
        w = None
        for h in range(PEER_HEADS):
            keep = r2_ref[h] < rows_bf16(nn_ref, h, i)
            term = jnp.where(keep, e2_ref[h] * rows_bf16(cc_ref, h, i), jnp.zeros((), BF16))
            w = term if w is None else w + term
        w_ref[i * PEER_KEYS:(i + 1) * PEER_KEYS, :] = w
    a_ref[slot_new] = jnp.dot(u_ref[...], h2t_ref[...], preferred_element_type=F32)
    acc_ref[...] += jnp.dot(vt_ref[...], gw_ref[slot_new], preferred_element_type=F32)
    for i in range(ni):
        rows = slice(i * PEER_KEYS, (i + 1) * PEER_KEYS)
        gw_ref[slot_mid, rows, :] = _gelu_tanh(a_ref[slot_mid, rows, :]).astype(BF16) * w_ref[rows, :]

    @pl.when(jnp.logical_and(s >= 2, lax.rem(jnp.maximum(s - 2, 0), ne) == ne - 1))
    def _():
        o_ref[...] = x1_ref[...] + acc_ref[...].T


def _peer(h2t, x1, r2, nn, e2, cc, u_tab, vt_tab):
    d, n = h2t.shape
    tm = min(1024, n)
    te = 512
    ne = u_tab.shape[0] // te
    ni = te // PEER_KEYS
    steps = (n // tm) * ne
    pair = lambda s, lag: jnp.clip(s - lag, 0, steps - 1)
    tile = lambda s, lag: pair(s, lag) // ne
    expert = lambda s, lag: lax.rem(pair(s, lag), ne)
    row_block = max(ni, SUBLANE_TILE)
    key_spec = pl.BlockSpec((PEER_HEADS, PEER_KEYS, tm), lambda s: (0, 0, tile(s, 1)))
    row_spec = pl.BlockSpec((PEER_HEADS, row_block, tm), lambda s: (0, expert(s, 1) // (row_block // ni), tile(s, 1)))
    out_spec = pl.BlockSpec((tm, d), lambda s: (tile(s, 2), 0))
    return pl.pallas_call(
        functools.partial(_peer_kernel, ni=ni, ne=ne, steps=steps),
        grid=(steps + 2,),
        in_specs=[pl.BlockSpec((d, tm), lambda s: (0, tile(s, 0))), out_spec,
                  pl.BlockSpec((te, d), lambda s: (expert(s, 0), 0)),
                  pl.BlockSpec((d, te), lambda s: (0, expert(s, 2))), key_spec, key_spec, row_spec, row_spec],
        out_specs=out_spec,
        out_shape=jax.ShapeDtypeStruct((n, d), F32),
        scratch_shapes=[pltpu.VMEM((d, tm), F32), pltpu.VMEM((2, te, tm), F32), pltpu.VMEM((2, te, tm), BF16),
                        pltpu.VMEM((te, tm), BF16)],
        compiler_params=_cparams(("arbitrary",)),
        name="peer_experts",
    )(h2t, x1, u_tab, vt_tab, r2, e2, nn, cc)


def _prep_weights(l, norm_mix, w_in, conv_w, conv_b, conv_norm_g, conv_norm_b, w_conv_out, forget_bias, q_norm,
                  k_norm, w_attn_out, w_out, norm_ffn, peer_w_q, peer_keys, peer_u, peer_v):
    o1 = 2 * D_CONV
    o2 = o1 + 3 * D_MODEL
    o3 = o2 + N_HEADS
    wl = w_in[l]
    head_of_col = jnp.arange(D_MODEL) // HEAD_DIM
    seg = (head_of_col[:, None] == jnp.arange(LANES)[None, :]).astype(BF16)
    row = lambda a: a.reshape(1, -1).astype(F32)
    return {
        "norm_mix": row(norm_mix[l]),
        "w_glu": wl[:, :o1].astype(BF16),
        "w_qkv": wl[:, o1:o2].astype(BF16),
        "w_f": jnp.pad(wl[:, o2:o3], ((0, 0), (0, LANES - N_HEADS))).astype(BF16),
        "w_gate": wl[:, o3:].astype(BF16),
        "f_bias": jnp.pad(row(forget_bias[l]), ((0, 0), (0, LANES - N_HEADS))),
        "q_gain": row(jnp.tile(q_norm[l], N_HEADS) * (HEAD_DIM ** -0.5 * LOG2E)),
        "k_gain": row(jnp.tile(k_norm[l], N_HEADS)),
        "seg": seg,
        "segt": seg.T,
        "conv_w": jnp.pad(conv_w[l], ((0, HIST_ROWS - CONV_WIDTH), (0, 0))),
        "conv_b": row(conv_b[l]),
        "ln_g": row(conv_norm_g[l]),
        "ln_b": row(conv_norm_b[l]),
        "w_conv_out": w_conv_out[l].astype(BF16),
        "w_attn_out": w_attn_out[l].astype(BF16),
        "w_out": w_out[l].astype(BF16),
        "norm_ffn": row(norm_ffn[l]),
        "peer_w_q": peer_w_q[l].astype(BF16),
        "peer_keys": jnp.pad(peer_keys[l].reshape(2 * PEER_HEADS, PEER_KEYS, -1).astype(BF16),
                             ((0, 0), (0, 0), (0, LANES - peer_keys.shape[-1]))),
        "peer_u": peer_u[l].astype(BF16),
        "peer_vt": peer_v[l].T.astype(BF16),
    }


def _layer_group(x3, conv_state, cache, w):
    b, t, d = x3.shape
    n = b * t
    assert t >= CONV_STATE
    x = x3.reshape(n, d)
    u, qb, qt, k, v, kb, vb, vt, logf, gate = _inproj(x, w)
    u3 = u.reshape(b, t, D_CONV)
    if conv_state is None:
        buf = jnp.zeros((b, HIST_ROWS, D_CONV), F32)
    else:
        buf = jnp.pad(conv_state, ((0, 0), (HIST_ROWS - CONV_STATE, 0), (0, 0)))
    cact = _conv(u3, buf, w)

    logf3 = logf.reshape(b, t, N_HEADS)
    if cache is None:
        cp = _cumsum_pieces(logf3)
        attn = _attn_prompt(qt, kb.reshape(b, t, d), cp, vt, b, t)
    else:
        hist_k, hist_v, cache_logf, layer = cache
        past = hist_k.shape[2]
        total = past + t
        padded = -(-total // LANES) * LANES
        lf = jnp.concatenate([cache_logf[layer].astype(F32), logf3], axis=1)
        cp = _cumsum_pieces(jnp.pad(lf, ((0, 0), (0, padded - total), (0, 0))))
        vnt = jnp.transpose(vb.reshape(b, t, d // LANES, LANES), (0, 2, 3, 1))
        attn = _attn_sample(_sample_query_weights(qb, b, t), kb.reshape(b, t, d), vnt, hist_k, hist_v, layer, cp)

    x1, h2, st = _outproj(attn.reshape(n, d), cact.reshape(n, D_CONV), gate, x, w)
    r2, nn, e2, cc = _topk(st)
    y = _peer(h2, x1, r2, nn, e2, cc, w["peer_u"], w["peer_vt"])
    return (y.reshape(b, t, d), k.reshape(b, t, N_HEADS, HEAD_DIM), v.reshape(b, t, N_HEADS, HEAD_DIM),
            logf3, u3[:, t - CONV_STATE:, :])


def kernel(x_prompt, x_sample, cache_k, cache_v, cache_logf, state_conv, norm_mix, w_in, conv_w, conv_b, conv_norm_g, conv_norm_b, w_conv_out, forget_bias, q_norm, k_norm, w_attn_out, w_out, norm_ffn, peer_w_q, peer_keys, peer_u, peer_v):
    depth = w_in.shape[0]
    yp, ys = x_prompt, x_sample
    outs_p, outs_s = [], []
    flat = cache_k.shape[:3] + (D_MODEL,)
    hist_k = cache_k.reshape(flat).astype(BF16)
    hist_v = cache_v.reshape(flat).astype(BF16)
    for l in range(depth):
        w = _prep_weights(l, norm_mix, w_in, conv_w, conv_b, conv_norm_g, conv_norm_b, w_conv_out, forget_bias,
                          q_norm, k_norm, w_attn_out, w_out, norm_ffn, peer_w_q, peer_keys, peer_u, peer_v)
        yp, *rest_p = _layer_group(yp, None, None, w)
        ys, *rest_s = _layer_group(ys, state_conv[l], (hist_k, hist_v, cache_logf, l), w)
        outs_p.append(rest_p)
        outs_s.append(rest_s)
    stack = lambda outs, i: jnp.stack([o[i] for o in outs])
    return (yp, ys, stack(outs_p, 0), stack(outs_p, 1), stack(outs_p, 2), stack(outs_p, 3),
            stack(outs_s, 0), stack(outs_s, 1), stack(outs_s, 2), stack(outs_s, 3))
```

```python
import functools
import math

import jax
import jax.numpy as jnp
from jax import lax
from jax.experimental import pallas as pl
from jax.experimental.pallas import tpu as pltpu

F32 = jnp.float32
BF16 = jnp.bfloat16

D_MODEL = 1024
N_HEADS = 16
HEAD_DIM = 64
D_CONV = 512
CONV_WIDTH = 31
CONV_STATE = CONV_WIDTH - 1
PEER_HEADS = 8
PEER_KEYS = 128
PEER_TOPK = 16
EPS = 1e-6
LOG2E = math.log2(math.e)
LANES = 128
HIST_ROWS = 32
BIAS_PIECES = 3
VMEM_LIMIT = 56 * 1024 * 1024


def _cparams(sem):
    return pltpu.CompilerParams(dimension_semantics=sem, vmem_limit_bytes=VMEM_LIMIT)


def _const_spec(shape):
    n = len(shape)
    return pl.BlockSpec(shape, lambda *_: (0,) * n)


def _split3(x):
    hi = x.astype(BF16)
    r1 = x - hi.astype(F32)
    mid = r1.astype(BF16)
    lo = (r1 - mid.astype(F32)).astype(BF16)
    return hi, mid, lo


def _split_dot(x, w):
    return sum(jnp.dot(p, w, preferred_element_type=F32) for p in _split3(x))


def _log_sigmoid(x):
    return jnp.minimum(x, 0.0) - jnp.log(1.0 + jnp.exp(-jnp.abs(x)))


def _inproj_kernel(x_ref, g_ref, wglu_ref, wqkv_ref, wf_ref, wgate_ref, fb_ref, qg_ref, kg_ref, seg_ref, segt_ref,
                   u_ref, qb_ref, qt_ref, k_ref, v_ref, kb_ref, vb_ref, vt_ref, logf_ref, gate_ref):
    x = x_ref[...]
    h = x * lax.rsqrt(jnp.mean(x * x, axis=-1, keepdims=True) + EPS) * g_ref[...]
    hb = h.astype(BF16)
    glu = jnp.dot(hb, wglu_ref[...], preferred_element_type=F32)
    u_ref[...] = glu[:, :D_CONV] * jax.nn.sigmoid(glu[:, D_CONV:])
    qkv = jnp.dot(hb, wqkv_ref[...], preferred_element_type=F32)

    def head_norm(t, gain):
        ss = _split_dot(t * t, seg_ref[...])
        r = lax.rsqrt(ss * (1.0 / HEAD_DIM) + EPS)
        return t * _split_dot(r, segt_ref[...]) * gain

    q = head_norm(qkv[:, :D_MODEL], qg_ref[...])
    k = head_norm(qkv[:, D_MODEL:2 * D_MODEL], kg_ref[...])
    v = qkv[:, 2 * D_MODEL:]
    qb_ref[...] = q.astype(BF16)
    qt_ref[...] = q.T.astype(BF16)
    k_ref[...] = k.reshape(k_ref.shape)
    kb_ref[...] = k.astype(BF16)
    v_ref[...] = v.reshape(v_ref.shape)
    vb_ref[...] = v.astype(BF16)
    vt_ref[...] = v.T.astype(BF16)
    f = jnp.dot(hb, wf_ref[...], preferred_element_type=F32) + fb_ref[...]
    logf_ref[...] = _log_sigmoid(f)[:, :N_HEADS]
    gate_ref[...] = jax.nn.sigmoid(jnp.dot(hb, wgate_ref[...], preferred_element_type=F32))


def _inproj(x, w):
    n = x.shape[0]
    tm = min(256, n)
    tok = lambda c: pl.BlockSpec((tm, c), lambda i: (i, 0))
    tok_t = pl.BlockSpec((D_MODEL, tm), lambda i: (0, i))
    tok_heads = pl.BlockSpec((tm, N_HEADS, HEAD_DIM), lambda i: (i, 0, 0))
    sds = jax.ShapeDtypeStruct
    out_shape = (
        sds((n, D_CONV), F32),
        sds((n, D_MODEL), BF16),
        sds((D_MODEL, n), BF16),
        sds((n, N_HEADS, HEAD_DIM), F32),
        sds((n, N_HEADS, HEAD_DIM), F32),
        sds((n, D_MODEL), BF16),
        sds((n, D_MODEL), BF16),
        sds((D_MODEL, n), BF16),
        sds((n, N_HEADS), F32),
        sds((n, 2 * D_MODEL), F32),
    )
    return pl.pallas_call(
        _inproj_kernel,
        grid=(n // tm,),
        in_specs=[tok(D_MODEL), _const_spec((1, D_MODEL)), _const_spec((D_MODEL, 2 * D_CONV)),
                  _const_spec((D_MODEL, 3 * D_MODEL)), _const_spec((D_MODEL, LANES)),
                  _const_spec((D_MODEL, 2 * D_MODEL)), _const_spec((1, LANES)), _const_spec((1, D_MODEL)),
                  _const_spec((1, D_MODEL)), _const_spec((D_MODEL, LANES)), _const_spec((LANES, D_MODEL))],
        out_specs=(tok(D_CONV), tok(D_MODEL), tok_t, tok_heads, tok_heads, tok(D_MODEL), tok(D_MODEL), tok_t,
                   tok(N_HEADS), tok(2 * D_MODEL)),
        out_shape=out_shape,
        compiler_params=_cparams(("parallel",)),
        name="inproj",
    )(x, w["norm_mix"], w["w_glu"], w["w_qkv"], w["w_f"], w["w_gate"], w["f_bias"], w["q_gain"], w["k_gain"],
      w["seg"], w["segt"])


CONV_ROWS = 32
SUBLANE_TILE = 8


def _conv_kernel(u_ref, halo_ref, buf_ref, w_ref, b_ref, lng_ref, lnb_ref, o_ref, pad_ref, *, tt):
    first = pl.program_id(1) == 0
    pad_ref[0, 0:HIST_ROWS, :] = jnp.where(first, buf_ref[...], halo_ref[...])
    pad_ref[0, HIST_ROWS:HIST_ROWS + tt, :] = u_ref[...]
    span = HIST_ROWS + tt - SUBLANE_TILE
    for s in range(1, SUBLANE_TILE):
        pad_ref[s, 0:span, :] = pad_ref[0, s:s + span, :]
    off = HIST_ROWS - CONV_STATE
    for r in range(tt // CONV_ROWS):
        acc = jnp.broadcast_to(b_ref[...], (CONV_ROWS, D_CONV))
        for k in range(CONV_WIDTH):
            shift = (off + k) % SUBLANE_TILE
            start = r * CONV_ROWS + (off + k) - shift
            acc = acc + w_ref[k:k + 1, :] * pad_ref[shift, start:start + CONV_ROWS, :]
        mu = jnp.mean(acc, axis=-1, keepdims=True)
        d = acc - mu
        var = jnp.mean(d * d, axis=-1, keepdims=True)
        y = d * lax.rsqrt(var + EPS) * lng_ref[...] + lnb_ref[...]
        o_ref[r * CONV_ROWS:(r + 1) * CONV_ROWS, :] = (y * jax.nn.sigmoid(y)).astype(BF16)


def _conv(u, buf, w):
    b, t, c = u.shape
    tt = min(256, t)
    hb = tt // HIST_ROWS
    return pl.pallas_call(
        functools.partial(_conv_kernel, tt=tt),
        grid=(b, t // tt),
        in_specs=[pl.BlockSpec((None, tt, c), lambda bi, i: (bi, i, 0)),
                  pl.BlockSpec((None, HIST_ROWS, c), lambda bi, i: (bi, jnp.maximum(i * hb - 1, 0), 0)),
                  pl.BlockSpec((None, HIST_ROWS, c), lambda bi, i: (bi, 0, 0)),
                  _const_spec((HIST_ROWS, c)), _const_spec((1, c)), _const_spec((1, c)), _const_spec((1, c))],
        out_specs=pl.BlockSpec((None, tt, c), lambda bi, i: (bi, i, 0)),
        out_shape=jax.ShapeDtypeStruct((b, t, c), BF16),
        scratch_shapes=[pltpu.VMEM((SUBLANE_TILE, HIST_ROWS + tt, c), F32)],
        compiler_params=_cparams(("parallel", "arbitrary")),
        name="conv",
    )(u, u, buf, w["conv_w"], w["conv_b"], w["ln_g"], w["ln_b"])


def _cumsum_pieces_kernel(x_ref, tri_ref, place_ref, o_ref, *, nblk):
    def body(j, carry):
        sl = pl.ds(pl.multiple_of(j * LANES, LANES), LANES)
        local = sum(jnp.dot(tri_ref[...], piece, preferred_element_type=F32) for piece in _split3(x_ref[sl, :]))
        cs = local + carry
        out = None
        for j2, piece in enumerate(_split3(cs * (-LOG2E))):
            term = jnp.dot(piece, place_ref[j2], preferred_element_type=F32)
            out = term if out is None else out + term
        o_ref[sl, :] = out.astype(BF16)
        return cs[LANES - 1:LANES, :]

    unroll = next(u for u in (4, 3, 2, 1) if nblk % u == 0)
    lax.fori_loop(0, nblk, body, jnp.zeros((1, x_ref.shape[1]), F32), unroll=unroll)


def _cumsum_pieces(logf):
    b, t, h = logf.shape
    ii = lax.broadcasted_iota(jnp.int32, (LANES, LANES), 0)
    jj = lax.broadcasted_iota(jnp.int32, (LANES, LANES), 1)
    tri = (jj <= ii).astype(BF16)
    head = jnp.arange(h)
    col = jnp.arange(LANES)
    place = jnp.stack([(col[None, :] == (head * BIAS_PIECES + j)[:, None]).astype(BF16)
                       for j in range(BIAS_PIECES)])
    return pl.pallas_call(
        functools.partial(_cumsum_pieces_kernel, nblk=t // LANES),
        grid=(b,),
        in_specs=[pl.BlockSpec((None, t, h), lambda i: (i, 0, 0)), _const_spec((LANES, LANES)),
                  _const_spec((BIAS_PIECES, h, LANES))],
        out_specs=pl.BlockSpec((None, t, LANES), lambda i: (i, 0, 0)),
        out_shape=jax.ShapeDtypeStruct((b, t, LANES), BF16),
        compiler_params=_cparams(("parallel",)),
        name="cumsum_pieces",
    )(logf, tri, place)


COL_TILE = 256

def _ones_rows(tq, pair):
    r = lax.broadcasted_iota(jnp.int32, (LANES, 2 * tq), 0) - 2 * BIAS_PIECES * pair
    c = lax.broadcasted_iota(jnp.int32, (LANES, 2 * tq), 1)
    first = jnp.where(c < tq, 0, BIAS_PIECES)
    return jnp.where(r >= first, jnp.where(r < first + BIAS_PIECES, 1.0, 0.0), 0.0).astype(BF16)


def _attn_init(m_ref, l_ref, acc_ref):
    m_ref[...] = jnp.full(m_ref.shape, -jnp.inf, F32)
    l_ref[...] = jnp.zeros(l_ref.shape, F32)
    acc_ref[...] = jnp.zeros(acc_ref.shape, F32)


def _attn_chunk(kaug, vt_c, wq_ref, m_ref, l_ref, acc_ref, tq, diagonal):
    ncol = wq_ref.shape[1]
    cw = min(COL_TILE, ncol)
    tiles = [slice(ct * cw, (ct + 1) * cw) for ct in range(ncol // cw)]
    logits = [jnp.dot(kaug, wq_ref[:, cs], preferred_element_type=F32) for cs in tiles]
    for ct, cs in enumerate(tiles):
        s = logits[ct]
        if diagonal:
            key = lax.broadcasted_iota(jnp.int32, s.shape, 0)
            qry = lax.broadcasted_iota(jnp.int32, s.shape, 1) + ct * cw
            qry = jnp.where(qry >= tq, qry - tq, qry)
            s = jnp.where(key <= qry, s, -jnp.inf)
        m_old = m_ref[:, cs]
        m_new = jnp.maximum(m_old, jnp.max(s, axis=0, keepdims=True))
        alpha = jnp.exp2(m_old - m_new)
        p = jnp.exp2(s - m_new)
        l_ref[:, cs] = alpha * l_ref[:, cs] + jnp.sum(p, axis=0, keepdims=True)
        acc_ref[:, cs] = alpha * acc_ref[:, cs] + jnp.dot(vt_c, p.astype(BF16), preferred_element_type=F32)
        m_ref[:, cs] = m_new


def _attn_finish(o_ref, l_ref, acc_ref, tq):
    ot = (acc_ref[...] / l_ref[...]).T
    lane = lax.broadcasted_iota(jnp.int32, (tq, LANES), 1)
    o_ref[...] = jnp.where(lane < HEAD_DIM, ot[:tq], ot[tq:]).astype(o_ref.dtype)


def _attn_prompt_kernel(qt_ref, k_ref, cp_ref, vt_ref, o_ref, wq_ref, m_ref, l_ref, acc_ref, *, tq):
    qi = pl.program_id(2)
    qt = qt_ref[...]
    row = lax.broadcasted_iota(jnp.int32, qt.shape, 0)
    zero = jnp.zeros_like(qt)
    wq_ref[0:LANES, :] = jnp.concatenate([jnp.where(row < HEAD_DIM, qt, zero), jnp.where(row >= HEAD_DIM, qt, zero)],
                                         axis=1)
    wq_ref[LANES:2 * LANES, :] = _ones_rows(tq, pl.program_id(1))
    _attn_init(m_ref, l_ref, acc_ref)

    def chunk(kc, diagonal):
        ksl = pl.ds(pl.multiple_of(kc * tq, tq), tq)
        kaug = jnp.concatenate([k_ref[ksl, :], cp_ref[ksl, :]], axis=1)
        _attn_chunk(kaug, vt_ref[:, ksl], wq_ref, m_ref, l_ref, acc_ref, tq, diagonal)

    def body(kc, carry):
        chunk(kc, False)
        return carry

    lax.fori_loop(0, qi, body, 0)
    chunk(qi, True)
    _attn_finish(o_ref, l_ref, acc_ref, tq)


def _attn_prompt(qt, kb, cp, vt, b, t):
    d = qt.shape[0]
    tq = min(512, t)
    nq = t // tq
    pairs = d // LANES
    return pl.pallas_call(
        functools.partial(_attn_prompt_kernel, tq=tq),
        grid=(b, pairs, nq),
        in_specs=[pl.BlockSpec((LANES, tq), lambda bi, p, qi: (p, bi * nq + qi)),
                  pl.BlockSpec((None, t, LANES), lambda bi, p, qi: (bi, 0, p)),
                  pl.BlockSpec((None, t, LANES), lambda bi, p, qi: (bi, 0, 0)),
                  pl.BlockSpec((LANES, t), lambda bi, p, qi: (p, bi))],
        out_specs=pl.BlockSpec((None, tq, LANES), lambda bi, p, qi: (bi, qi, p)),
        out_shape=jax.ShapeDtypeStruct((b, t, d), BF16),
        scratch_shapes=[pltpu.VMEM((2 * LANES, 2 * tq), BF16), pltpu.VMEM((1, 2 * tq), F32),
                        pltpu.VMEM((1, 2 * tq), F32), pltpu.VMEM((LANES, 2 * tq), F32)],
        compiler_params=_cparams(("parallel", "parallel", "arbitrary")),
        name="attn_prompt",
    )(qt, kb, cp, vt)


def _attn_sample_kernel(wq_ref, kh_ref, vh_ref, cp_ref, kn_ref, vnt_ref, o_ref, *, past, t):
    wq = wq_ref[...]
    s_hist = jnp.dot(jnp.concatenate([kh_ref[...].astype(BF16), cp_ref[0:past, :]], axis=1), wq, preferred_element_type=F32)
    s_new = jnp.dot(jnp.concatenate([kn_ref[...], cp_ref[past:past + t, :]], axis=1), wq, preferred_element_type=F32)
    key = lax.broadcasted_iota(jnp.int32, s_new.shape, 0)
    qry = lax.broadcasted_iota(jnp.int32, s_new.shape, 1)
    s_new = jnp.where(key <= jnp.where(qry >= t, qry - t, qry), s_new, -jnp.inf)
    m = jnp.maximum(jnp.max(s_hist, axis=0, keepdims=True), jnp.max(s_new, axis=0, keepdims=True))
    p_hist = jnp.exp2(s_hist - m)
    p_new = jnp.exp2(s_new - m)
    denom = jnp.sum(p_hist, axis=0, keepdims=True) + jnp.sum(p_new, axis=0, keepdims=True)
    acc = (jnp.dot(vh_ref[...].astype(BF16).T, p_hist.astype(BF16), preferred_element_type=F32)
           + jnp.dot(vnt_ref[...], p_new.astype(BF16), preferred_element_type=F32))
    ot = (acc / denom).T
    lane = lax.broadcasted_iota(jnp.int32, (t, LANES), 1)
    o_ref[...] = jnp.where(lane < HEAD_DIM, ot[:t], ot[t:]).astype(o_ref.dtype)


def _attn_sample(wq, kn, vnt, hist_k, hist_v, layer, cp):
    b, t, d = kn.shape
    past = hist_k.shape[2]
    pairs = d // LANES
    hist_spec = pl.BlockSpec((None, None, past, LANES), lambda bi, p: (layer, bi, 0, p))
    return pl.pallas_call(
        functools.partial(_attn_sample_kernel, past=past, t=t),
        grid=(b, pairs),
        in_specs=[pl.BlockSpec((None, None, 2 * LANES, 2 * t), lambda bi, p: (bi, p, 0, 0)), hist_spec, hist_spec,
                  pl.BlockSpec((None, cp.shape[1], LANES), lambda bi, p: (bi, 0, 0)),
                  pl.BlockSpec((None, t, LANES), lambda bi, p: (bi, 0, p)),
                  pl.BlockSpec((None, None, LANES, t), lambda bi, p: (bi, p, 0, 0))],
        out_specs=pl.BlockSpec((None, t, LANES), lambda bi, p: (bi, 0, p)),
        out_shape=jax.ShapeDtypeStruct((b, t, d), BF16),
        compiler_params=_cparams(("parallel", "parallel")),
        name="attn_sample",
    )(wq, hist_k, hist_v, cp, kn, vnt)


def _sample_query_weights(qb, b, t):
    pairs = D_MODEL // LANES
    q = jnp.transpose(qb.reshape(b, t, pairs, 2, HEAD_DIM), (0, 2, 3, 4, 1))
    zero = jnp.zeros_like(q[:, :, 0])
    top = jnp.concatenate([jnp.concatenate([q[:, :, 0], zero], axis=-1),
                           jnp.concatenate([zero, q[:, :, 1]], axis=-1)], axis=-2)
    ones = jnp.broadcast_to(jnp.stack([_ones_rows(t, p) for p in range(pairs)]), (b, pairs, LANES, 2 * t))
    return jnp.concatenate([top, ones], axis=-2)


def _outproj_kernel(attn_ref, cact_ref, gate_ref, x_ref, wco_ref, wao_ref, wo_ref, g_ref, wpq_ref, keys_ref,
                    x1_ref, h2_ref, st_ref):
    conv_out = jnp.dot(cact_ref[...], wco_ref[...], preferred_element_type=F32)
    attn_out = jnp.dot(attn_ref[...], wao_ref[...], preferred_element_type=F32)
    merged = gate_ref[:, :D_MODEL] * conv_out + gate_ref[:, D_MODEL:] * attn_out
    x1 = x_ref[...] + jnp.dot(merged.astype(BF16), wo_ref[...], preferred_element_type=F32)
    x1_ref[...] = x1
    h2f = x1 * lax.rsqrt(jnp.mean(x1 * x1, axis=-1, keepdims=True) + EPS) * g_ref[...]
    h2 = h2f.astype(BF16)
    h2_ref[...] = h2f.T.astype(BF16)
    qp =jnp.dot(h2, wpq_ref[...], preferred_element_type=F32).astype(BF16)
    for hp in range(2 * PEER_HEADS):
        st_ref[hp] = lax.dot_general(keys_ref[hp], qp[:, hp * LANES:(hp + 1) * LANES], (((1,), (1,)), ((), ())),
                                     preferred_element_type=F32)


def _outproj(attn, cact, gate, x, w):
    n = x.shape[0]
    tm = min(256, n)
    tok = lambda c: pl.BlockSpec((tm, c), lambda i: (i, 0))
    nhp = 2 * PEER_HEADS
    return pl.pallas_call(
        _outproj_kernel,
        grid=(n // tm,),
        in_specs=[tok(D_MODEL), tok(D_CONV), tok(2 * D_MODEL), tok(D_MODEL),
                  _const_spec((D_CONV, D_MODEL)), _const_spec((D_MODEL, D_MODEL)), _const_spec((D_MODEL, D_MODEL)),
                  _const_spec((1, D_MODEL)), _const_spec((D_MODEL, nhp * LANES)),
                  _const_spec((nhp, PEER_KEYS, LANES))],
        out_specs=(tok(D_MODEL), pl.BlockSpec((D_MODEL, tm), lambda i: (0, i)),
                   pl.BlockSpec((nhp, PEER_KEYS, tm), lambda i: (0, 0, i))),
        out_shape=(jax.ShapeDtypeStruct((n, D_MODEL), F32), jax.ShapeDtypeStruct((D_MODEL, n), BF16),
                   jax.ShapeDtypeStruct((nhp, PEER_KEYS, n), F32)),
        compiler_params=_cparams(("parallel",)),
        name="outproj",
    )(attn, cact, gate, x, w["w_conv_out"], w["w_attn_out"], w["w_out"], w["norm_ffn"], w["peer_w_q"], w["peer_keys"])


def _top_ranks(s, break_ties):
    nrow = s.shape[0]
    iota = lax.broadcasted_iota(jnp.int32, s.shape, 0).astype(F32)
    rank = jnp.full(s.shape, float(PEER_TOPK), F32)
    vals = []
    for r in range(PEER_TOPK):
        m = jnp.max(s, axis=0, keepdims=True)
        sel = s == m
        if break_ties:
            sel = iota == jnp.min(jnp.where(sel, iota, float(nrow)), axis=0, keepdims=True)
        rank = jnp.where(sel, float(r), rank)
        s = jnp.where(sel, -jnp.inf, s)
        vals.append(m)
    count = jnp.sum(jnp.where(rank < float(PEER_TOPK), 1.0, 0.0), axis=0, keepdims=True)
    return rank, vals, count


def _staircase_counts(sv1, sv2):
    top = float(PEER_TOPK)
    iota = lax.broadcasted_iota(jnp.int32, sv1.shape, 0).astype(F32)
    n = jnp.zeros(sv1.shape, F32)
    head = sv1 + sv2[0:1, :]
    m0 = head[0:1, :]
    z = jnp.zeros(m0.shape, F32)
    for _ in range(PEER_TOPK):
        m = jnp.max(head, axis=0, keepdims=True)
        a_star = jnp.min(jnp.where(head == m, iota, top), axis=0, keepdims=True)
        onehot = iota == a_star
        z = z + jnp.exp(m - m0)
        n = n + jnp.where(onehot, 1.0, 0.0)
        n_star = jnp.sum(jnp.where(onehot, n, 0.0), axis=0, keepdims=True)
        nxt = jnp.sum(jnp.where(iota == n_star, sv2, 0.0), axis=0, keepdims=True)
        nxt = jnp.where(n_star >= top, -jnp.inf, nxt)
        a_val = jnp.sum(jnp.where(onehot, sv1, 0.0), axis=0, keepdims=True)
        head = jnp.where(onehot, a_val + nxt, head)
    return n, z


def _topk_head(st_ref, r2_ref, nn_ref, e2_ref, cc_ref, h, break_ties):
    top = float(PEER_TOPK)
    s1 = st_ref[2 * h]
    s2 = st_ref[2 * h + 1]
    rank1, v1, count1 = _top_ranks(s1, break_ties)
    rank2, v2, count2 = _top_ranks(s2, break_ties)
    n, z = _staircase_counts(jnp.concatenate(v1, axis=0), jnp.concatenate(v2, axis=0))
    nn = jnp.zeros(s1.shape, F32)
    for a in range(PEER_TOPK):
        nn = jnp.where(rank1 == float(a), n[a:a + 1, :], nn)
    r2_ref[h] = rank2.astype(BF16)
    nn_ref[h] = nn
    e2_ref[h] = jnp.where(rank2 < top, jnp.exp(s2 - v2[0]), 0.0).astype(BF16)
    cc_ref[h] = jnp.where(rank1 < top, jnp.exp(s1 - v1[0]), 0.0) / z
    return jnp.max(jnp.abs(count1 - top) + jnp.abs(count2 - top)) > 0.0


def _topk_kernel(st_ref, r2_ref, nn_ref, e2_ref, cc_ref):
    def body(i, carry):
        tied_a = _topk_head(st_ref, r2_ref, nn_ref, e2_ref, cc_ref, 2 * i, False)
        tied_b = _topk_head(st_ref, r2_ref, nn_ref, e2_ref, cc_ref, 2 * i + 1, False)

        @pl.when(jnp.logical_or(tied_a, tied_b))
        def _():
            _topk_head(st_ref, r2_ref, nn_ref, e2_ref, cc_ref, 2 * i, True)
            _topk_head(st_ref, r2_ref, nn_ref, e2_ref, cc_ref, 2 * i + 1, True)

        return carry

    lax.fori_loop(0, PEER_HEADS // 2, body, 0)


def _topk(st):
    nhp, nkeys, n = st.shape
    tl = 2 * LANES
    out_spec = pl.BlockSpec((PEER_HEADS, nkeys, tl), lambda i: (0, 0, i))
    sds = lambda dt: jax.ShapeDtypeStruct((PEER_HEADS, nkeys, n), dt)
    return pl.pallas_call(
        _topk_kernel,
        grid=(n // tl,),
        in_specs=[pl.BlockSpec((nhp, nkeys, tl), lambda i: (0, 0, i))],
        out_specs=(out_spec, out_spec, out_spec, out_spec),
        out_shape=(sds(BF16), sds(F32), sds(BF16), sds(F32)),
        compiler_params=_cparams(("parallel",)),
        name="peer_topk",
    )(st)


GELU_A = -2.0 * 0.7978845608028654 * LOG2E
GELU_B = GELU_A * 0.044715


def _gelu_tanh(x):
    return x / (1.0 + jnp.exp2(x * (GELU_A + GELU_B * (x * x))))


def _peer_kernel(h2t_ref, x1_ref, u_ref, vt_ref, r2_ref, e2_ref, nn_ref, cc_ref, o_ref, acc_ref, a_ref, gw_ref, w_ref,
                 *, ni, ne, steps):
    s = pl.program_id(0)

    @pl.when(s == 0)
    def _():
        a_ref[...] = jnp.zeros(a_ref.shape, F32)
        gw_ref[...] = jnp.zeros(gw_ref.shape, BF16)

    @pl.when(jnp.logical_or(s < 2, lax.rem(jnp.maximum(s - 2, 0), ne) == 0))
    def _():
        acc_ref[...] = jnp.zeros(acc_ref.shape, F32)

    slot_new = lax.rem(s, 2)
    slot_mid = 1 - slot_new
    tm = h2t_ref.shape[1]
    tiles_per_block = nn_ref.shape[1] // ni
    row0 = lax.rem(lax.rem(jnp.clip(s - 1, 0, steps - 1), ne), tiles_per_block) * ni

    def rows_bf16(ref, h, i):
        tile = jnp.broadcast_to(ref[h, pl.ds(row0 + i, 1), :], (16, tm)).astype(BF16)
        return jnp.concatenate([tile] * (PEER_KEYS // 16), axis=0)

    for i in range(ni):
        w = None
        for h in range(PEER_HEADS):
            keep = r2_ref[h] < rows_bf16(nn_ref, h, i)
            term = jnp.where(keep, e2_ref[h] * rows_bf16(cc_ref, h, i), jnp.zeros((), BF16))
            w = term if w is None else w + term
        w_ref[i * PEER_KEYS:(i + 1) * PEER_KEYS, :] = w
    a_ref[slot_new] = jnp.dot(u_ref[...], h2t_ref[...], preferred_element_type=F32)
    acc_ref[...] += jnp.dot(vt_ref[...], gw_ref[slot_new], preferred_element_type=F32)
    for i in range(ni):
        rows = slice(i * PEER_KEYS, (i + 1) * PEER_KEYS)
        gw_ref[slot_mid, rows, :] = _gelu_tanh(a_ref[slot_mid, rows, :]).astype(BF16) * w_ref[rows, :]

    @pl.when(jnp.logical_and(s >= 2, lax.rem(jnp.maximum(s - 2, 0), ne) == ne - 1))
    def _():
        o_ref[...] = x1_ref[...] + acc_ref[...].T


def _peer(h2t, x1, r2, nn, e2, cc, u_tab, vt_tab):
    d, n = h2t.shape
    tm = min(1024, n)
    te = 512
    ne = u_tab.shape[0] // te
    ni = te // PEER_KEYS
    steps = (n // tm) * ne
    pair = lambda s, lag: jnp.clip(s - lag, 0, steps - 1)
    tile = lambda s, lag: pair(s, lag) // ne
    expert = lambda s, lag: lax.rem(pair(s, lag), ne)
    row_block = max(ni, SUBLANE_TILE)
    key_spec = pl.BlockSpec((PEER_HEADS, PEER_KEYS, tm), lambda s: (0, 0, tile(s, 1)))
    row_spec = pl.BlockSpec((PEER_HEADS, row_block, tm), lambda s: (0, expert(s, 1) // (row_block // ni), tile(s, 1)))
    out_spec = pl.BlockSpec((tm, d), lambda s: (tile(s, 2), 0))
    return pl.pallas_call(
        functools.partial(_peer_kernel, ni=ni, ne=ne, steps=steps),
        grid=(steps + 2,),
        in_specs=[pl.BlockSpec((d, tm), lambda s: (0, tile(s, 0))), out_spec,
                  pl.BlockSpec((te, d), lambda s: (expert(s, 0), 0)),
                  pl.BlockSpec((d, te), lambda s: (0, expert(s, 2))), key_spec, key_spec, row_spec, row_spec],
        out_specs=out_spec,
        out_shape=jax.ShapeDtypeStruct((n, d), F32),
        scratch_shapes=[pltpu.VMEM((d, tm), F32), pltpu.VMEM((2, te, tm), F32), pltpu.VMEM((2, te, tm), BF16),
                        pltpu.VMEM((te, tm), BF16)],
        compiler_params=_cparams(("arbitrary",)),
        name="peer_experts",
    )(h2t, x1, u_tab, vt_tab, r2, e2, nn, cc)


def _prep_weights(l, norm_mix, w_in, conv_w, conv_b, conv_norm_g, conv_norm_b, w_conv_out, forget_bias, q_norm,
                  k_norm, w_attn_out, w_out, norm_ffn, peer_w_q, peer_keys, peer_u, peer_v):
    o1 = 2 * D_CONV
    o2 = o1 + 3 * D_MODEL
    o3 = o2 + N_HEADS
    wl = w_in[l]
    head_of_col = jnp.arange(D_MODEL) // HEAD_DIM
    seg = (head_of_col[:, None] == jnp.arange(LANES)[None, :]).astype(BF16)
    row = lambda a: a.reshape(1, -1).astype(F32)
    return {
        "norm_mix": row(norm_mix[l]),
        "w_glu": wl[:, :o1].astype(BF16),
        "w_qkv": wl[:, o1:o2].astype(BF16),
        "w_f": jnp.pad(wl[:, o2:o3], ((0, 0), (0, LANES - N_HEADS))).astype(BF16),
        "w_gate": wl[:, o3:].astype(BF16),
        "f_bias": jnp.pad(row(forget_bias[l]), ((0, 0), (0, LANES - N_HEADS))),
        "q_gain": row(jnp.tile(q_norm[l], N_HEADS) * (HEAD_DIM ** -0.5 * LOG2E)),
        "k_gain": row(jnp.tile(k_norm[l], N_HEADS)),
        "seg": seg,
        "segt": seg.T,
        "conv_w": jnp.pad(conv_w[l], ((0, HIST_ROWS - CONV_WIDTH), (0, 0))),
        "conv_b": row(conv_b[l]),
        "ln_g": row(conv_norm_g[l]),
        "ln_b": row(conv_norm_b[l]),
        "w_conv_out": w_conv_out[l].astype(BF16),
        "w_attn_out": w_attn_out[l].astype(BF16),
        "w_out": w_out[l].astype(BF16),
        "norm_ffn": row(norm_ffn[l]),
        "peer_w_q": peer_w_q[l].astype(BF16),
        "peer_keys": jnp.pad(peer_keys[l].reshape(2 * PEER_HEADS, PEER_KEYS, -1).astype(BF16),
                             ((0, 0), (0, 0), (0, LANES - peer_keys.shape[-1]))),
        "peer_u": peer_u[l].astype(BF16),
        "peer_vt": peer_v[l].T.astype(BF16),
    }


def _layer_group(x3, conv_state, cache, w):
    b, t, d = x3.shape
    n = b * t
    assert t >= CONV_STATE
    x = x3.reshape(n, d)
    u, qb, qt, k, v, kb, vb, vt, logf, gate = _inproj(x, w)
    u3 = u.reshape(b, t, D_CONV)
    if conv_state is None:
        buf = jnp.zeros((b, HIST_ROWS, D_CONV), F32)
    else:
        buf = jnp.pad(conv_state, ((0, 0), (HIST_ROWS - CONV_STATE, 0), (0, 0)))
    cact = _conv(u3, buf, w)

    logf3 = logf.reshape(b, t, N_HEADS)
    if cache is None:
        cp = _cumsum_pieces(logf3)
        attn = _attn_prompt(qt, kb.reshape(b, t, d), cp, vt, b, t)
    else:
        hist_k, hist_v, cache_logf, layer = cache
        past = hist_k.shape[2]
        total = past + t
        padded = -(-total // LANES) * LANES
        lf = jnp.concatenate([cache_logf[layer].astype(F32), logf3], axis=1)
        cp = _cumsum_pieces(jnp.pad(lf, ((0, 0), (0, padded - total), (0, 0))))
        vnt = jnp.transpose(vb.reshape(b, t, d // LANES, LANES), (0, 2, 3, 1))
        attn = _attn_sample(_sample_query_weights(qb, b, t), kb.reshape(b, t, d), vnt, hist_k, hist_v, layer, cp)

    x1, h2, st = _outproj(attn.reshape(n, d), cact.reshape(n, D_CONV), gate, x, w)
    r2, nn, e2, cc = _topk(st)
    y = _peer(h2, x1, r2, nn, e2, cc, w["peer_u"], w["peer_vt"])
    return (y.reshape(b, t, d), k.reshape(b, t, N_HEADS, HEAD_DIM), v.reshape(b, t, N_HEADS, HEAD_DIM),
            logf3, u3[:, t - CONV_STATE:, :])


def kernel(x_prompt, x_sample, cache_k, cache_v, cache_logf, state_conv, norm_mix, w_in, conv_w, conv_b, conv_norm_g, conv_norm_b, w_conv_out, forget_bias, q_norm, k_norm, w_attn_out, w_out, norm_ffn, peer_w_q, peer_keys, peer_u, peer_v):
    depth = w_in.shape[0]
    yp, ys = x_prompt, x_sample
    outs_p, outs_s = [], []
    flat = cache_k.shape[:3] + (D_MODEL,)
    hist_k = cache_k.reshape(flat)
    hist_v = cache_v.reshape(flat)
    for l in range(depth):
        w = _prep_weights(l, norm_mix, w_in, conv_w, conv_b, conv_norm_g, conv_norm_b, w_conv_out, forget_bias,
                          q_norm, k_norm, w_attn_out, w_out, norm_ffn, peer_w_q, peer_keys, peer_u, peer_v)
        yp, *rest_p = _layer_group(yp, None, None, w)
        ys, *rest_s = _layer_group(ys, state_conv[l], (hist_k, hist_v, cache_logf, l), w)
        outs_p.append(rest_p)
        outs_s.append(rest_s)
    stack = lambda outs, i: jnp.stack([o[i] for o in outs])
    return (yp, ys, stack(outs_p, 0), stack(outs_p, 1), stack(outs_p, 2), stack(outs_p, 3),
            stack(outs_s, 0), stack(outs_s, 1), stack(outs_s, 2), stack(outs_s, 3))
```

```python
import functools
import math

import jax
import jax.numpy as jnp
from jax import lax
from jax.experimental import pallas as pl
from jax.experimental.pallas import tpu as pltpu

F32 = jnp.float32
BF16 = jnp.bfloat16

D_MODEL = 1024
N_HEADS = 16
HEAD_DIM = 64
D_CONV = 512
CONV_WIDTH = 31
CONV_STATE = CONV_WIDTH - 1
PEER_HEADS = 8
PEER_KEYS = 128
PEER_TOPK = 16
EPS = 1e-6
LOG2E = math.log2(math.e)
LANES = 128
HIST_ROWS = 32
BIAS_PIECES = 3
VMEM_LIMIT = 56 * 1024 * 1024


def _cparams(sem):
    return pltpu.CompilerParams(dimension_semantics=sem, vmem_limit_bytes=VMEM_LIMIT)


def _const_spec(shape):
    n = len(shape)
    return pl.BlockSpec(shape, lambda *_: (0,) * n)


def _split3(x):
    hi = x.astype(BF16)
    r1 = x - hi.astype(F32)
    mid = r1.astype(BF16)
    lo = (r1 - mid.astype(F32)).astype(BF16)
    return hi, mid, lo


def _split_dot(x, w):
    return sum(jnp.dot(p, w, preferred_element_type=F32) for p in _split3(x)[:2])


def _log_sigmoid(x):
    return jnp.minimum(x, 0.0) - jnp.log(1.0 + jnp.exp(-jnp.abs(x)))


def _inproj_kernel(x_ref, g_ref, wglu_ref, wqkv_ref, wf_ref, wgate_ref, fb_ref, qg_ref, kg_ref, seg_ref, segt_ref,
                   u_ref, qb_ref, qt_ref, k_ref, v_ref, kb_ref, vb_ref, vt_ref, logf_ref, gate_ref):
    x = x_ref[...]
    h = x * lax.rsqrt(jnp.mean(x * x, axis=-1, keepdims=True) + EPS) * g_ref[...]
    hb = h.astype(BF16)
    glu = jnp.dot(hb, wglu_ref[...], preferred_element_type=F32)
    u_ref[...] = glu[:, :D_CONV] * jax.nn.sigmoid(glu[:, D_CONV:])
    qkv = jnp.dot(hb, wqkv_ref[...], preferred_element_type=F32)

    def head_norm(t, gain):
        ss = _split_dot(t * t, seg_ref[...])
        r = lax.rsqrt(ss * (1.0 / HEAD_DIM) + EPS)
        return t * _split_dot(r, segt_ref[...]) * gain

    q = head_norm(qkv[:, :D_MODEL], qg_ref[...])
    k = head_norm(qkv[:, D_MODEL:2 * D_MODEL], kg_ref[...])
    v = qkv[:, 2 * D_MODEL:]
    qb_ref[...] = q.astype(BF16)
    qt_ref[...] = q.T.astype(BF16)
    k_ref[...] = k.reshape(k_ref.shape)
    kb_ref[...] = k.astype(BF16)
    v_ref[...] = v.reshape(v_ref.shape)
    vb_ref[...] = v.astype(BF16)
    vt_ref[...] = v.T.astype(BF16)
    f = jnp.dot(hb, wf_ref[...], preferred_element_type=F32) + fb_ref[...]
    logf_ref[...] = _log_sigmoid(f)[:, :N_HEADS]
    gate_ref[...] = jax.nn.sigmoid(jnp.dot(hb, wgate_ref[...], preferred_element_type=F32))


def _inproj(x, w):
    n = x.shape[0]
    tm = min(256, n)
    tok = lambda c: pl.BlockSpec((tm, c), lambda i: (i, 0))
    tok_t = pl.BlockSpec((D_MODEL, tm), lambda i: (0, i))
    tok_heads = pl.BlockSpec((tm, N_HEADS, HEAD_DIM), lambda i: (i, 0, 0))
    sds = jax.ShapeDtypeStruct
    out_shape = (
        sds((n, D_CONV), F32),
        sds((n, D_MODEL), BF16),
        sds((D_MODEL, n), BF16),
        sds((n, N_HEADS, HEAD_DIM), F32),
        sds((n, N_HEADS, HEAD_DIM), F32),
        sds((n, D_MODEL), BF16),
        sds((n, D_MODEL), BF16),
        sds((D_MODEL, n), BF16),
        sds((n, N_HEADS), F32),
        sds((n, 2 * D_MODEL), F32),
    )
    return pl.pallas_call(
        _inproj_kernel,
        grid=(n // tm,),
        in_specs=[tok(D_MODEL), _const_spec((1, D_MODEL)), _const_spec((D_MODEL, 2 * D_CONV)),
                  _const_spec((D_MODEL, 3 * D_MODEL)), _const_spec((D_MODEL, LANES)),
                  _const_spec((D_MODEL, 2 * D_MODEL)), _const_spec((1, LANES)), _const_spec((1, D_MODEL)),
                  _const_spec((1, D_MODEL)), _const_spec((D_MODEL, LANES)), _const_spec((LANES, D_MODEL))],
        out_specs=(tok(D_CONV), tok(D_MODEL), tok_t, tok_heads, tok_heads, tok(D_MODEL), tok(D_MODEL), tok_t,
                   tok(N_HEADS), tok(2 * D_MODEL)),
        out_shape=out_shape,
        compiler_params=_cparams(("parallel",)),
        name="inproj",
    )(x, w["norm_mix"], w["w_glu"], w["w_qkv"], w["w_f"], w["w_gate"], w["f_bias"], w["q_gain"], w["k_gain"],
      w["seg"], w["segt"])


CONV_ROWS = 32
SUBLANE_TILE = 8


def _conv_kernel(u_ref, halo_ref, buf_ref, w_ref, b_ref, lng_ref, lnb_ref, o_ref, pad_ref, *, tt):
    first = pl.program_id(1) == 0
    pad_ref[0, 0:HIST_ROWS, :] = jnp.where(first, buf_ref[...], halo_ref[...])
    pad_ref[0, HIST_ROWS:HIST_ROWS + tt, :] = u_ref[...]
    span = HIST_ROWS + tt - SUBLANE_TILE
    for s in range(1, SUBLANE_TILE):
        pad_ref[s, 0:span, :] = pad_ref[0, s:s + span, :]
    off = HIST_ROWS - CONV_STATE
    for r in range(tt // CONV_ROWS):
        acc = jnp.broadcast_to(b_ref[...], (CONV_ROWS, D_CONV))
        for k in range(CONV_WIDTH):
            shift = (off + k) % SUBLANE_TILE
            start = r * CONV_ROWS + (off + k) - shift
            acc = acc + w_ref[k:k + 1, :] * pad_ref[shift, start:start + CONV_ROWS, :]
        mu = jnp.mean(acc, axis=-1, keepdims=True)
        d = acc - mu
        var = jnp.mean(d * d, axis=-1, keepdims=True)
        y = d * lax.rsqrt(var + EPS) * lng_ref[...] + lnb_ref[...]
        o_ref[r * CONV_ROWS:(r + 1) * CONV_ROWS, :] = (y * jax.nn.sigmoid(y)).astype(BF16)


def _conv(u, buf, w):
    b, t, c = u.shape
    tt = min(256, t)
    hb = tt // HIST_ROWS
    return pl.pallas_call(
        functools.partial(_conv_kernel, tt=tt),
        grid=(b, t // tt),
        in_specs=[pl.BlockSpec((None, tt, c), lambda bi, i: (bi, i, 0)),
                  pl.BlockSpec((None, HIST_ROWS, c), lambda bi, i: (bi, jnp.maximum(i * hb - 1, 0), 0)),
                  pl.BlockSpec((None, HIST_ROWS, c), lambda bi, i: (bi, 0, 0)),
                  _const_spec((HIST_ROWS, c)), _const_spec((1, c)), _const_spec((1, c)), _const_spec((1, c))],
        out_specs=pl.BlockSpec((None, tt, c), lambda bi, i: (bi, i, 0)),
        out_shape=jax.ShapeDtypeStruct((b, t, c), BF16),
        scratch_shapes=[pltpu.VMEM((SUBLANE_TILE, HIST_ROWS + tt, c), F32)],
        compiler_params=_cparams(("parallel", "arbitrary")),
        name="conv",
    )(u, u, buf, w["conv_w"], w["conv_b"], w["ln_g"], w["ln_b"])


def _cumsum_pieces_kernel(x_ref, tri_ref, place_ref, o_ref, *, nblk):
    group = x_ref.shape[0]

    def body(j, carries):
        sl = pl.ds(pl.multiple_of(j * LANES, LANES), LANES)
        new = []
        for g in range(group):
            local = sum(jnp.dot(tri_ref[...], piece, preferred_element_type=F32)
                        for piece in _split3(x_ref[g, sl, :]))
            cs = local + carries[g]
            out = None
            for j2, piece in enumerate(_split3(cs * (-LOG2E))):
                term = jnp.dot(piece, place_ref[j2], preferred_element_type=F32)
                out = term if out is None else out + term
            o_ref[g, sl, :] = out.astype(BF16)
            new.append(cs[LANES - 1:LANES, :])
        return tuple(new)

    unroll = next(u for u in (4, 3, 2, 1) if nblk % u == 0)
    zero = jnp.zeros((1, x_ref.shape[2]), F32)
    lax.fori_loop(0, nblk, body, (zero,) * group, unroll=unroll)


def _cumsum_pieces(logf):
    b, t, h = logf.shape
    ii = lax.broadcasted_iota(jnp.int32, (LANES, LANES), 0)
    jj = lax.broadcasted_iota(jnp.int32, (LANES, LANES), 1)
    tri = (jj <= ii).astype(BF16)
    head = jnp.arange(h)
    col = jnp.arange(LANES)
    place = jnp.stack([(col[None, :] == (head * BIAS_PIECES + j)[:, None]).astype(BF16)
                       for j in range(BIAS_PIECES)])
    group = 2 if b % 2 == 0 else 1
    return pl.pallas_call(
        functools.partial(_cumsum_pieces_kernel, nblk=t // LANES),
        grid=(b // group,),
        in_specs=[pl.BlockSpec((group, t, h), lambda i: (i, 0, 0)), _const_spec((LANES, LANES)),
                  _const_spec((BIAS_PIECES, h, LANES))],
        out_specs=pl.BlockSpec((group, t, LANES), lambda i: (i, 0, 0)),
        out_shape=jax.ShapeDtypeStruct((b, t, LANES), BF16),
        compiler_params=_cparams(("parallel",)),
        name="cumsum_pieces",
    )(logf, tri, place)


COL_TILE = 256

def _ones_rows(tq, pair):
    r = lax.broadcasted_iota(jnp.int32, (LANES, 2 * tq), 0) - 2 * BIAS_PIECES * pair
    c = lax.broadcasted_iota(jnp.int32, (LANES, 2 * tq), 1)
    first = jnp.where(c < tq, 0, BIAS_PIECES)
    return jnp.where(r >= first, jnp.where(r < first + BIAS_PIECES, 1.0, 0.0), 0.0).astype(BF16)


def _attn_init(m_ref, l_ref, acc_ref):
    m_ref[...] = jnp.full(m_ref.shape, -jnp.inf, F32)
    l_ref[...] = jnp.zeros(l_ref.shape, F32)
    acc_ref[...] = jnp.zeros(acc_ref.shape, F32)


def _attn_chunk(kaug, vt_c, wq_ref, m_ref, l_ref, acc_ref, tq, diagonal):
    ncol = wq_ref.shape[1]
    cw = min(COL_TILE, ncol)
    tiles = [slice(ct * cw, (ct + 1) * cw) for ct in range(ncol // cw)]
    logits = [jnp.dot(kaug, wq_ref[:, cs], preferred_element_type=F32) for cs in tiles]
    for ct, cs in enumerate(tiles):
        s = logits[ct]
        if diagonal:
            key = lax.broadcasted_iota(jnp.int32, s.shape, 0)
            qry = lax.broadcasted_iota(jnp.int32, s.shape, 1) + ct * cw
            qry = jnp.where(qry >= tq, qry - tq, qry)
            s = jnp.where(key <= qry, s, -jnp.inf)
        m_old = m_ref[:, cs]
        m_new = jnp.maximum(m_old, jnp.max(s, axis=0, keepdims=True))
        alpha = jnp.exp2(m_old - m_new)
        p = jnp.exp2(s - m_new)
        l_ref[:, cs] = alpha * l_ref[:, cs] + jnp.sum(p, axis=0, keepdims=True)
        acc_ref[:, cs] = alpha * acc_ref[:, cs] + jnp.dot(vt_c, p.astype(BF16), preferred_element_type=F32)
        m_ref[:, cs] = m_new


def _attn_finish(o_ref, l_ref, acc_ref, tq):
    ot = (acc_ref[...] / l_ref[...]).T
    lane = lax.broadcasted_iota(jnp.int32, (tq, LANES), 1)
    o_ref[...] = jnp.where(lane < HEAD_DIM, ot[:tq], ot[tq:]).astype(o_ref.dtype)


def _attn_prompt_kernel(qt_ref, k_ref, cp_ref, vt_ref, o_ref, wq_ref, m_ref, l_ref, acc_ref, *, tq):
    qi = pl.program_id(2)
    qt = qt_ref[...]
    row = lax.broadcasted_iota(jnp.int32, qt.shape, 0)
    zero = jnp.zeros_like(qt)
    wq_ref[0:LANES, :] = jnp.concatenate([jnp.where(row < HEAD_DIM, qt, zero), jnp.where(row >= HEAD_DIM, qt, zero)],
                                         axis=1)
    wq_ref[LANES:2 * LANES, :] = _ones_rows(tq, pl.program_id(1))
    _attn_init(m_ref, l_ref, acc_ref)

    def chunk(kc, diagonal):
        ksl = pl.ds(pl.multiple_of(kc * tq, tq), tq)
        kaug = jnp.concatenate([k_ref[ksl, :], cp_ref[ksl, :]], axis=1)
        _attn_chunk(kaug, vt_ref[:, ksl], wq_ref, m_ref, l_ref, acc_ref, tq, diagonal)

    def body(kc, carry):
        chunk(kc, False)
        return carry

    lax.fori_loop(0, qi, body, 0)
    chunk(qi, True)
    _attn_finish(o_ref, l_ref, acc_ref, tq)


def _attn_prompt(qt, kb, cp, vt, b, t):
    d = qt.shape[0]
    tq = min(512, t)
    nq = t // tq
    pairs = d // LANES
    return pl.pallas_call(
        functools.partial(_attn_prompt_kernel, tq=tq),
        grid=(b, pairs, nq),
        in_specs=[pl.BlockSpec((LANES, tq), lambda bi, p, qi: (p, bi * nq + qi)),
                  pl.BlockSpec((None, t, LANES), lambda bi, p, qi: (bi, 0, p)),
                  pl.BlockSpec((None, t, LANES), lambda bi, p, qi: (bi, 0, 0)),
                  pl.BlockSpec((LANES, t), lambda bi, p, qi: (p, bi))],
        out_specs=pl.BlockSpec((None, tq, LANES), lambda bi, p, qi: (bi, qi, p)),
        out_shape=jax.ShapeDtypeStruct((b, t, d), BF16),
        scratch_shapes=[pltpu.VMEM((2 * LANES, 2 * tq), BF16), pltpu.VMEM((1, 2 * tq), F32),
                        pltpu.VMEM((1, 2 * tq), F32), pltpu.VMEM((LANES, 2 * tq), F32)],
        compiler_params=_cparams(("parallel", "parallel", "arbitrary")),
        name="attn_prompt",
    )(qt, kb, cp, vt)


def _attn_sample_kernel(wq_ref, kh_ref, vh_ref, cp_ref, kn_ref, vnt_ref, o_ref, *, past, t):
    wq = wq_ref[...]
    s_hist = jnp.dot(jnp.concatenate([kh_ref[...].astype(BF16), cp_ref[0:past, :]], axis=1), wq, preferred_element_type=F32)
    s_new = jnp.dot(jnp.concatenate([kn_ref[...], cp_ref[past:past + t, :]], axis=1), wq, preferred_element_type=F32)
    key = lax.broadcasted_iota(jnp.int32, s_new.shape, 0)
    qry = lax.broadcasted_iota(jnp.int32, s_new.shape, 1)
    s_new = jnp.where(key <= jnp.where(qry >= t, qry - t, qry), s_new, -jnp.inf)
    m = jnp.maximum(jnp.max(s_hist, axis=0, keepdims=True), jnp.max(s_new, axis=0, keepdims=True))
    p_hist = jnp.exp2(s_hist - m)
    p_new = jnp.exp2(s_new - m)
    denom = jnp.sum(p_hist, axis=0, keepdims=True) + jnp.sum(p_new, axis=0, keepdims=True)
    acc = (jnp.dot(vh_ref[...].astype(BF16).T, p_hist.astype(BF16), preferred_element_type=F32)
           + jnp.dot(vnt_ref[...], p_new.astype(BF16), preferred_element_type=F32))
    ot = (acc / denom).T
    lane = lax.broadcasted_iota(jnp.int32, (t, LANES), 1)
    o_ref[...] = jnp.where(lane < HEAD_DIM, ot[:t], ot[t:]).astype(o_ref.dtype)


def _attn_sample(wq, kn, vnt, hist_k, hist_v, layer, cp):
    b, t, d = kn.shape
    past = hist_k.shape[2]
    pairs = d // LANES
    hist_spec = pl.BlockSpec((None, None, past, LANES), lambda bi, p: (layer, bi, 0, p))
    return pl.pallas_call(
        functools.partial(_attn_sample_kernel, past=past, t=t),
        grid=(b, pairs),
        in_specs=[pl.BlockSpec((None, None, 2 * LANES, 2 * t), lambda bi, p: (bi, p, 0, 0)), hist_spec, hist_spec,
                  pl.BlockSpec((None, cp.shape[1], LANES), lambda bi, p: (bi, 0, 0)),
                  pl.BlockSpec((None, t, LANES), lambda bi, p: (bi, 0, p)),
                  pl.BlockSpec((None, None, LANES, t), lambda bi, p: (bi, p, 0, 0))],
        out_specs=pl.BlockSpec((None, t, LANES), lambda bi, p: (bi, 0, p)),
        out_shape=jax.ShapeDtypeStruct((b, t, d), BF16),
        compiler_params=_cparams(("parallel", "parallel")),
        name="attn_sample",
    )(wq, hist_k, hist_v, cp, kn, vnt)


def _sample_query_weights(qb, b, t):
    pairs = D_MODEL // LANES
    q = jnp.transpose(qb.reshape(b, t, pairs, 2, HEAD_DIM), (0, 2, 3, 4, 1))
    zero = jnp.zeros_like(q[:, :, 0])
    top = jnp.concatenate([jnp.concatenate([q[:, :, 0], zero], axis=-1),
                           jnp.concatenate([zero, q[:, :, 1]], axis=-1)], axis=-2)
    ones = jnp.broadcast_to(jnp.stack([_ones_rows(t, p) for p in range(pairs)]), (b, pairs, LANES, 2 * t))
    return jnp.concatenate([top, ones], axis=-2)


def _outproj_kernel(attn_ref, cact_ref, gate_ref, x_ref, wco_ref, wao_ref, wo_ref, g_ref, wpq_ref, keys_ref,
                    x1_ref, h2_ref, st_ref):
    conv_out = jnp.dot(cact_ref[...], wco_ref[...], preferred_element_type=F32)
    attn_out = jnp.dot(attn_ref[...], wao_ref[...], preferred_element_type=F32)
    merged = gate_ref[:, :D_MODEL] * conv_out + gate_ref[:, D_MODEL:] * attn_out
    x1 = x_ref[...] + jnp.dot(merged.astype(BF16), wo_ref[...], preferred_element_type=F32)
    x1_ref[...] = x1
    h2f = x1 * lax.rsqrt(jnp.mean(x1 * x1, axis=-1, keepdims=True) + EPS) * g_ref[...]
    h2 = h2f.astype(BF16)
    h2_ref[...] = h2f.T.astype(BF16)
    qp =jnp.dot(h2, wpq_ref[...], preferred_element_type=F32).astype(BF16)
    for hp in range(2 * PEER_HEADS):
        st_ref[hp] = lax.dot_general(keys_ref[hp], qp[:, hp * LANES:(hp + 1) * LANES], (((1,), (1,)), ((), ())),
                                     preferred_element_type=F32)


def _outproj(attn, cact, gate, x, w):
    n = x.shape[0]
    tm = min(256, n)
    tok = lambda c: pl.BlockSpec((tm, c), lambda i: (i, 0))
    nhp = 2 * PEER_HEADS
    return pl.pallas_call(
        _outproj_kernel,
        grid=(n // tm,),
        in_specs=[tok(D_MODEL), tok(D_CONV), tok(2 * D_MODEL), tok(D_MODEL),
                  _const_spec((D_CONV, D_MODEL)), _const_spec((D_MODEL, D_MODEL)), _const_spec((D_MODEL, D_MODEL)),
                  _const_spec((1, D_MODEL)), _const_spec((D_MODEL, nhp * LANES)),
                  _const_spec((nhp, PEER_KEYS, LANES))],
        out_specs=(tok(D_MODEL), pl.BlockSpec((D_MODEL, tm), lambda i: (0, i)),
                   pl.BlockSpec((nhp, PEER_KEYS, tm), lambda i: (0, 0, i))),
        out_shape=(jax.ShapeDtypeStruct((n, D_MODEL), F32), jax.ShapeDtypeStruct((D_MODEL, n), BF16),
                   jax.ShapeDtypeStruct((nhp, PEER_KEYS, n), F32)),
        compiler_params=_cparams(("parallel",)),
        name="outproj",
    )(attn, cact, gate, x, w["w_conv_out"], w["w_attn_out"], w["w_out"], w["norm_ffn"], w["peer_w_q"], w["peer_keys"])


def _top_ranks(s, break_ties):
    nrow = s.shape[0]
    iota = lax.broadcasted_iota(jnp.int32, s.shape, 0).astype(F32)
    rank = jnp.full(s.shape, float(PEER_TOPK), F32)
    vals = []
    for r in range(PEER_TOPK):
        m = jnp.max(s, axis=0, keepdims=True)
        sel = s == m
        if break_ties:
            sel = iota == jnp.min(jnp.where(sel, iota, float(nrow)), axis=0, keepdims=True)
        rank = jnp.where(sel, float(r), rank)
        s = jnp.where(sel, -jnp.inf, s)
        vals.append(m)
    count = jnp.sum(jnp.where(rank < float(PEER_TOPK), 1.0, 0.0), axis=0, keepdims=True)
    return rank, vals, count


def _staircase_counts(sv1, sv2):
    top = float(PEER_TOPK)
    iota = lax.broadcasted_iota(jnp.int32, sv1.shape, 0).astype(F32)
    n = jnp.zeros(sv1.shape, F32)
    head = sv1 + sv2[0:1, :]
    m0 = head[0:1, :]
    z = jnp.zeros(m0.shape, F32)
    for _ in range(PEER_TOPK):
        m = jnp.max(head, axis=0, keepdims=True)
        a_star = jnp.min(jnp.where(head == m, iota, top), axis=0, keepdims=True)
        onehot = iota == a_star
        z = z + jnp.exp(m - m0)
        n = n + jnp.where(onehot, 1.0, 0.0)
        n_star = jnp.sum(jnp.where(onehot, n, 0.0), axis=0, keepdims=True)
        nxt = jnp.sum(jnp.where(iota == n_star, sv2, 0.0), axis=0, keepdims=True)
        nxt = jnp.where(n_star >= top, -jnp.inf, nxt)
        a_val = jnp.sum(jnp.where(onehot, sv1, 0.0), axis=0, keepdims=True)
        head = jnp.where(onehot, a_val + nxt, head)
    return n, z


def _topk_head(st_ref, r2_ref, nn_ref, e2_ref, cc_ref, h, break_ties):
    top = float(PEER_TOPK)
    s1 = st_ref[2 * h]
    s2 = st_ref[2 * h + 1]
    rank1, v1, count1 = _top_ranks(s1, break_ties)
    rank2, v2, count2 = _top_ranks(s2, break_ties)
    n, z = _staircase_counts(jnp.concatenate(v1, axis=0), jnp.concatenate(v2, axis=0))
    nn = jnp.zeros(s1.shape, F32)
    for a in range(PEER_TOPK):
        nn = jnp.where(rank1 == float(a), n[a:a + 1, :], nn)
    r2_ref[h] = rank2.astype(BF16)
    nn_ref[h] = nn
    e2_ref[h] = jnp.where(rank2 < top, jnp.exp(s2 - v2[0]), 0.0).astype(BF16)
    cc_ref[h] = jnp.where(rank1 < top, jnp.exp(s1 - v1[0]), 0.0) / z
    return jnp.max(jnp.abs(count1 - top) + jnp.abs(count2 - top)) > 0.0


def _topk_kernel(st_ref, r2_ref, nn_ref, e2_ref, cc_ref):
    def body(i, carry):
        tied_a = _topk_head(st_ref, r2_ref, nn_ref, e2_ref, cc_ref, 2 * i, False)
        tied_b = _topk_head(st_ref, r2_ref, nn_ref, e2_ref, cc_ref, 2 * i + 1, False)

        @pl.when(jnp.logical_or(tied_a, tied_b))
        def _():
            _topk_head(st_ref, r2_ref, nn_ref, e2_ref, cc_ref, 2 * i, True)
            _topk_head(st_ref, r2_ref, nn_ref, e2_ref, cc_ref, 2 * i + 1, True)

        return carry

    lax.fori_loop(0, PEER_HEADS // 2, body, 0)


def _topk(st):
    nhp, nkeys, n = st.shape
    tl = min(2 * LANES, n)
    out_spec = pl.BlockSpec((PEER_HEADS, nkeys, tl), lambda i: (0, 0, i))
    sds = lambda dt: jax.ShapeDtypeStruct((PEER_HEADS, nkeys, n), dt)
    return pl.pallas_call(
        _topk_kernel,
        grid=(n // tl,),
        in_specs=[pl.BlockSpec((nhp, nkeys, tl), lambda i: (0, 0, i))],
        out_specs=(out_spec, out_spec, out_spec, out_spec),
        out_shape=(sds(BF16), sds(F32), sds(BF16), sds(F32)),
        compiler_params=_cparams(("parallel",)),
        name="peer_topk",
    )(st)


GELU_A = -2.0 * 0.7978845608028654 * LOG2E
GELU_B = GELU_A * 0.044715


def _gelu_tanh(x):
    return x / (1.0 + jnp.exp2(x * (GELU_A + GELU_B * (x * x))))


def _peer_kernel(h2t_ref, x1_ref, u_ref, vt_ref, r2_ref, e2_ref, nn_ref, cc_ref, o_ref, acc_ref, a_ref, gw_ref, w_ref,
                 *, ni, ne, steps):
    s = pl.program_id(0)

    @pl.when(s == 0)
    def _():
        a_ref[...] = jnp.zeros(a_ref.shape, F32)
        gw_ref[...] = jnp.zeros(gw_ref.shape, BF16)

    @pl.when(jnp.logical_or(s < 2, lax.rem(jnp.maximum(s - 2, 0), ne) == 0))
    def _():
        acc_ref[...] = jnp.zeros(acc_ref.shape, F32)

    slot_new = lax.rem(s, 2)
    slot_mid = 1 - slot_new
    tm = h2t_ref.shape[1]
    tiles_per_block = nn_ref.shape[1] // ni
    row0 = lax.rem(lax.rem(jnp.clip(s - 1, 0, steps - 1), ne), tiles_per_block) * ni

    def rows_bf16(ref, h, i):
        tile = jnp.broadcast_to(ref[h, pl.ds(row0 + i, 1), :], (16, tm)).astype(BF16)
        return jnp.concatenate([tile] * (PEER_KEYS // 16), axis=0)

    for i in range(ni):
        w = None
        for h in range(PEER_HEADS):
            keep = r2_ref[h] < rows_bf16(nn_ref, h, i)
            term = jnp.where(keep, e2_ref[h] * rows_bf16(cc_ref, h, i), jnp.zeros((), BF16))
            w = term if w is None else w + term
        w_ref[i * PEER_KEYS:(i + 1) * PEER_KEYS, :] = w
    a_ref[slot_new] = jnp.dot(u_ref[...], h2t_ref[...], preferred_element_type=F32)
    acc_ref[...] += jnp.dot(vt_ref[...], gw_ref[slot_new], preferred_element_type=F32)
    for i in range(ni):
        rows = slice(i * PEER_KEYS, (i + 1) * PEER_KEYS)
        gw_ref[slot_mid, rows, :] = _gelu_tanh(a_ref[slot_mid, rows, :]).astype(BF16) * w_ref[rows, :]

    @pl.when(jnp.logical_and(s >= 2, lax.rem(jnp.maximum(s - 2, 0), ne) == ne - 1))
    def _():
        o_ref[...] = x1_ref[...] + acc_ref[...].T


def _peer(h2t, x1, r2, nn, e2, cc, u_tab, vt_tab):
    d, n = h2t.shape
    tm = min(1024, n)
    te = 512
    ne = u_tab.shape[0] // te
    ni = te // PEER_KEYS
    steps = (n // tm) * ne
    pair = lambda s, lag: jnp.clip(s - lag, 0, steps - 1)
    tile = lambda s, lag: pair(s, lag) // ne
    expert = lambda s, lag: lax.rem(pair(s, lag), ne)
    row_block = max(ni, SUBLANE_TILE)
    key_spec = pl.BlockSpec((PEER_HEADS, PEER_KEYS, tm), lambda s: (0, 0, tile(s, 1)))
    row_spec = pl.BlockSpec((PEER_HEADS, row_block, tm), lambda s: (0, expert(s, 1) // (row_block // ni), tile(s, 1)))
    out_spec = pl.BlockSpec((tm, d), lambda s: (tile(s, 2), 0))
    return pl.pallas_call(
        functools.partial(_peer_kernel, ni=ni, ne=ne, steps=steps),
        grid=(steps + 2,),
        in_specs=[pl.BlockSpec((d, tm), lambda s: (0, tile(s, 0))), out_spec,
                  pl.BlockSpec((te, d), lambda s: (expert(s, 0), 0)),
                  pl.BlockSpec((d, te), lambda s: (0, expert(s, 2))), key_spec, key_spec, row_spec, row_spec],
        out_specs=out_spec,
        out_shape=jax.ShapeDtypeStruct((n, d), F32),
        scratch_shapes=[pltpu.VMEM((d, tm), F32), pltpu.VMEM((2, te, tm), F32), pltpu.VMEM((2, te, tm), BF16),
                        pltpu.VMEM((te, tm), BF16)],
        compiler_params=_cparams(("arbitrary",)),
        name="peer_experts",
    )(h2t, x1, u_tab, vt_tab, r2, e2, nn, cc)


def _prep_weights(l, norm_mix, w_in, conv_w, conv_b, conv_norm_g, conv_norm_b, w_conv_out, forget_bias, q_norm,
                  k_norm, w_attn_out, w_out, norm_ffn, peer_w_q, peer_keys, peer_u, peer_v):
    o1 = 2 * D_CONV
    o2 = o1 + 3 * D_MODEL
    o3 = o2 + N_HEADS
    wl = w_in[l]
    head_of_col = jnp.arange(D_MODEL) // HEAD_DIM
    seg = (head_of_col[:, None] == jnp.arange(LANES)[None, :]).astype(BF16)
    row = lambda a: a.reshape(1, -1).astype(F32)
    return {
        "norm_mix": row(norm_mix[l]),
        "w_glu": wl[:, :o1].astype(BF16),
        "w_qkv": wl[:, o1:o2].astype(BF16),
        "w_f": jnp.pad(wl[:, o2:o3], ((0, 0), (0, LANES - N_HEADS))).astype(BF16),
        "w_gate": wl[:, o3:].astype(BF16),
        "f_bias": jnp.pad(row(forget_bias[l]), ((0, 0), (0, LANES - N_HEADS))),
        "q_gain": row(jnp.tile(q_norm[l], N_HEADS) * (HEAD_DIM ** -0.5 * LOG2E)),
        "k_gain": row(jnp.tile(k_norm[l], N_HEADS)),
        "seg": seg,
        "segt": seg.T,
        "conv_w": jnp.pad(conv_w[l], ((0, HIST_ROWS - CONV_WIDTH), (0, 0))),
        "conv_b": row(conv_b[l]),
        "ln_g": row(conv_norm_g[l]),
        "ln_b": row(conv_norm_b[l]),
        "w_conv_out": w_conv_out[l].astype(BF16),
        "w_attn_out": w_attn_out[l].astype(BF16),
        "w_out": w_out[l].astype(BF16),
        "norm_ffn": row(norm_ffn[l]),
        "peer_w_q": peer_w_q[l].astype(BF16),
        "peer_keys": jnp.pad(peer_keys[l].reshape(2 * PEER_HEADS, PEER_KEYS, -1).astype(BF16),
                             ((0, 0), (0, 0), (0, LANES - peer_keys.shape[-1]))),
        "peer_u": peer_u[l].astype(BF16),
        "peer_vt": peer_v[l].T.astype(BF16),
    }


def _layer_group(x3, conv_state, cache, w):
    b, t, d = x3.shape
    n = b * t
    assert t >= CONV_STATE
    x = x3.reshape(n, d)
    u, qb, qt, k, v, kb, vb, vt, logf, gate = _inproj(x, w)
    u3 = u.reshape(b, t, D_CONV)
    if conv_state is None:
        buf = jnp.zeros((b, HIST_ROWS, D_CONV), F32)
    else:
        buf = jnp.pad(conv_state, ((0, 0), (HIST_ROWS - CONV_STATE, 0), (0, 0)))
    cact = _conv(u3, buf, w)

    logf3 = logf.reshape(b, t, N_HEADS)
    if cache is None:
        cp = _cumsum_pieces(logf3)
        attn = _attn_prompt(qt, kb.reshape(b, t, d), cp, vt, b, t)
    else:
        hist_k, hist_v, cache_logf, layer = cache
        past = hist_k.shape[2]
        total = past + t
        padded = -(-total // LANES) * LANES
        lf = jnp.concatenate([cache_logf[layer].astype(F32), logf3], axis=1)
        cp = _cumsum_pieces(jnp.pad(lf, ((0, 0), (0, padded - total), (0, 0))))
        vnt = jnp.transpose(vb.reshape(b, t, d // LANES, LANES), (0, 2, 3, 1))
        attn = _attn_sample(_sample_query_weights(qb, b, t), kb.reshape(b, t, d), vnt, hist_k, hist_v, layer, cp)

    x1, h2, st = _outproj(attn.reshape(n, d), cact.reshape(n, D_CONV), gate, x, w)
    r2, nn, e2, cc = _topk(st)
    y = _peer(h2, x1, r2, nn, e2, cc, w["peer_u"], w["peer_vt"])
    return (y.reshape(b, t, d), k.reshape(b, t, N_HEADS, HEAD_DIM), v.reshape(b, t, N_HEADS, HEAD_DIM),
            logf3, u3[:, t - CONV_STATE:, :])


def kernel(x_prompt, x_sample, cache_k, cache_v, cache_logf, state_conv, norm_mix, w_in, conv_w, conv_b, conv_norm_g, conv_norm_b, w_conv_out, forget_bias, q_norm, k_norm, w_attn_out, w_out, norm_ffn, peer_w_q, peer_keys, peer_u, peer_v):
    depth = w_in.shape[0]
    yp, ys = x_prompt, x_sample
    outs_p, outs_s = [], []
    flat = cache_k.shape[:3] + (D_MODEL,)
    hist_k = cache_k.reshape(flat)
    hist_v = cache_v.reshape(flat)
    for l in range(depth):
        w = _prep_weights(l, norm_mix, w_in, conv_w, conv_b, conv_norm_g, conv_norm_b, w_conv_out, forget_bias,
                          q_norm, k_norm, w_attn_out, w_out, norm_ffn, peer_w_q, peer_keys, peer_u, peer_v)
        yp, *rest_p = _layer_group(yp, None, None, w)
        ys, *rest_s = _layer_group(ys, state_conv[l], (hist_k, hist_v, cache_logf, l), w)
        outs_p.append(rest_p)
        outs_s.append(rest_s)
    stack = lambda outs, i: jnp.stack([o[i] for o in outs])
    return (yp, ys, stack(outs_p, 0), stack(outs_p, 1), stack(outs_p, 2), stack(outs_p, 3),
            stack(outs_s, 0), stack(outs_s, 1), stack(outs_s, 2), stack(outs_s, 3))
```

```python
import functools
import math

import jax
import jax.numpy as jnp
from jax import lax
from jax.experimental import pallas as pl
from jax.experimental.pallas import tpu as pltpu

F32 = jnp.float32
BF16 = jnp.bfloat16

D_MODEL = 1024
N_HEADS = 16
HEAD_DIM = 64
D_CONV = 512
CONV_WIDTH = 31
CONV_STATE = CONV_WIDTH - 1
PEER_HEADS = 8
PEER_KEYS = 128
PEER_TOPK = 16
EPS = 1e-6
LOG2E = math.log2(math.e)
LANES = 128
HIST_ROWS = 32
BIAS_PIECES = 3
VMEM_LIMIT = 56 * 1024 * 1024


def _cparams(sem):
    return pltpu.CompilerParams(dimension_semantics=sem, vmem_limit_bytes=VMEM_LIMIT)


def _const_spec(shape):
    n = len(shape)
    return pl.BlockSpec(shape, lambda *_: (0,) * n)


def _split3(x):
    hi = x.astype(BF16)
    r1 = x - hi.astype(F32)
    mid = r1.astype(BF16)
    lo = (r1 - mid.astype(F32)).astype(BF16)
    return hi, mid, lo


def _split_dot(x, w):
    return sum(jnp.dot(p, w, preferred_element_type=F32) for p in _split3(x)[:2])


def _log_sigmoid(x):
    return jnp.minimum(x, 0.0) - jnp.log(1.0 + jnp.exp(-jnp.abs(x)))


def _inproj_kernel(x_ref, g_ref, wglu_ref, wqkv_ref, wf_ref, wgate_ref, fb_ref, qg_ref, kg_ref, seg_ref, segt_ref,
                   u_ref, qb_ref, qt_ref, k_ref, v_ref, kb_ref, vb_ref, vt_ref, logf_ref, gate_ref):
    x = x_ref[...]
    h = x * lax.rsqrt(jnp.mean(x * x, axis=-1, keepdims=True) + EPS) * g_ref[...]
    hb = h.astype(BF16)
    glu = jnp.dot(hb, wglu_ref[...], preferred_element_type=F32)
    u_ref[...] = glu[:, :D_CONV] * jax.nn.sigmoid(glu[:, D_CONV:])
    qkv = jnp.dot(hb, wqkv_ref[...], preferred_element_type=F32)

    def head_norm(t, gain):
        ss = _split_dot(t * t, seg_ref[...])
        r = lax.rsqrt(ss * (1.0 / HEAD_DIM) + EPS)
        return t * _split_dot(r, segt_ref[...]) * gain

    q = head_norm(qkv[:, :D_MODEL], qg_ref[...])
    k = head_norm(qkv[:, D_MODEL:2 * D_MODEL], kg_ref[...])
    v = qkv[:, 2 * D_MODEL:]
    qb_ref[...] = q.astype(BF16)
    qt_ref[...] = q.T.astype(BF16)
    k_ref[...] = k.reshape(k_ref.shape)
    kb_ref[...] = k.astype(BF16)
    v_ref[...] = v.reshape(v_ref.shape)
    vb_ref[...] = v.astype(BF16)
    vt_ref[...] = v.T.astype(BF16)
    f = jnp.dot(hb, wf_ref[...], preferred_element_type=F32) + fb_ref[...]
    logf_ref[...] = _log_sigmoid(f)[:, :N_HEADS]
    gate_ref[...] = jax.nn.sigmoid(jnp.dot(hb, wgate_ref[...], preferred_element_type=F32))


def _inproj(x, w):
    n = x.shape[0]
    tm = min(256, n)
    tok = lambda c: pl.BlockSpec((tm, c), lambda i: (i, 0))
    tok_t = pl.BlockSpec((D_MODEL, tm), lambda i: (0, i))
    tok_heads = pl.BlockSpec((tm, N_HEADS, HEAD_DIM), lambda i: (i, 0, 0))
    sds = jax.ShapeDtypeStruct
    out_shape = (
        sds((n, D_CONV), F32),
        sds((n, D_MODEL), BF16),
        sds((D_MODEL, n), BF16),
        sds((n, N_HEADS, HEAD_DIM), F32),
        sds((n, N_HEADS, HEAD_DIM), F32),
        sds((n, D_MODEL), BF16),
        sds((n, D_MODEL), BF16),
        sds((D_MODEL, n), BF16),
        sds((n, N_HEADS), F32),
        sds((n, 2 * D_MODEL), F32),
    )
    return pl.pallas_call(
        _inproj_kernel,
        grid=(n // tm,),
        in_specs=[tok(D_MODEL), _const_spec((1, D_MODEL)), _const_spec((D_MODEL, 2 * D_CONV)),
                  _const_spec((D_MODEL, 3 * D_MODEL)), _const_spec((D_MODEL, LANES)),
                  _const_spec((D_MODEL, 2 * D_MODEL)), _const_spec((1, LANES)), _const_spec((1, D_MODEL)),
                  _const_spec((1, D_MODEL)), _const_spec((D_MODEL, LANES)), _const_spec((LANES, D_MODEL))],
        out_specs=(tok(D_CONV), tok(D_MODEL), tok_t, tok_heads, tok_heads, tok(D_MODEL), tok(D_MODEL), tok_t,
                   tok(N_HEADS), tok(2 * D_MODEL)),
        out_shape=out_shape,
        compiler_params=_cparams(("parallel",)),
        name="inproj",
    )(x, w["norm_mix"], w["w_glu"], w["w_qkv"], w["w_f"], w["w_gate"], w["f_bias"], w["q_gain"], w["k_gain"],
      w["seg"], w["segt"])


CONV_ROWS = 32
SUBLANE_TILE = 8


def _conv_kernel(u_ref, halo_ref, buf_ref, w_ref, b_ref, lng_ref, lnb_ref, o_ref, pad_ref, *, tt):
    first = pl.program_id(1) == 0
    pad_ref[0, 0:HIST_ROWS, :] = jnp.where(first, buf_ref[...], halo_ref[...])
    pad_ref[0, HIST_ROWS:HIST_ROWS + tt, :] = u_ref[...]
    span = HIST_ROWS + tt - SUBLANE_TILE
    for s in range(1, SUBLANE_TILE):
        pad_ref[s, 0:span, :] = pad_ref[0, s:s + span, :]
    off = HIST_ROWS - CONV_STATE
    for r in range(tt // CONV_ROWS):
        acc = jnp.broadcast_to(b_ref[...], (CONV_ROWS, D_CONV))
        for k in range(CONV_WIDTH):
            shift = (off + k) % SUBLANE_TILE
            start = r * CONV_ROWS + (off + k) - shift
            acc = acc + w_ref[k:k + 1, :] * pad_ref[shift, start:start + CONV_ROWS, :]
        mu = jnp.mean(acc, axis=-1, keepdims=True)
        d = acc - mu
        var = jnp.mean(d * d, axis=-1, keepdims=True)
        y = d * lax.rsqrt(var + EPS) * lng_ref[...] + lnb_ref[...]
        o_ref[r * CONV_ROWS:(r + 1) * CONV_ROWS, :] = (y * jax.nn.sigmoid(y)).astype(BF16)


def _conv(u, buf, w):
    b, t, c = u.shape
    tt = min(256, t)
    hb = tt // HIST_ROWS
    return pl.pallas_call(
        functools.partial(_conv_kernel, tt=tt),
        grid=(b, t // tt),
        in_specs=[pl.BlockSpec((None, tt, c), lambda bi, i: (bi, i, 0)),
                  pl.BlockSpec((None, HIST_ROWS, c), lambda bi, i: (bi, jnp.maximum(i * hb - 1, 0), 0)),
                  pl.BlockSpec((None, HIST_ROWS, c), lambda bi, i: (bi, 0, 0)),
                  _const_spec((HIST_ROWS, c)), _const_spec((1, c)), _const_spec((1, c)), _const_spec((1, c))],
        out_specs=pl.BlockSpec((None, tt, c), lambda bi, i: (bi, i, 0)),
        out_shape=jax.ShapeDtypeStruct((b, t, c), BF16),
        scratch_shapes=[pltpu.VMEM((SUBLANE_TILE, HIST_ROWS + tt, c), F32)],
        compiler_params=_cparams(("parallel", "arbitrary")),
        name="conv",
    )(u, u, buf, w["conv_w"], w["conv_b"], w["ln_g"], w["ln_b"])


def _cumsum_pieces_kernel(x_ref, tri_ref, place_ref, o_ref, *, nblk):
    group = x_ref.shape[0]

    def body(j, carries):
        sl = pl.ds(pl.multiple_of(j * LANES, LANES), LANES)
        new = []
        for g in range(group):
            local = sum(jnp.dot(tri_ref[...], piece, preferred_element_type=F32)
                        for piece in _split3(x_ref[g, sl, :]))
            cs = local + carries[g]
            out = None
            for j2, piece in enumerate(_split3(cs * (-LOG2E))):
                term = jnp.dot(piece, place_ref[j2], preferred_element_type=F32)
                out = term if out is None else out + term
            o_ref[g, sl, :] = out.astype(BF16)
            new.append(cs[LANES - 1:LANES, :])
        return tuple(new)

    unroll = next(u for u in (4, 3, 2, 1) if nblk % u == 0)
    zero = jnp.zeros((1, x_ref.shape[2]), F32)
    lax.fori_loop(0, nblk, body, (zero,) * group, unroll=unroll)


def _cumsum_pieces(logf):
    b, t, h = logf.shape
    ii = lax.broadcasted_iota(jnp.int32, (LANES, LANES), 0)
    jj = lax.broadcasted_iota(jnp.int32, (LANES, LANES), 1)
    tri = (jj <= ii).astype(BF16)
    head = jnp.arange(h)
    col = jnp.arange(LANES)
    place = jnp.stack([(col[None, :] == (head * BIAS_PIECES + j)[:, None]).astype(BF16)
                       for j in range(BIAS_PIECES)])
    group = 2 if b % 2 == 0 else 1
    return pl.pallas_call(
        functools.partial(_cumsum_pieces_kernel, nblk=t // LANES),
        grid=(b // group,),
        in_specs=[pl.BlockSpec((group, t, h), lambda i: (i, 0, 0)), _const_spec((LANES, LANES)),
                  _const_spec((BIAS_PIECES, h, LANES))],
        out_specs=pl.BlockSpec((group, t, LANES), lambda i: (i, 0, 0)),
        out_shape=jax.ShapeDtypeStruct((b, t, LANES), BF16),
        compiler_params=_cparams(("parallel",)),
        name="cumsum_pieces",
    )(logf, tri, place)


COL_TILE = 256

def _ones_rows(tq, pair):
    r = lax.broadcasted_iota(jnp.int32, (LANES, 2 * tq), 0) - 2 * BIAS_PIECES * pair
    c = lax.broadcasted_iota(jnp.int32, (LANES, 2 * tq), 1)
    first = jnp.where(c < tq, 0, BIAS_PIECES)
    return jnp.where(r >= first, jnp.where(r < first + BIAS_PIECES, 1.0, 0.0), 0.0).astype(BF16)


def _attn_init(m_ref, l_ref, acc_ref):
    m_ref[...] = jnp.full(m_ref.shape, -jnp.inf, F32)
    l_ref[...] = jnp.zeros(l_ref.shape, F32)
    acc_ref[...] = jnp.zeros(acc_ref.shape, F32)


def _attn_chunk(kaug, vt_c, wq_ref, m_ref, l_ref, acc_ref, tq, diagonal):
    ncol = wq_ref.shape[1]
    cw = min(COL_TILE, ncol)
    tiles = [slice(ct * cw, (ct + 1) * cw) for ct in range(ncol // cw)]
    nkeys = [min(kaug.shape[0], (ct * cw) % tq + cw) if diagonal else kaug.shape[0] for ct in range(len(tiles))]
    logits = [jnp.dot(kaug[:nk], wq_ref[:, cs], preferred_element_type=F32) for nk, cs in zip(nkeys, tiles)]
    for ct, cs in enumerate(tiles):
        s = logits[ct]
        if diagonal:
            key = lax.broadcasted_iota(jnp.int32, s.shape, 0)
            qry = lax.broadcasted_iota(jnp.int32, s.shape, 1) + ct * cw
            qry = jnp.where(qry >= tq, qry - tq, qry)
            s = jnp.where(key <= qry, s, -jnp.inf)
        m_old = m_ref[:, cs]
        m_new = jnp.maximum(m_old, jnp.max(s, axis=0, keepdims=True))
        alpha = jnp.exp2(m_old - m_new)
        p = jnp.exp2(s - m_new)
        l_ref[:, cs] = alpha * l_ref[:, cs] + jnp.sum(p, axis=0, keepdims=True)
        acc_ref[:, cs] = alpha * acc_ref[:, cs] + jnp.dot(vt_c[:, :nkeys[ct]], p.astype(BF16),
                                                          preferred_element_type=F32)
        m_ref[:, cs] = m_new


def _attn_finish(o_ref, l_ref, acc_ref, tq):
    ot = (acc_ref[...] / l_ref[...]).T
    lane = lax.broadcasted_iota(jnp.int32, (tq, LANES), 1)
    o_ref[...] = jnp.where(lane < HEAD_DIM, ot[:tq], ot[tq:]).astype(o_ref.dtype)


def _attn_prompt_kernel(qt_ref, k_ref, cp_ref, vt_ref, o_ref, wq_ref, m_ref, l_ref, acc_ref, *, tq):
    qi = pl.program_id(2)
    qt = qt_ref[...]
    row = lax.broadcasted_iota(jnp.int32, qt.shape, 0)
    zero = jnp.zeros_like(qt)
    wq_ref[0:LANES, :] = jnp.concatenate([jnp.where(row < HEAD_DIM, qt, zero), jnp.where(row >= HEAD_DIM, qt, zero)],
                                         axis=1)
    wq_ref[LANES:2 * LANES, :] = _ones_rows(tq, pl.program_id(1))
    _attn_init(m_ref, l_ref, acc_ref)

    def chunk(kc, diagonal):
        ksl = pl.ds(pl.multiple_of(kc * tq, tq), tq)
        kaug = jnp.concatenate([k_ref[ksl, :], cp_ref[ksl, :]], axis=1)
        _attn_chunk(kaug, vt_ref[:, ksl], wq_ref, m_ref, l_ref, acc_ref, tq, diagonal)

    def body(kc, carry):
        chunk(kc, False)
        return carry

    lax.fori_loop(0, qi, body, 0)
    chunk(qi, True)
    _attn_finish(o_ref, l_ref, acc_ref, tq)


def _attn_prompt(qt, kb, cp, vt, b, t):
    d = qt.shape[0]
    tq = min(512, t)
    nq = t // tq
    pairs = d // LANES
    return pl.pallas_call(
        functools.partial(_attn_prompt_kernel, tq=tq),
        grid=(b, pairs, nq),
        in_specs=[pl.BlockSpec((LANES, tq), lambda bi, p, qi: (p, bi * nq + qi)),
                  pl.BlockSpec((None, t, LANES), lambda bi, p, qi: (bi, 0, p)),
                  pl.BlockSpec((None, t, LANES), lambda bi, p, qi: (bi, 0, 0)),
                  pl.BlockSpec((LANES, t), lambda bi, p, qi: (p, bi))],
        out_specs=pl.BlockSpec((None, tq, LANES), lambda bi, p, qi: (bi, qi, p)),
        out_shape=jax.ShapeDtypeStruct((b, t, d), BF16),
        scratch_shapes=[pltpu.VMEM((2 * LANES, 2 * tq), BF16), pltpu.VMEM((1, 2 * tq), F32),
                        pltpu.VMEM((1, 2 * tq), F32), pltpu.VMEM((LANES, 2 * tq), F32)],
        compiler_params=_cparams(("parallel", "parallel", "arbitrary")),
        name="attn_prompt",
    )(qt, kb, cp, vt)


def _attn_sample_kernel(wq_ref, kh_ref, vh_ref, cp_ref, kn_ref, vnt_ref, o_ref, *, past, t):
    wq = wq_ref[...]
    s_hist = jnp.dot(jnp.concatenate([kh_ref[...].astype(BF16), cp_ref[0:past, :]], axis=1), wq, preferred_element_type=F32)
    s_new = jnp.dot(jnp.concatenate([kn_ref[...], cp_ref[past:past + t, :]], axis=1), wq, preferred_element_type=F32)
    key = lax.broadcasted_iota(jnp.int32, s_new.shape, 0)
    qry = lax.broadcasted_iota(jnp.int32, s_new.shape, 1)
    s_new = jnp.where(key <= jnp.where(qry >= t, qry - t, qry), s_new, -jnp.inf)
    m = jnp.maximum(jnp.max(s_hist, axis=0, keepdims=True), jnp.max(s_new, axis=0, keepdims=True))
    p_hist = jnp.exp2(s_hist - m)
    p_new = jnp.exp2(s_new - m)
    denom = jnp.sum(p_hist, axis=0, keepdims=True) + jnp.sum(p_new, axis=0, keepdims=True)
    acc = (jnp.dot(vh_ref[...].astype(BF16).T, p_hist.astype(BF16), preferred_element_type=F32)
           + jnp.dot(vnt_ref[...], p_new.astype(BF16), preferred_element_type=F32))
    ot = (acc / denom).T
    lane = lax.broadcasted_iota(jnp.int32, (t, LANES), 1)
    o_ref[...] = jnp.where(lane < HEAD_DIM, ot[:t], ot[t:]).astype(o_ref.dtype)


def _attn_sample(wq, kn, vnt, hist_k, hist_v, layer, cp):
    b, t, d = kn.shape
    past = hist_k.shape[2]
    pairs = d // LANES
    hist_spec = pl.BlockSpec((None, None, past, LANES), lambda bi, p: (layer, bi, 0, p))
    return pl.pallas_call(
        functools.partial(_attn_sample_kernel, past=past, t=t),
        grid=(b, pairs),
        in_specs=[pl.BlockSpec((None, None, 2 * LANES, 2 * t), lambda bi, p: (bi, p, 0, 0)), hist_spec, hist_spec,
                  pl.BlockSpec((None, cp.shape[1], LANES), lambda bi, p: (bi, 0, 0)),
                  pl.BlockSpec((None, t, LANES), lambda bi, p: (bi, 0, p)),
                  pl.BlockSpec((None, None, LANES, t), lambda bi, p: (bi, p, 0, 0))],
        out_specs=pl.BlockSpec((None, t, LANES), lambda bi, p: (bi, 0, p)),
        out_shape=jax.ShapeDtypeStruct((b, t, d), BF16),
        compiler_params=_cparams(("parallel", "parallel")),
        name="attn_sample",
    )(wq, hist_k, hist_v, cp, kn, vnt)


def _sample_query_weights(qb, b, t):
    pairs = D_MODEL // LANES
    q = jnp.transpose(qb.reshape(b, t, pairs, 2, HEAD_DIM), (0, 2, 3, 4, 1))
    zero = jnp.zeros_like(q[:, :, 0])
    top = jnp.concatenate([jnp.concatenate([q[:, :, 0], zero], axis=-1),
                           jnp.concatenate([zero, q[:, :, 1]], axis=-1)], axis=-2)
    ones = jnp.broadcast_to(jnp.stack([_ones_rows(t, p) for p in range(pairs)]), (b, pairs, LANES, 2 * t))
    return jnp.concatenate([top, ones], axis=-2)


def _outproj_kernel(attn_ref, cact_ref, gate_ref, x_ref, wco_ref, wao_ref, wo_ref, g_ref, wpq_ref, keys_ref,
                    x1_ref, h2_ref, st_ref):
    conv_out = jnp.dot(cact_ref[...], wco_ref[...], preferred_element_type=F32)
    attn_out = jnp.dot(attn_ref[...], wao_ref[...], preferred_element_type=F32)
    merged = gate_ref[:, :D_MODEL] * conv_out + gate_ref[:, D_MODEL:] * attn_out
    x1 = x_ref[...] + jnp.dot(merged.astype(BF16), wo_ref[...], preferred_element_type=F32)
    x1_ref[...] = x1
    h2f = x1 * lax.rsqrt(jnp.mean(x1 * x1, axis=-1, keepdims=True) + EPS) * g_ref[...]
    h2 = h2f.astype(BF16)
    h2_ref[...] = h2f.T.astype(BF16)
    qp =jnp.dot(h2, wpq_ref[...], preferred_element_type=F32).astype(BF16)
    for hp in range(2 * PEER_HEADS):
        st_ref[hp] = lax.dot_general(keys_ref[hp], qp[:, hp * LANES:(hp + 1) * LANES], (((1,), (1,)), ((), ())),
                                     preferred_element_type=F32)


def _outproj(attn, cact, gate, x, w):
    n = x.shape[0]
    tm = min(256, n)
    tok = lambda c: pl.BlockSpec((tm, c), lambda i: (i, 0))
    nhp = 2 * PEER_HEADS
    return pl.pallas_call(
        _outproj_kernel,
        grid=(n // tm,),
        in_specs=[tok(D_MODEL), tok(D_CONV), tok(2 * D_MODEL), tok(D_MODEL),
                  _const_spec((D_CONV, D_MODEL)), _const_spec((D_MODEL, D_MODEL)), _const_spec((D_MODEL, D_MODEL)),
                  _const_spec((1, D_MODEL)), _const_spec((D_MODEL, nhp * LANES)),
                  _const_spec((nhp, PEER_KEYS, LANES))],
        out_specs=(tok(D_MODEL), pl.BlockSpec((D_MODEL, tm), lambda i: (0, i)),
                   pl.BlockSpec((nhp, PEER_KEYS, tm), lambda i: (0, 0, i))),
        out_shape=(jax.ShapeDtypeStruct((n, D_MODEL), F32), jax.ShapeDtypeStruct((D_MODEL, n), BF16),
                   jax.ShapeDtypeStruct((nhp, PEER_KEYS, n), F32)),
        compiler_params=_cparams(("parallel",)),
        name="outproj",
    )(attn, cact, gate, x, w["w_conv_out"], w["w_attn_out"], w["w_out"], w["norm_ffn"], w["peer_w_q"], w["peer_keys"])


def _top_ranks(s, break_ties, want_rank=True):
    nrow = s.shape[0]
    iota = lax.broadcasted_iota(jnp.int32, s.shape, 0).astype(F32)
    rank = jnp.full(s.shape, float(PEER_TOPK), F32) if want_rank else None
    vals = []
    for r in range(PEER_TOPK):
        m = jnp.max(s, axis=0, keepdims=True)
        sel = s == m
        if break_ties:
            sel = iota == jnp.min(jnp.where(sel, iota, float(nrow)), axis=0, keepdims=True)
        if want_rank:
            rank = jnp.where(sel, float(r), rank)
        s = jnp.where(sel, -jnp.inf, s)
        vals.append(m)
    count = jnp.sum(jnp.where(s == -jnp.inf, 1.0, 0.0), axis=0, keepdims=True)
    return rank, vals, count


def _staircase_counts(sv1, sv2):
    top = float(PEER_TOPK)
    iota = lax.broadcasted_iota(jnp.int32, sv1.shape, 0).astype(F32)
    n = jnp.zeros(sv1.shape, F32)
    head = sv1 + sv2[0:1, :]
    m0 = head[0:1, :]
    z = jnp.zeros(m0.shape, F32)
    for _ in range(PEER_TOPK):
        m = jnp.max(head, axis=0, keepdims=True)
        a_star = jnp.min(jnp.where(head == m, iota, top), axis=0, keepdims=True)
        onehot = iota == a_star
        z = z + jnp.exp(m - m0)
        n = n + jnp.where(onehot, 1.0, 0.0)
        n_star = jnp.sum(jnp.where(onehot, n, 0.0), axis=0, keepdims=True)
        nxt = jnp.sum(jnp.where(iota == n_star, sv2, 0.0), axis=0, keepdims=True)
        nxt = jnp.where(n_star >= top, -jnp.inf, nxt)
        a_val = jnp.sum(jnp.where(onehot, sv1, 0.0), axis=0, keepdims=True)
        head = jnp.where(onehot, a_val + nxt, head)
    return n, z


def _topk_head(st_ref, r2_ref, nn_ref, e2_ref, cc_ref, h, break_ties):
    top = float(PEER_TOPK)
    s1 = st_ref[2 * h]
    s2 = st_ref[2 * h + 1]
    rank1, v1, count1 = _top_ranks(s1, break_ties, want_rank=break_ties)
    rank2, v2, count2 = _top_ranks(s2, break_ties)
    n, z = _staircase_counts(jnp.concatenate(v1, axis=0), jnp.concatenate(v2, axis=0))
    nn = jnp.zeros(s1.shape, F32)
    for a in range(PEER_TOPK):
        is_a = rank1 == float(a) if break_ties else s1 == v1[a]
        nn = jnp.where(is_a, n[a:a + 1, :], nn)
    taken1 = rank1 < top if break_ties else s1 >= v1[PEER_TOPK - 1]
    r2_ref[h] = rank2.astype(BF16)
    nn_ref[h] = nn
    e2_ref[h] = jnp.where(rank2 < top, jnp.exp(s2 - v2[0]), 0.0).astype(BF16)
    cc_ref[h] = jnp.where(taken1, jnp.exp(s1 - v1[0]), 0.0) / z
    return jnp.max(jnp.abs(count1 - top) + jnp.abs(count2 - top)) > 0.0


def _topk_kernel(st_ref, r2_ref, nn_ref, e2_ref, cc_ref):
    def body(i, carry):
        tied_a = _topk_head(st_ref, r2_ref, nn_ref, e2_ref, cc_ref, 2 * i, False)
        tied_b = _topk_head(st_ref, r2_ref, nn_ref, e2_ref, cc_ref, 2 * i + 1, False)

        @pl.when(jnp.logical_or(tied_a, tied_b))
        def _():
            _topk_head(st_ref, r2_ref, nn_ref, e2_ref, cc_ref, 2 * i, True)
            _topk_head(st_ref, r2_ref, nn_ref, e2_ref, cc_ref, 2 * i + 1, True)

        return carry

    lax.fori_loop(0, PEER_HEADS // 2, body, 0)


def _topk(st):
    nhp, nkeys, n = st.shape
    tl = min(2 * LANES, n)
    out_spec = pl.BlockSpec((PEER_HEADS, nkeys, tl), lambda i: (0, 0, i))
    sds = lambda dt: jax.ShapeDtypeStruct((PEER_HEADS, nkeys, n), dt)
    return pl.pallas_call(
        _topk_kernel,
        grid=(n // tl,),
        in_specs=[pl.BlockSpec((nhp, nkeys, tl), lambda i: (0, 0, i))],
        out_specs=(out_spec, out_spec, out_spec, out_spec),
        out_shape=(sds(BF16), sds(F32), sds(BF16), sds(F32)),
        compiler_params=_cparams(("parallel",)),
        name="peer_topk",
    )(st)


GELU_A = -2.0 * 0.7978845608028654 * LOG2E
GELU_B = GELU_A * 0.044715


def _gelu_tanh(x):
    return x / (1.0 + jnp.exp2(x * (GELU_A + GELU_B * (x * x))))


def _peer_kernel(h2t_ref, x1_ref, u_ref, vt_ref, r2_ref, e2_ref, nn_ref, cc_ref, o_ref, acc_ref, a_ref, gw_ref, w_ref,
                 *, ni, ne, steps):
    s = pl.program_id(0)

    @pl.when(s == 0)
    def _():
        a_ref[...] = jnp.zeros(a_ref.shape, F32)
        gw_ref[...] = jnp.zeros(gw_ref.shape, BF16)

    @pl.when(jnp.logical_or(s < 2, lax.rem(jnp.maximum(s - 2, 0), ne) == 0))
    def _():
        acc_ref[...] = jnp.zeros(acc_ref.shape, F32)

    slot_new = lax.rem(s, 2)
    slot_mid = 1 - slot_new
    tm = h2t_ref.shape[1]
    tiles_per_block = nn_ref.shape[1] // ni
    row0 = lax.rem(lax.rem(jnp.clip(s - 1, 0, steps - 1), ne), tiles_per_block) * ni

    def rows_bf16(ref, h, i):
        tile = jnp.broadcast_to(ref[h, pl.ds(row0 + i, 1), :], (16, tm)).astype(BF16)
        return jnp.concatenate([tile] * (PEER_KEYS // 16), axis=0)

    for i in range(ni):
        w = None
        for h in range(PEER_HEADS):
            keep = r2_ref[h] < rows_bf16(nn_ref, h, i)
            term = jnp.where(keep, e2_ref[h] * rows_bf16(cc_ref, h, i), jnp.zeros((), BF16))
            w = term if w is None else w + term
        w_ref[i * PEER_KEYS:(i + 1) * PEER_KEYS, :] = w
    a_ref[slot_new] = jnp.dot(u_ref[...], h2t_ref[...], preferred_element_type=F32)
    acc_ref[...] += jnp.dot(vt_ref[...], gw_ref[slot_new], preferred_element_type=F32)
    for i in range(ni):
        rows = slice(i * PEER_KEYS, (i + 1) * PEER_KEYS)
        gw_ref[slot_mid, rows, :] = _gelu_tanh(a_ref[slot_mid, rows, :]).astype(BF16) * w_ref[rows, :]

    @pl.when(jnp.logical_and(s >= 2, lax.rem(jnp.maximum(s - 2, 0), ne) == ne - 1))
    def _():
        o_ref[...] = x1_ref[...] + acc_ref[...].T


def _peer(h2t, x1, r2, nn, e2, cc, u_tab, vt_tab):
    d, n = h2t.shape
    tm = min(1024, n)
    te = 512
    ne = u_tab.shape[0] // te
    ni = te // PEER_KEYS
    steps = (n // tm) * ne
    pair = lambda s, lag: jnp.clip(s - lag, 0, steps - 1)
    tile = lambda s, lag: pair(s, lag) // ne
    expert = lambda s, lag: lax.rem(pair(s, lag), ne)
    row_block = max(ni, SUBLANE_TILE)
    key_spec = pl.BlockSpec((PEER_HEADS, PEER_KEYS, tm), lambda s: (0, 0, tile(s, 1)))
    row_spec = pl.BlockSpec((PEER_HEADS, row_block, tm), lambda s: (0, expert(s, 1) // (row_block // ni), tile(s, 1)))
    out_spec = pl.BlockSpec((tm, d), lambda s: (tile(s, 2), 0))
    return pl.pallas_call(
        functools.partial(_peer_kernel, ni=ni, ne=ne, steps=steps),
        grid=(steps + 2,),
        in_specs=[pl.BlockSpec((d, tm), lambda s: (0, tile(s, 0))), out_spec,
                  pl.BlockSpec((te, d), lambda s: (expert(s, 0), 0)),
                  pl.BlockSpec((d, te), lambda s: (0, expert(s, 2))), key_spec, key_spec, row_spec, row_spec],
        out_specs=out_spec,
        out_shape=jax.ShapeDtypeStruct((n, d), F32),
        scratch_shapes=[pltpu.VMEM((d, tm), F32), pltpu.VMEM((2, te, tm), F32), pltpu.VMEM((2, te, tm), BF16),
                        pltpu.VMEM((te, tm), BF16)],
        compiler_params=_cparams(("arbitrary",)),
        name="peer_experts",
    )(h2t, x1, u_tab, vt_tab, r2, e2, nn, cc)


def _prep_weights(l, norm_mix, w_in, conv_w, conv_b, conv_norm_g, conv_norm_b, w_conv_out, forget_bias, q_norm,
                  k_norm, w_attn_out, w_out, norm_ffn, peer_w_q, peer_keys, peer_u, peer_v):
    o1 = 2 * D_CONV
    o2 = o1 + 3 * D_MODEL
    o3 = o2 + N_HEADS
    wl = w_in[l]
    head_of_col = jnp.arange(D_MODEL) // HEAD_DIM
    seg = (head_of_col[:, None] == jnp.arange(LANES)[None, :]).astype(BF16)
    row = lambda a: a.reshape(1, -1).astype(F32)
    return {
        "norm_mix": row(norm_mix[l]),
        "w_glu": wl[:, :o1].astype(BF16),
        "w_qkv": wl[:, o1:o2].astype(BF16),
        "w_f": jnp.pad(wl[:, o2:o3], ((0, 0), (0, LANES - N_HEADS))).astype(BF16),
        "w_gate": wl[:, o3:].astype(BF16),
        "f_bias": jnp.pad(row(forget_bias[l]), ((0, 0), (0, LANES - N_HEADS))),
        "q_gain": row(jnp.tile(q_norm[l], N_HEADS) * (HEAD_DIM ** -0.5 * LOG2E)),
        "k_gain": row(jnp.tile(k_norm[l], N_HEADS)),
        "seg": seg,
        "segt": seg.T,
        "conv_w": jnp.pad(conv_w[l], ((0, HIST_ROWS - CONV_WIDTH), (0, 0))),
        "conv_b": row(conv_b[l]),
        "ln_g": row(conv_norm_g[l]),
        "ln_b": row(conv_norm_b[l]),
        "w_conv_out": w_conv_out[l].astype(BF16),
        "w_attn_out": w_attn_out[l].astype(BF16),
        "w_out": w_out[l].astype(BF16),
        "norm_ffn": row(norm_ffn[l]),
        "peer_w_q": peer_w_q[l].astype(BF16),
        "peer_keys": jnp.pad(peer_keys[l].reshape(2 * PEER_HEADS, PEER_KEYS, -1).astype(BF16),
                             ((0, 0), (0, 0), (0, LANES - peer_keys.shape[-1]))),
        "peer_u": peer_u[l].astype(BF16),
        "peer_vt": peer_v[l].T.astype(BF16),
    }


def _layer_group(x3, conv_state, cache, w):
    b, t, d = x3.shape
    n = b * t
    assert t >= CONV_STATE
    x = x3.reshape(n, d)
    u, qb, qt, k, v, kb, vb, vt, logf, gate = _inproj(x, w)
    u3 = u.reshape(b, t, D_CONV)
    if conv_state is None:
        buf = jnp.zeros((b, HIST_ROWS, D_CONV), F32)
    else:
        buf = jnp.pad(conv_state, ((0, 0), (HIST_ROWS - CONV_STATE, 0), (0, 0)))
    cact = _conv(u3, buf, w)

    logf3 = logf.reshape(b, t, N_HEADS)
    if cache is None:
        cp = _cumsum_pieces(logf3)
        attn = _attn_prompt(qt, kb.reshape(b, t, d), cp, vt, b, t)
    else:
        hist_k, hist_v, cache_logf, layer = cache
        past = hist_k.shape[2]
        total = past + t
        padded = -(-total // LANES) * LANES
        lf = jnp.concatenate([cache_logf[layer].astype(F32), logf3], axis=1)
        cp = _cumsum_pieces(jnp.pad(lf, ((0, 0), (0, padded - total), (0, 0))))
        vnt = jnp.transpose(vb.reshape(b, t, d // LANES, LANES), (0, 2, 3, 1))
        attn = _attn_sample(_sample_query_weights(qb, b, t), kb.reshape(b, t, d), vnt, hist_k, hist_v, layer, cp)

    x1, h2, st = _outproj(attn.reshape(n, d), cact.reshape(n, D_CONV), gate, x, w)
    r2, nn, e2, cc = _topk(st)
    y = _peer(h2, x1, r2, nn, e2, cc, w["peer_u"], w["peer_vt"])
    return (y.reshape(b, t, d), k.reshape(b, t, N_HEADS, HEAD_DIM), v.reshape(b, t, N_HEADS, HEAD_DIM),
            logf3, u3[:, t - CONV_STATE:, :])


def kernel(x_prompt, x_sample, cache_k, cache_v, cache_logf, state_conv, norm_mix, w_in, conv_w, conv_b, conv_norm_g, conv_norm_b, w_conv_out, forget_bias, q_norm, k_norm, w_attn_out, w_out, norm_ffn, peer_w_q, peer_keys, peer_u, peer_v):
    depth = w_in.shape[0]
    yp, ys = x_prompt, x_sample
    outs_p, outs_s = [], []
    flat = cache_k.shape[:3] + (D_MODEL,)
    hist_k = cache_k.reshape(flat)
    hist_v = cache_v.reshape(flat)
    for l in range(depth):
        w = _prep_weights(l, norm_mix, w_in, conv_w, conv_b, conv_norm_g, conv_norm_b, w_conv_out, forget_bias,
                          q_norm, k_norm, w_attn_out, w_out, norm_ffn, peer_w_q, peer_keys, peer_u, peer_v)
        yp, *rest_p = _layer_group(yp, None, None, w)
        ys, *rest_s = _layer_group(ys, state_conv[l], (hist_k, hist_v, cache_logf, l), w)
        outs_p.append(rest_p)
        outs_s.append(rest_s)
    stack = lambda outs, i: jnp.stack([o[i] for o in outs])
    return (yp, ys, stack(outs_p, 0), stack(outs_p, 1), stack(outs_p, 2), stack(outs_p, 3),
            stack(outs_s, 0), stack(outs_s, 1), stack(outs_s, 2), stack(outs_s, 3))
```

```python
import functools
import math

import jax
import jax.numpy as jnp
from jax import lax
from jax.experimental import pallas as pl
from jax.experimental.pallas import tpu as pltpu

F32 = jnp.float32
BF16 = jnp.bfloat16

D_MODEL = 1024
N_HEADS = 16
HEAD_DIM = 64
D_CONV = 512
CONV_WIDTH = 31
CONV_STATE = CONV_WIDTH - 1
PEER_HEADS = 8
PEER_KEYS = 128
PEER_TOPK = 16
EPS = 1e-6
LOG2E = math.log2(math.e)
LANES = 128
HIST_ROWS = 32
BIAS_PIECES = 3
VMEM_LIMIT = 56 * 1024 * 1024


def _cparams(sem):
    return pltpu.CompilerParams(dimension_semantics=sem, vmem_limit_bytes=VMEM_LIMIT)


def _const_spec(shape):
    n = len(shape)
    return pl.BlockSpec(shape, lambda *_: (0,) * n)


def _split3(x):
    hi = x.astype(BF16)
    r1 = x - hi.astype(F32)
    mid = r1.astype(BF16)
    lo = (r1 - mid.astype(F32)).astype(BF16)
    return hi, mid, lo


def _split_dot(x, w):
    return sum(jnp.dot(p, w, preferred_element_type=F32) for p in _split3(x)[:2])


def _log_sigmoid(x):
    return jnp.minimum(x, 0.0) - jnp.log(1.0 + jnp.exp(-jnp.abs(x)))


def _inproj_kernel(x_ref, g_ref, wglu_ref, wqkv_ref, wf_ref, wgate_ref, fb_ref, qg_ref, kg_ref, seg_ref, segt_ref,
                   u_ref, qb_ref, qt_ref, k_ref, v_ref, kb_ref, vb_ref, vt_ref, logf_ref, gate_ref):
    x = x_ref[...]
    h = x * lax.rsqrt(jnp.mean(x * x, axis=-1, keepdims=True) + EPS) * g_ref[...]
    hb = h.astype(BF16)
    glu = jnp.dot(hb, wglu_ref[...], preferred_element_type=F32)
    u_ref[...] = glu[:, :D_CONV] * jax.nn.sigmoid(glu[:, D_CONV:])
    qkv = jnp.dot(hb, wqkv_ref[...], preferred_element_type=F32)

    def head_norm(t, gain):
        ss = _split_dot(t * t, seg_ref[...])
        r = lax.rsqrt(ss * (1.0 / HEAD_DIM) + EPS)
        return t * _split_dot(r, segt_ref[...]) * gain

    q = head_norm(qkv[:, :D_MODEL], qg_ref[...])
    k = head_norm(qkv[:, D_MODEL:2 * D_MODEL], kg_ref[...])
    v = qkv[:, 2 * D_MODEL:]
    qb_ref[...] = q.astype(BF16)
    qt_ref[...] = q.T.astype(BF16)
    k_ref[...] = k.reshape(k_ref.shape)
    kb_ref[...] = k.astype(BF16)
    v_ref[...] = v.reshape(v_ref.shape)
    vb_ref[...] = v.astype(BF16)
    vt_ref[...] = v.T.astype(BF16)
    f = jnp.dot(hb, wf_ref[...], preferred_element_type=F32) + fb_ref[...]
    logf_ref[...] = _log_sigmoid(f)[:, :N_HEADS]
    gate_ref[...] = jax.nn.sigmoid(jnp.dot(hb, wgate_ref[...], preferred_element_type=F32))


def _inproj(x, w):
    n = x.shape[0]
    tm = min(256, n)
    tok = lambda c: pl.BlockSpec((tm, c), lambda i: (i, 0))
    tok_t = pl.BlockSpec((D_MODEL, tm), lambda i: (0, i))
    tok_heads = pl.BlockSpec((tm, N_HEADS, HEAD_DIM), lambda i: (i, 0, 0))
    sds = jax.ShapeDtypeStruct
    out_shape = (
        sds((n, D_CONV), F32),
        sds((n, D_MODEL), BF16),
        sds((D_MODEL, n), BF16),
        sds((n, N_HEADS, HEAD_DIM), F32),
        sds((n, N_HEADS, HEAD_DIM), F32),
        sds((n, D_MODEL), BF16),
        sds((n, D_MODEL), BF16),
        sds((D_MODEL, n), BF16),
        sds((n, N_HEADS), F32),
        sds((n, 2 * D_MODEL), F32),
    )
    return pl.pallas_call(
        _inproj_kernel,
        grid=(n // tm,),
        in_specs=[tok(D_MODEL), _const_spec((1, D_MODEL)), _const_spec((D_MODEL, 2 * D_CONV)),
                  _const_spec((D_MODEL, 3 * D_MODEL)), _const_spec((D_MODEL, LANES)),
                  _const_spec((D_MODEL, 2 * D_MODEL)), _const_spec((1, LANES)), _const_spec((1, D_MODEL)),
                  _const_spec((1, D_MODEL)), _const_spec((D_MODEL, LANES)), _const_spec((LANES, D_MODEL))],
        out_specs=(tok(D_CONV), tok(D_MODEL), tok_t, tok_heads, tok_heads, tok(D_MODEL), tok(D_MODEL), tok_t,
                   tok(N_HEADS), tok(2 * D_MODEL)),
        out_shape=out_shape,
        compiler_params=_cparams(("parallel",)),
        name="inproj",
    )(x, w["norm_mix"], w["w_glu"], w["w_qkv"], w["w_f"], w["w_gate"], w["f_bias"], w["q_gain"], w["k_gain"],
      w["seg"], w["segt"])


CONV_ROWS = 32
SUBLANE_TILE = 8


def _conv_kernel(u_ref, halo_ref, buf_ref, w_ref, b_ref, lng_ref, lnb_ref, o_ref, pad_ref, *, tt):
    first = pl.program_id(1) == 0
    pad_ref[0, 0:HIST_ROWS, :] = jnp.where(first, buf_ref[...], halo_ref[...])
    pad_ref[0, HIST_ROWS:HIST_ROWS + tt, :] = u_ref[...]
    span = HIST_ROWS + tt - SUBLANE_TILE
    for s in range(1, SUBLANE_TILE):
        pad_ref[s, 0:span, :] = pad_ref[0, s:s + span, :]
    off = HIST_ROWS - CONV_STATE
    for r in range(tt // CONV_ROWS):
        acc = jnp.broadcast_to(b_ref[...], (CONV_ROWS, D_CONV))
        for k in range(CONV_WIDTH):
            shift = (off + k) % SUBLANE_TILE
            start = r * CONV_ROWS + (off + k) - shift
            acc = acc + w_ref[k:k + 1, :] * pad_ref[shift, start:start + CONV_ROWS, :]
        mu = jnp.mean(acc, axis=-1, keepdims=True)
        d = acc - mu
        var = jnp.mean(d * d, axis=-1, keepdims=True)
        y = d * lax.rsqrt(var + EPS) * lng_ref[...] + lnb_ref[...]
        o_ref[r * CONV_ROWS:(r + 1) * CONV_ROWS, :] = (y * jax.nn.sigmoid(y)).astype(BF16)


def _conv(u, buf, w):
    b, t, c = u.shape
    tt = min(256, t)
    hb = tt // HIST_ROWS
    return pl.pallas_call(
        functools.partial(_conv_kernel, tt=tt),
        grid=(b, t // tt),
        in_specs=[pl.BlockSpec((None, tt, c), lambda bi, i: (bi, i, 0)),
                  pl.BlockSpec((None, HIST_ROWS, c), lambda bi, i: (bi, jnp.maximum(i * hb - 1, 0), 0)),
                  pl.BlockSpec((None, HIST_ROWS, c), lambda bi, i: (bi, 0, 0)),
                  _const_spec((HIST_ROWS, c)), _const_spec((1, c)), _const_spec((1, c)), _const_spec((1, c))],
        out_specs=pl.BlockSpec((None, tt, c), lambda bi, i: (bi, i, 0)),
        out_shape=jax.ShapeDtypeStruct((b, t, c), BF16),
        scratch_shapes=[pltpu.VMEM((SUBLANE_TILE, HIST_ROWS + tt, c), F32)],
        compiler_params=_cparams(("parallel", "arbitrary")),
        name="conv",
    )(u, u, buf, w["conv_w"], w["conv_b"], w["ln_g"], w["ln_b"])


def _cumsum_pieces_kernel(x_ref, tri_ref, place_ref, o_ref, *, nblk):
    group = x_ref.shape[0]

    def body(j, carries):
        sl = pl.ds(pl.multiple_of(j * LANES, LANES), LANES)
        new = []
        for g in range(group):
            local = sum(jnp.dot(tri_ref[...], piece, preferred_element_type=F32)
                        for piece in _split3(x_ref[g, sl, :]))
            cs = local + carries[g]
            out = None
            for j2, piece in enumerate(_split3(cs * (-LOG2E))):
                term = jnp.dot(piece, place_ref[j2], preferred_element_type=F32)
                out = term if out is None else out + term
            o_ref[g, sl, :] = out.astype(BF16)
            new.append(cs[LANES - 1:LANES, :])
        return tuple(new)

    unroll = next(u for u in (4, 3, 2, 1) if nblk % u == 0)
    zero = jnp.zeros((1, x_ref.shape[2]), F32)
    lax.fori_loop(0, nblk, body, (zero,) * group, unroll=unroll)


def _cumsum_pieces(logf):
    b, t, h = logf.shape
    ii = lax.broadcasted_iota(jnp.int32, (LANES, LANES), 0)
    jj = lax.broadcasted_iota(jnp.int32, (LANES, LANES), 1)
    tri = (jj <= ii).astype(BF16)
    head = jnp.arange(h)
    col = jnp.arange(LANES)
    place = jnp.stack([(col[None, :] == (head * BIAS_PIECES + j)[:, None]).astype(BF16)
                       for j in range(BIAS_PIECES)])
    group = 2 if b % 2 == 0 else 1
    return pl.pallas_call(
        functools.partial(_cumsum_pieces_kernel, nblk=t // LANES),
        grid=(b // group,),
        in_specs=[pl.BlockSpec((group, t, h), lambda i: (i, 0, 0)), _const_spec((LANES, LANES)),
                  _const_spec((BIAS_PIECES, h, LANES))],
        out_specs=pl.BlockSpec((group, t, LANES), lambda i: (i, 0, 0)),
        out_shape=jax.ShapeDtypeStruct((b, t, LANES), BF16),
        compiler_params=_cparams(("parallel",)),
        name="cumsum_pieces",
    )(logf, tri, place)


COL_TILE = 256

def _ones_rows(tq, pair):
    r = lax.broadcasted_iota(jnp.int32, (LANES, 2 * tq), 0) - 2 * BIAS_PIECES * pair
    c = lax.broadcasted_iota(jnp.int32, (LANES, 2 * tq), 1)
    first = jnp.where(c < tq, 0, BIAS_PIECES)
    return jnp.where(r >= first, jnp.where(r < first + BIAS_PIECES, 1.0, 0.0), 0.0).astype(BF16)


def _attn_init(m_ref, l_ref, acc_ref):
    m_ref[...] = jnp.full(m_ref.shape, -jnp.inf, F32)
    l_ref[...] = jnp.zeros(l_ref.shape, F32)
    acc_ref[...] = jnp.zeros(acc_ref.shape, F32)


def _attn_chunk(kaug, vt_c, wq_ref, m_ref, l_ref, acc_ref, tq, diagonal):
    ncol = wq_ref.shape[1]
    cw = min(COL_TILE, ncol)
    tiles = [slice(ct * cw, (ct + 1) * cw) for ct in range(ncol // cw)]
    nkeys = [min(kaug.shape[0], (ct * cw) % tq + cw) if diagonal else kaug.shape[0] for ct in range(len(tiles))]
    logits = [jnp.dot(kaug[:nk], wq_ref[:, cs], preferred_element_type=F32) for nk, cs in zip(nkeys, tiles)]
    for ct, cs in enumerate(tiles):
        s = logits[ct]
        if diagonal:
            key = lax.broadcasted_iota(jnp.int32, s.shape, 0)
            qry = lax.broadcasted_iota(jnp.int32, s.shape, 1) + ct * cw
            qry = jnp.where(qry >= tq, qry - tq, qry)
            s = jnp.where(key <= qry, s, -jnp.inf)
        m_old = m_ref[:, cs]
        m_new = jnp.maximum(m_old, jnp.max(s, axis=0, keepdims=True))
        alpha = jnp.exp2(m_old - m_new)
        p = jnp.exp2(s - m_new)
        l_ref[:, cs] = alpha * l_ref[:, cs] + jnp.sum(p, axis=0, keepdims=True)
        acc_ref[:, cs] = alpha * acc_ref[:, cs] + jnp.dot(vt_c[:, :nkeys[ct]], p.astype(BF16),
                                                          preferred_element_type=F32)
        m_ref[:, cs] = m_new


def _attn_finish(o_ref, l_ref, acc_ref, tq):
    ot = (acc_ref[...] / l_ref[...]).T
    lane = lax.broadcasted_iota(jnp.int32, (tq, LANES), 1)
    o_ref[...] = jnp.where(lane < HEAD_DIM, ot[:tq], ot[tq:]).astype(o_ref.dtype)


def _attn_prompt_kernel(qt_ref, k_ref, cp_ref, vt_ref, o_ref, wq_ref, m_ref, l_ref, acc_ref, *, tq):
    qi = pl.program_id(2)
    qt = qt_ref[...]
    row = lax.broadcasted_iota(jnp.int32, qt.shape, 0)
    zero = jnp.zeros_like(qt)
    wq_ref[0:LANES, :] = jnp.concatenate([jnp.where(row < HEAD_DIM, qt, zero), jnp.where(row >= HEAD_DIM, qt, zero)],
                                         axis=1)
    wq_ref[LANES:2 * LANES, :] = _ones_rows(tq, pl.program_id(1))
    _attn_init(m_ref, l_ref, acc_ref)

    def chunk(kc, diagonal):
        ksl = pl.ds(pl.multiple_of(kc * tq, tq), tq)
        kaug = jnp.concatenate([k_ref[ksl, :], cp_ref[ksl, :]], axis=1)
        _attn_chunk(kaug, vt_ref[:, ksl], wq_ref, m_ref, l_ref, acc_ref, tq, diagonal)

    def body(kc, carry):
        chunk(kc, False)
        return carry

    lax.fori_loop(0, qi, body, 0)
    chunk(qi, True)
    _attn_finish(o_ref, l_ref, acc_ref, tq)


def _attn_prompt(qt, kb, cp, vt, b, t):
    d = qt.shape[0]
    tq = min(1024, t)
    nq = t // tq
    pairs = d // LANES
    return pl.pallas_call(
        functools.partial(_attn_prompt_kernel, tq=tq),
        grid=(b, pairs, nq),
        in_specs=[pl.BlockSpec((LANES, tq), lambda bi, p, qi: (p, bi * nq + qi)),
                  pl.BlockSpec((None, t, LANES), lambda bi, p, qi: (bi, 0, p)),
                  pl.BlockSpec((None, t, LANES), lambda bi, p, qi: (bi, 0, 0)),
                  pl.BlockSpec((LANES, t), lambda bi, p, qi: (p, bi))],
        out_specs=pl.BlockSpec((None, tq, LANES), lambda bi, p, qi: (bi, qi, p)),
        out_shape=jax.ShapeDtypeStruct((b, t, d), BF16),
        scratch_shapes=[pltpu.VMEM((2 * LANES, 2 * tq), BF16), pltpu.VMEM((1, 2 * tq), F32),
                        pltpu.VMEM((1, 2 * tq), F32), pltpu.VMEM((LANES, 2 * tq), F32)],
        compiler_params=_cparams(("parallel", "parallel", "arbitrary")),
        name="attn_prompt",
    )(qt, kb, cp, vt)


def _attn_sample_kernel(wq_ref, kh_ref, vh_ref, cp_ref, kn_ref, vnt_ref, o_ref, *, past, t):
    wq = wq_ref[...]
    s_hist = jnp.dot(jnp.concatenate([kh_ref[...].astype(BF16), cp_ref[0:past, :]], axis=1), wq, preferred_element_type=F32)
    s_new = jnp.dot(jnp.concatenate([kn_ref[...], cp_ref[past:past + t, :]], axis=1), wq, preferred_element_type=F32)
    key = lax.broadcasted_iota(jnp.int32, s_new.shape, 0)
    qry = lax.broadcasted_iota(jnp.int32, s_new.shape, 1)
    s_new = jnp.where(key <= jnp.where(qry >= t, qry - t, qry), s_new, -jnp.inf)
    m = jnp.maximum(jnp.max(s_hist, axis=0, keepdims=True), jnp.max(s_new, axis=0, keepdims=True))
    p_hist = jnp.exp2(s_hist - m)
    p_new = jnp.exp2(s_new - m)
    denom = jnp.sum(p_hist, axis=0, keepdims=True) + jnp.sum(p_new, axis=0, keepdims=True)
    acc = (jnp.dot(vh_ref[...].astype(BF16).T, p_hist.astype(BF16), preferred_element_type=F32)
           + jnp.dot(vnt_ref[...], p_new.astype(BF16), preferred_element_type=F32))
    ot = (acc / denom).T
    lane = lax.broadcasted_iota(jnp.int32, (t, LANES), 1)
    o_ref[...] = jnp.where(lane < HEAD_DIM, ot[:t], ot[t:]).astype(o_ref.dtype)


def _attn_sample(wq, kn, vnt, hist_k, hist_v, layer, cp):
    b, t, d = kn.shape
    past = hist_k.shape[2]
    pairs = d // LANES
    hist_spec = pl.BlockSpec((None, None, past, LANES), lambda bi, p: (layer, bi, 0, p))
    return pl.pallas_call(
        functools.partial(_attn_sample_kernel, past=past, t=t),
        grid=(b, pairs),
        in_specs=[pl.BlockSpec((None, None, 2 * LANES, 2 * t), lambda bi, p: (bi, p, 0, 0)), hist_spec, hist_spec,
                  pl.BlockSpec((None, cp.shape[1], LANES), lambda bi, p: (bi, 0, 0)),
                  pl.BlockSpec((None, t, LANES), lambda bi, p: (bi, 0, p)),
                  pl.BlockSpec((None, None, LANES, t), lambda bi, p: (bi, p, 0, 0))],
        out_specs=pl.BlockSpec((None, t, LANES), lambda bi, p: (bi, 0, p)),
        out_shape=jax.ShapeDtypeStruct((b, t, d), BF16),
        compiler_params=_cparams(("parallel", "parallel")),
        name="attn_sample",
    )(wq, hist_k, hist_v, cp, kn, vnt)


def _sample_query_weights(qb, b, t):
    pairs = D_MODEL // LANES
    q = jnp.transpose(qb.reshape(b, t, pairs, 2, HEAD_DIM), (0, 2, 3, 4, 1))
    zero = jnp.zeros_like(q[:, :, 0])
    top = jnp.concatenate([jnp.concatenate([q[:, :, 0], zero], axis=-1),
                           jnp.concatenate([zero, q[:, :, 1]], axis=-1)], axis=-2)
    ones = jnp.broadcast_to(jnp.stack([_ones_rows(t, p) for p in range(pairs)]), (b, pairs, LANES, 2 * t))
    return jnp.concatenate([top, ones], axis=-2)


def _outproj_kernel(attn_ref, cact_ref, gate_ref, x_ref, wco_ref, wao_ref, wo_ref, g_ref, wpq_ref, keys_ref,
                    x1_ref, h2_ref, st_ref):
    conv_out = jnp.dot(cact_ref[...], wco_ref[...], preferred_element_type=F32)
    attn_out = jnp.dot(attn_ref[...], wao_ref[...], preferred_element_type=F32)
    merged = gate_ref[:, :D_MODEL] * conv_out + gate_ref[:, D_MODEL:] * attn_out
    x1 = x_ref[...] + jnp.dot(merged.astype(BF16), wo_ref[...], preferred_element_type=F32)
    x1_ref[...] = x1
    h2f = x1 * lax.rsqrt(jnp.mean(x1 * x1, axis=-1, keepdims=True) + EPS) * g_ref[...]
    h2 = h2f.astype(BF16)
    h2_ref[...] = h2f.T.astype(BF16)
    qp =jnp.dot(h2, wpq_ref[...], preferred_element_type=F32).astype(BF16)
    for hp in range(2 * PEER_HEADS):
        st_ref[hp] = lax.dot_general(keys_ref[hp], qp[:, hp * LANES:(hp + 1) * LANES], (((1,), (1,)), ((), ())),
                                     preferred_element_type=F32)


def _outproj(attn, cact, gate, x, w):
    n = x.shape[0]
    tm = min(256, n)
    tok = lambda c: pl.BlockSpec((tm, c), lambda i: (i, 0))
    nhp = 2 * PEER_HEADS
    return pl.pallas_call(
        _outproj_kernel,
        grid=(n // tm,),
        in_specs=[tok(D_MODEL), tok(D_CONV), tok(2 * D_MODEL), tok(D_MODEL),
                  _const_spec((D_CONV, D_MODEL)), _const_spec((D_MODEL, D_MODEL)), _const_spec((D_MODEL, D_MODEL)),
                  _const_spec((1, D_MODEL)), _const_spec((D_MODEL, nhp * LANES)),
                  _const_spec((nhp, PEER_KEYS, LANES))],
        out_specs=(tok(D_MODEL), pl.BlockSpec((D_MODEL, tm), lambda i: (0, i)),
                   pl.BlockSpec((nhp, PEER_KEYS, tm), lambda i: (0, 0, i))),
        out_shape=(jax.ShapeDtypeStruct((n, D_MODEL), F32), jax.ShapeDtypeStruct((D_MODEL, n), BF16),
                   jax.ShapeDtypeStruct((nhp, PEER_KEYS, n), F32)),
        compiler_params=_cparams(("parallel",)),
        name="outproj",
    )(attn, cact, gate, x, w["w_conv_out"], w["w_attn_out"], w["w_out"], w["norm_ffn"], w["peer_w_q"], w["peer_keys"])


def _top_ranks(s, break_ties, want_rank=True):
    nrow = s.shape[0]
    iota = lax.broadcasted_iota(jnp.int32, s.shape, 0).astype(F32)
    rank = jnp.full(s.shape, float(PEER_TOPK), F32) if want_rank else None
    vals = []
    for r in range(PEER_TOPK):
        m = jnp.max(s, axis=0, keepdims=True)
        sel = s == m
        if break_ties:
            sel = iota == jnp.min(jnp.where(sel, iota, float(nrow)), axis=0, keepdims=True)
        if want_rank:
            rank = jnp.where(sel, float(r), rank)
        s = jnp.where(sel, -jnp.inf, s)
        vals.append(m)
    count = jnp.sum(jnp.where(s == -jnp.inf, 1.0, 0.0), axis=0, keepdims=True)
    return rank, vals, count


def _staircase_counts(sv1, sv2):
    top = float(PEER_TOPK)
    iota = lax.broadcasted_iota(jnp.int32, sv1.shape, 0).astype(F32)
    n = jnp.zeros(sv1.shape, F32)
    head = sv1 + sv2[0:1, :]
    m0 = head[0:1, :]
    z = jnp.zeros(m0.shape, F32)
    for _ in range(PEER_TOPK):
        m = jnp.max(head, axis=0, keepdims=True)
        a_star = jnp.min(jnp.where(head == m, iota, top), axis=0, keepdims=True)
        onehot = iota == a_star
        z = z + jnp.exp(m - m0)
        n = n + jnp.where(onehot, 1.0, 0.0)
        n_star = jnp.sum(jnp.where(onehot, n, 0.0), axis=0, keepdims=True)
        nxt = jnp.sum(jnp.where(iota == n_star, sv2, 0.0), axis=0, keepdims=True)
        nxt = jnp.where(n_star >= top, -jnp.inf, nxt)
        a_val = jnp.sum(jnp.where(onehot, sv1, 0.0), axis=0, keepdims=True)
        head = jnp.where(onehot, a_val + nxt, head)
    return n, z


def _topk_head(st_ref, r2_ref, nn_ref, e2_ref, cc_ref, h, break_ties):
    top = float(PEER_TOPK)
    s1 = st_ref[2 * h]
    s2 = st_ref[2 * h + 1]
    rank1, v1, count1 = _top_ranks(s1, break_ties, want_rank=break_ties)
    rank2, v2, count2 = _top_ranks(s2, break_ties)
    n, z = _staircase_counts(jnp.concatenate(v1, axis=0), jnp.concatenate(v2, axis=0))
    nn = jnp.zeros(s1.shape, F32)
    for a in range(PEER_TOPK):
        is_a = rank1 == float(a) if break_ties else s1 == v1[a]
        nn = jnp.where(is_a, n[a:a + 1, :], nn)
    taken1 = rank1 < top if break_ties else s1 >= v1[PEER_TOPK - 1]
    r2_ref[h] = rank2.astype(BF16)
    nn_ref[h] = nn
    e2_ref[h] = jnp.where(rank2 < top, jnp.exp(s2 - v2[0]), 0.0).astype(BF16)
    cc_ref[h] = jnp.where(taken1, jnp.exp(s1 - v1[0]), 0.0) / z
    return jnp.max(jnp.abs(count1 - top) + jnp.abs(count2 - top)) > 0.0


def _topk_kernel(st_ref, r2_ref, nn_ref, e2_ref, cc_ref):
    def body(i, carry):
        tied_a = _topk_head(st_ref, r2_ref, nn_ref, e2_ref, cc_ref, 2 * i, False)
        tied_b = _topk_head(st_ref, r2_ref, nn_ref, e2_ref, cc_ref, 2 * i + 1, False)

        @pl.when(jnp.logical_or(tied_a, tied_b))
        def _():
            _topk_head(st_ref, r2_ref, nn_ref, e2_ref, cc_ref, 2 * i, True)
            _topk_head(st_ref, r2_ref, nn_ref, e2_ref, cc_ref, 2 * i + 1, True)

        return carry

    lax.fori_loop(0, PEER_HEADS // 2, body, 0)


def _topk(st):
    nhp, nkeys, n = st.shape
    tl = min(2 * LANES, n)
    out_spec = pl.BlockSpec((PEER_HEADS, nkeys, tl), lambda i: (0, 0, i))
    sds = lambda dt: jax.ShapeDtypeStruct((PEER_HEADS, nkeys, n), dt)
    return pl.pallas_call(
        _topk_kernel,
        grid=(n // tl,),
        in_specs=[pl.BlockSpec((nhp, nkeys, tl), lambda i: (0, 0, i))],
        out_specs=(out_spec, out_spec, out_spec, out_spec),
        out_shape=(sds(BF16), sds(F32), sds(BF16), sds(F32)),
        compiler_params=_cparams(("parallel",)),
        name="peer_topk",
    )(st)


GELU_A = -2.0 * 0.7978845608028654 * LOG2E
GELU_B = GELU_A * 0.044715


def _gelu_tanh(x):
    return x / (1.0 + jnp.exp2(x * (GELU_A + GELU_B * (x * x))))


def _peer_kernel(h2t_ref, x1_ref, u_ref, vt_ref, r2_ref, e2_ref, nn_ref, cc_ref, o_ref, acc_ref, a_ref, gw_ref, w_ref,
                 *, ni, ne, steps):
    s = pl.program_id(0)

    @pl.when(s == 0)
    def _():
        a_ref[...] = jnp.zeros(a_ref.shape, F32)
        gw_ref[...] = jnp.zeros(gw_ref.shape, BF16)

    @pl.when(jnp.logical_or(s < 2, lax.rem(jnp.maximum(s - 2, 0), ne) == 0))
    def _():
        acc_ref[...] = jnp.zeros(acc_ref.shape, F32)

    slot_new = lax.rem(s, 2)
    slot_mid = 1 - slot_new
    tm = h2t_ref.shape[1]
    tiles_per_block = nn_ref.shape[1] // ni
    row0 = lax.rem(lax.rem(jnp.clip(s - 1, 0, steps - 1), ne), tiles_per_block) * ni

    def rows_bf16(ref, h, i):
        tile = jnp.broadcast_to(ref[h, pl.ds(row0 + i, 1), :], (16, tm)).astype(BF16)
        return jnp.concatenate([tile] * (PEER_KEYS // 16), axis=0)

    for i in range(ni):
        w = None
        for h in range(PEER_HEADS):
            keep = r2_ref[h] < rows_bf16(nn_ref, h, i)
            term = jnp.where(keep, e2_ref[h] * rows_bf16(cc_ref, h, i), jnp.zeros((), BF16))
            w = term if w is None else w + term
        w_ref[i * PEER_KEYS:(i + 1) * PEER_KEYS, :] = w
    a_ref[slot_new] = jnp.dot(u_ref[...], h2t_ref[...], preferred_element_type=F32)
    acc_ref[...] += jnp.dot(vt_ref[...], gw_ref[slot_new], preferred_element_type=F32)
    for i in range(ni):
        rows = slice(i * PEER_KEYS, (i + 1) * PEER_KEYS)
        gw_ref[slot_mid, rows, :] = _gelu_tanh(a_ref[slot_mid, rows, :]).astype(BF16) * w_ref[rows, :]

    @pl.when(jnp.logical_and(s >= 2, lax.rem(jnp.maximum(s - 2, 0), ne) == ne - 1))
    def _():
        o_ref[...] = x1_ref[...] + acc_ref[...].T


def _peer(h2t, x1, r2, nn, e2, cc, u_tab, vt_tab):
    d, n = h2t.shape
    tm = min(1024, n)
    te = 512
    ne = u_tab.shape[0] // te
    ni = te // PEER_KEYS
    steps = (n // tm) * ne
    pair = lambda s, lag: jnp.clip(s - lag, 0, steps - 1)
    tile = lambda s, lag: pair(s, lag) // ne
    expert = lambda s, lag: lax.rem(pair(s, lag), ne)
    row_block = max(ni, SUBLANE_TILE)
    key_spec = pl.BlockSpec((PEER_HEADS, PEER_KEYS, tm), lambda s: (0, 0, tile(s, 1)))
    row_spec = pl.BlockSpec((PEER_HEADS, row_block, tm), lambda s: (0, expert(s, 1) // (row_block // ni), tile(s, 1)))
    out_spec = pl.BlockSpec((tm, d), lambda s: (tile(s, 2), 0))
    return pl.pallas_call(
        functools.partial(_peer_kernel, ni=ni, ne=ne, steps=steps),
        grid=(steps + 2,),
        in_specs=[pl.BlockSpec((d, tm), lambda s: (0, tile(s, 0))), out_spec,
                  pl.BlockSpec((te, d), lambda s: (expert(s, 0), 0)),
                  pl.BlockSpec((d, te), lambda s: (0, expert(s, 2))), key_spec, key_spec, row_spec, row_spec],
        out_specs=out_spec,
        out_shape=jax.ShapeDtypeStruct((n, d), F32),
        scratch_shapes=[pltpu.VMEM((d, tm), F32), pltpu.VMEM((2, te, tm), F32), pltpu.VMEM((2, te, tm), BF16),
                        pltpu.VMEM((te, tm), BF16)],
        compiler_params=_cparams(("arbitrary",)),
        name="peer_experts",
    )(h2t, x1, u_tab, vt_tab, r2, e2, nn, cc)


def _prep_weights(l, norm_mix, w_in, conv_w, conv_b, conv_norm_g, conv_norm_b, w_conv_out, forget_bias, q_norm,
                  k_norm, w_attn_out, w_out, norm_ffn, peer_w_q, peer_keys, peer_u, peer_v):
    o1 = 2 * D_CONV
    o2 = o1 + 3 * D_MODEL
    o3 = o2 + N_HEADS
    wl = w_in[l]
    head_of_col = jnp.arange(D_MODEL) // HEAD_DIM
    seg = (head_of_col[:, None] == jnp.arange(LANES)[None, :]).astype(BF16)
    row = lambda a: a.reshape(1, -1).astype(F32)
    return {
        "norm_mix": row(norm_mix[l]),
        "w_glu": wl[:, :o1].astype(BF16),
        "w_qkv": wl[:, o1:o2].astype(BF16),
        "w_f": jnp.pad(wl[:, o2:o3], ((0, 0), (0, LANES - N_HEADS))).astype(BF16),
        "w_gate": wl[:, o3:].astype(BF16),
        "f_bias": jnp.pad(row(forget_bias[l]), ((0, 0), (0, LANES - N_HEADS))),
        "q_gain": row(jnp.tile(q_norm[l], N_HEADS) * (HEAD_DIM ** -0.5 * LOG2E)),
        "k_gain": row(jnp.tile(k_norm[l], N_HEADS)),
        "seg": seg,
        "segt": seg.T,
        "conv_w": jnp.pad(conv_w[l], ((0, HIST_ROWS - CONV_WIDTH), (0, 0))),
        "conv_b": row(conv_b[l]),
        "ln_g": row(conv_norm_g[l]),
        "ln_b": row(conv_norm_b[l]),
        "w_conv_out": w_conv_out[l].astype(BF16),
        "w_attn_out": w_attn_out[l].astype(BF16),
        "w_out": w_out[l].astype(BF16),
        "norm_ffn": row(norm_ffn[l]),
        "peer_w_q": peer_w_q[l].astype(BF16),
        "peer_keys": jnp.pad(peer_keys[l].reshape(2 * PEER_HEADS, PEER_KEYS, -1).astype(BF16),
                             ((0, 0), (0, 0), (0, LANES - peer_keys.shape[-1]))),
        "peer_u": peer_u[l].astype(BF16),
        "peer_vt": peer_v[l].T.astype(BF16),
    }


def _layer_group(x3, conv_state, cache, w):
    b, t, d = x3.shape
    n = b * t
    assert t >= CONV_STATE
    x = x3.reshape(n, d)
    u, qb, qt, k, v, kb, vb, vt, logf, gate = _inproj(x, w)
    u3 = u.reshape(b, t, D_CONV)
    if conv_state is None:
        buf = jnp.zeros((b, HIST_ROWS, D_CONV), F32)
    else:
        buf = jnp.pad(conv_state, ((0, 0), (HIST_ROWS - CONV_STATE, 0), (0, 0)))
    cact = _conv(u3, buf, w)

    logf3 = logf.reshape(b, t, N_HEADS)
    if cache is None:
        cp = _cumsum_pieces(logf3)
        attn = _attn_prompt(qt, kb.reshape(b, t, d), cp, vt, b, t)
    else:
        hist_k, hist_v, cache_logf, layer = cache
        past = hist_k.shape[2]
        total = past + t
        padded = -(-total // LANES) * LANES
        lf = jnp.concatenate([cache_logf[layer].astype(F32), logf3], axis=1)
        cp = _cumsum_pieces(jnp.pad(lf, ((0, 0), (0, padded - total), (0, 0))))
        vnt = jnp.transpose(vb.reshape(b, t, d // LANES, LANES), (0, 2, 3, 1))
        attn = _attn_sample(_sample_query_weights(qb, b, t), kb.reshape(b, t, d), vnt, hist_k, hist_v, layer, cp)

    x1, h2, st = _outproj(attn.reshape(n, d), cact.reshape(n, D_CONV), gate, x, w)
    r2, nn, e2, cc = _topk(st)
    y = _peer(h2, x1, r2, nn, e2, cc, w["peer_u"], w["peer_vt"])
    return (y.reshape(b, t, d), k.reshape(b, t, N_HEADS, HEAD_DIM), v.reshape(b, t, N_HEADS, HEAD_DIM),
            logf3, u3[:, t - CONV_STATE:, :])


def kernel(x_prompt, x_sample, cache_k, cache_v, cache_logf, state_conv, norm_mix, w_in, conv_w, conv_b, conv_norm_g, conv_norm_b, w_conv_out, forget_bias, q_norm, k_norm, w_attn_out, w_out, norm_ffn, peer_w_q, peer_keys, peer_u, peer_v):
    depth = w_in.shape[0]
    yp, ys = x_prompt, x_sample
    outs_p, outs_s = [], []
    flat = cache_k.shape[:3] + (D_MODEL,)
    hist_k = cache_k.reshape(flat)
    hist_v = cache_v.reshape(flat)
    for l in range(depth):
        w = _prep_weights(l, norm_mix, w_in, conv_w, conv_b, conv_norm_g, conv_norm_b, w_conv_out, forget_bias,
                          q_norm, k_norm, w_attn_out, w_out, norm_ffn, peer_w_q, peer_keys, peer_u, peer_v)
        yp, *rest_p = _layer_group(yp, None, None, w)
        ys, *rest_s = _layer_group(ys, state_conv[l], (hist_k, hist_v, cache_logf, l), w)
        outs_p.append(rest_p)
        outs_s.append(rest_s)
    stack = lambda outs, i: jnp.stack([o[i] for o in outs])
    return (yp, ys, stack(outs_p, 0), stack(outs_p, 1), stack(outs_p, 2), stack(outs_p, 3),
            stack(outs_s, 0), stack(outs_s, 1), stack(outs_s, 2), stack(outs_s, 3))
```

```python
import functools
import math

import jax
import jax.numpy as jnp
from jax import lax
from jax.experimental import pallas as pl
from jax.experimental.pallas import tpu as pltpu

F32 = jnp.float32
BF16 = jnp.bfloat16

D_MODEL = 1024
N_HEADS = 16
HEAD_DIM = 64
D_CONV = 512
CONV_WIDTH = 31
CONV_STATE = CONV_WIDTH - 1
PEER_HEADS = 8
PEER_KEYS = 128
PEER_TOPK = 16
EPS = 1e-6
LOG2E = math.log2(math.e)
LANES = 128
HIST_ROWS = 32
BIAS_PIECES = 3
VMEM_LIMIT = 56 * 1024 * 1024


def _cparams(sem):
    return pltpu.CompilerParams(dimension_semantics=sem, vmem_limit_bytes=VMEM_LIMIT)


def _const_spec(shape):
    n = len(shape)
    return pl.BlockSpec(shape, lambda *_: (0,) * n)


def _split3(x):
    hi = x.astype(BF16)
    r1 = x - hi.astype(F32)
    mid = r1.astype(BF16)
    lo = (r1 - mid.astype(F32)).astype(BF16)
    return hi, mid, lo


def _split_dot(x, w):
    return sum(jnp.dot(p, w, preferred_element_type=F32) for p in _split3(x)[:2])


def _log_sigmoid(x):
    return jnp.minimum(x, 0.0) - jnp.log(1.0 + jnp.exp(-jnp.abs(x)))


def _inproj_kernel(x_ref, g_ref, wglu_ref, wqkv_ref, wf_ref, wgate_ref, fb_ref, qg_ref, kg_ref, seg_ref, segt_ref,
                   u_ref, qb_ref, qt_ref, k_ref, v_ref, kb_ref, vb_ref, vt_ref, logf_ref, gate_ref):
    x = x_ref[...]
    h = x * lax.rsqrt(jnp.mean(x * x, axis=-1, keepdims=True) + EPS) * g_ref[...]
    hb = h.astype(BF16)
    glu = jnp.dot(hb, wglu_ref[...], preferred_element_type=F32)
    u_ref[...] = glu[:, :D_CONV] * jax.nn.sigmoid(glu[:, D_CONV:])
    qkv = jnp.dot(hb, wqkv_ref[...], preferred_element_type=F32)

    def head_norm(t, gain):
        ss = _split_dot(t * t, seg_ref[...])
        r = lax.rsqrt(ss * (1.0 / HEAD_DIM) + EPS)
        return t * _split_dot(r, segt_ref[...]) * gain

    q = head_norm(qkv[:, :D_MODEL], qg_ref[...])
    k = head_norm(qkv[:, D_MODEL:2 * D_MODEL], kg_ref[...])
    v = qkv[:, 2 * D_MODEL:]
    qb_ref[...] = q.astype(BF16)
    qt_ref[...] = q.T.astype(BF16)
    k_ref[...] = k.reshape(k_ref.shape)
    kb_ref[...] = k.astype(BF16)
    v_ref[...] = v.reshape(v_ref.shape)
    vb_ref[...] = v.astype(BF16)
    vt_ref[...] = v.T.astype(BF16)
    f = jnp.dot(hb, wf_ref[...], preferred_element_type=F32) + fb_ref[...]
    logf_ref[...] = _log_sigmoid(f)[:, :N_HEADS]
    gate_ref[...] = jax.nn.sigmoid(jnp.dot(hb, wgate_ref[...], preferred_element_type=F32))


def _inproj(x, w):
    n = x.shape[0]
    tm = min(256, n)
    tok = lambda c: pl.BlockSpec((tm, c), lambda i: (i, 0))
    tok_t = pl.BlockSpec((D_MODEL, tm), lambda i: (0, i))
    tok_heads = pl.BlockSpec((tm, N_HEADS, HEAD_DIM), lambda i: (i, 0, 0))
    sds = jax.ShapeDtypeStruct
    out_shape = (
        sds((n, D_CONV), F32),
        sds((n, D_MODEL), BF16),
        sds((D_MODEL, n), BF16),
        sds((n, N_HEADS, HEAD_DIM), F32),
        sds((n, N_HEADS, HEAD_DIM), F32),
        sds((n, D_MODEL), BF16),
        sds((n, D_MODEL), BF16),
        sds((D_MODEL, n), BF16),
        sds((n, N_HEADS), F32),
        sds((n, 2 * D_MODEL), F32),
    )
    return pl.pallas_call(
        _inproj_kernel,
        grid=(n // tm,),
        in_specs=[tok(D_MODEL), _const_spec((1, D_MODEL)), _const_spec((D_MODEL, 2 * D_CONV)),
                  _const_spec((D_MODEL, 3 * D_MODEL)), _const_spec((D_MODEL, LANES)),
                  _const_spec((D_MODEL, 2 * D_MODEL)), _const_spec((1, LANES)), _const_spec((1, D_MODEL)),
                  _const_spec((1, D_MODEL)), _const_spec((D_MODEL, LANES)), _const_spec((LANES, D_MODEL))],
        out_specs=(tok(D_CONV), tok(D_MODEL), tok_t, tok_heads, tok_heads, tok(D_MODEL), tok(D_MODEL), tok_t,
                   tok(N_HEADS), tok(2 * D_MODEL)),
        out_shape=out_shape,
        compiler_params=_cparams(("parallel",)),
        name="inproj",
    )(x, w["norm_mix"], w["w_glu"], w["w_qkv"], w["w_f"], w["w_gate"], w["f_bias"], w["q_gain"], w["k_gain"],
      w["seg"], w["segt"])


CONV_ROWS = 32
SUBLANE_TILE = 8


def _conv_kernel(u_ref, halo_ref, buf_ref, w_ref, b_ref, lng_ref, lnb_ref, o_ref, pad_ref, *, tt):
    first = pl.program_id(1) == 0
    pad_ref[0, 0:HIST_ROWS, :] = jnp.where(first, buf_ref[...], halo_ref[...])
    pad_ref[0, HIST_ROWS:HIST_ROWS + tt, :] = u_ref[...]
    span = HIST_ROWS + tt - SUBLANE_TILE
    for s in range(1, SUBLANE_TILE):
        pad_ref[s, 0:span, :] = pad_ref[0, s:s + span, :]
    off = HIST_ROWS - CONV_STATE
    for r in range(tt // CONV_ROWS):
        acc = jnp.broadcast_to(b_ref[...], (CONV_ROWS, D_CONV))
        for k in range(CONV_WIDTH):
            shift = (off + k) % SUBLANE_TILE
            start = r * CONV_ROWS + (off + k) - shift
            acc = acc + w_ref[k:k + 1, :] * pad_ref[shift, start:start + CONV_ROWS, :]
        mu = jnp.mean(acc, axis=-1, keepdims=True)
        d = acc - mu
        var = jnp.mean(d * d, axis=-1, keepdims=True)
        y = d * lax.rsqrt(var + EPS) * lng_ref[...] + lnb_ref[...]
        o_ref[r * CONV_ROWS:(r + 1) * CONV_ROWS, :] = (y * jax.nn.sigmoid(y)).astype(BF16)


def _conv(u, buf, w):
    b, t, c = u.shape
    tt = min(256, t)
    hb = tt // HIST_ROWS
    return pl.pallas_call(
        functools.partial(_conv_kernel, tt=tt),
        grid=(b, t // tt),
        in_specs=[pl.BlockSpec((None, tt, c), lambda bi, i: (bi, i, 0)),
                  pl.BlockSpec((None, HIST_ROWS, c), lambda bi, i: (bi, jnp.maximum(i * hb - 1, 0), 0)),
                  pl.BlockSpec((None, HIST_ROWS, c), lambda bi, i: (bi, 0, 0)),
                  _const_spec((HIST_ROWS, c)), _const_spec((1, c)), _const_spec((1, c)), _const_spec((1, c))],
        out_specs=pl.BlockSpec((None, tt, c), lambda bi, i: (bi, i, 0)),
        out_shape=jax.ShapeDtypeStruct((b, t, c), BF16),
        scratch_shapes=[pltpu.VMEM((SUBLANE_TILE, HIST_ROWS + tt, c), F32)],
        compiler_params=_cparams(("parallel", "arbitrary")),
        name="conv",
    )(u, u, buf, w["conv_w"], w["conv_b"], w["ln_g"], w["ln_b"])


def _cumsum_pieces_kernel(x_ref, tri_ref, place_ref, o_ref, *, nblk):
    group = x_ref.shape[0]

    def body(j, carries):
        sl = pl.ds(pl.multiple_of(j * LANES, LANES), LANES)
        new = []
        for g in range(group):
            local = sum(jnp.dot(tri_ref[...], piece, preferred_element_type=F32)
                        for piece in _split3(x_ref[g, sl, :]))
            cs = local + carries[g]
            out = None
            for j2, piece in enumerate(_split3(cs * (-LOG2E))):
                term = jnp.dot(piece, place_ref[j2], preferred_element_type=F32)
                out = term if out is None else out + term
            o_ref[g, sl, :] = out.astype(BF16)
            new.append(cs[LANES - 1:LANES, :])
        return tuple(new)

    unroll = next(u for u in (4, 3, 2, 1) if nblk % u == 0)
    zero = jnp.zeros((1, x_ref.shape[2]), F32)
    lax.fori_loop(0, nblk, body, (zero,) * group, unroll=unroll)


def _cumsum_pieces(logf):
    b, t, h = logf.shape
    ii = lax.broadcasted_iota(jnp.int32, (LANES, LANES), 0)
    jj = lax.broadcasted_iota(jnp.int32, (LANES, LANES), 1)
    tri = (jj <= ii).astype(BF16)
    head = jnp.arange(h)
    col = jnp.arange(LANES)
    place = jnp.stack([(col[None, :] == (head * BIAS_PIECES + j)[:, None]).astype(BF16)
                       for j in range(BIAS_PIECES)])
    group = 2 if b % 2 == 0 else 1
    return pl.pallas_call(
        functools.partial(_cumsum_pieces_kernel, nblk=t // LANES),
        grid=(b // group,),
        in_specs=[pl.BlockSpec((group, t, h), lambda i: (i, 0, 0)), _const_spec((LANES, LANES)),
                  _const_spec((BIAS_PIECES, h, LANES))],
        out_specs=pl.BlockSpec((group, t, LANES), lambda i: (i, 0, 0)),
        out_shape=jax.ShapeDtypeStruct((b, t, LANES), BF16),
        compiler_params=_cparams(("parallel",)),
        name="cumsum_pieces",
    )(logf, tri, place)


COL_TILE = 256

def _ones_rows(tq, pair):
    r = lax.broadcasted_iota(jnp.int32, (LANES, 2 * tq), 0) - 2 * BIAS_PIECES * pair
    c = lax.broadcasted_iota(jnp.int32, (LANES, 2 * tq), 1)
    first = jnp.where(c < tq, 0, BIAS_PIECES)
    return jnp.where(r >= first, jnp.where(r < first + BIAS_PIECES, 1.0, 0.0), 0.0).astype(BF16)


def _attn_init(m_ref, l_ref, acc_ref):
    m_ref[...] = jnp.full(m_ref.shape, -jnp.inf, F32)
    l_ref[...] = jnp.zeros(l_ref.shape, F32)
    acc_ref[...] = jnp.zeros(acc_ref.shape, F32)


def _attn_chunk(kaug, vt_c, wq_ref, m_ref, l_ref, acc_ref, tq, diagonal):
    ncol = wq_ref.shape[1]
    cw = min(COL_TILE, ncol)
    tiles = [slice(ct * cw, (ct + 1) * cw) for ct in range(ncol // cw)]
    nkeys = [min(kaug.shape[0], (ct * cw) % tq + cw) if diagonal else kaug.shape[0] for ct in range(len(tiles))]
    logits = [jnp.dot(kaug[:nk], wq_ref[:, cs], preferred_element_type=F32) for nk, cs in zip(nkeys, tiles)]
    for ct, cs in enumerate(tiles):
        s = logits[ct]
        if diagonal:
            key = lax.broadcasted_iota(jnp.int32, s.shape, 0)
            qry = lax.broadcasted_iota(jnp.int32, s.shape, 1) + ct * cw
            qry = jnp.where(qry >= tq, qry - tq, qry)
            s = jnp.where(key <= qry, s, -jnp.inf)
        m_old = m_ref[:, cs]
        m_new = jnp.maximum(m_old, jnp.max(s, axis=0, keepdims=True))
        alpha = jnp.exp2(m_old - m_new)
        p = jnp.exp2(s - m_new)
        l_ref[:, cs] = alpha * l_ref[:, cs] + jnp.sum(p, axis=0, keepdims=True)
        acc_ref[:, cs] = alpha * acc_ref[:, cs] + jnp.dot(vt_c[:, :nkeys[ct]], p.astype(BF16),
                                                          preferred_element_type=F32)
        m_ref[:, cs] = m_new


def _attn_finish(o_ref, l_ref, acc_ref, tq):
    ot = (acc_ref[...] / l_ref[...]).T
    lane = lax.broadcasted_iota(jnp.int32, (tq, LANES), 1)
    o_ref[...] = jnp.where(lane < HEAD_DIM, ot[:tq], ot[tq:]).astype(o_ref.dtype)


def _attn_prompt_kernel(qt_ref, k_ref, cp_ref, vt_ref, o_ref, wq_ref, m_ref, l_ref, acc_ref, *, tq):
    qi = pl.program_id(2)
    qt = qt_ref[...]
    row = lax.broadcasted_iota(jnp.int32, qt.shape, 0)
    zero = jnp.zeros_like(qt)
    wq_ref[0:LANES, :] = jnp.concatenate([jnp.where(row < HEAD_DIM, qt, zero), jnp.where(row >= HEAD_DIM, qt, zero)],
                                         axis=1)
    wq_ref[LANES:2 * LANES, :] = _ones_rows(tq, pl.program_id(1))
    _attn_init(m_ref, l_ref, acc_ref)

    def chunk(kc, diagonal):
        ksl = pl.ds(pl.multiple_of(kc * tq, tq), tq)
        kaug = jnp.concatenate([k_ref[ksl, :], cp_ref[ksl, :]], axis=1)
        _attn_chunk(kaug, vt_ref[:, ksl], wq_ref, m_ref, l_ref, acc_ref, tq, diagonal)

    def body(kc, carry):
        chunk(kc, False)
        return carry

    lax.fori_loop(0, qi, body, 0)
    chunk(qi, True)
    _attn_finish(o_ref, l_ref, acc_ref, tq)


def _attn_prompt(qt, kb, cp, vt, b, t):
    d = qt.shape[0]
    tq = min(1024, t)
    nq = t // tq
    pairs = d // LANES
    return pl.pallas_call(
        functools.partial(_attn_prompt_kernel, tq=tq),
        grid=(b, pairs, nq),
        in_specs=[pl.BlockSpec((LANES, tq), lambda bi, p, qi: (p, bi * nq + qi)),
                  pl.BlockSpec((None, t, LANES), lambda bi, p, qi: (bi, 0, p)),
                  pl.BlockSpec((None, t, LANES), lambda bi, p, qi: (bi, 0, 0)),
                  pl.BlockSpec((LANES, t), lambda bi, p, qi: (p, bi))],
        out_specs=pl.BlockSpec((None, tq, LANES), lambda bi, p, qi: (bi, qi, p)),
        out_shape=jax.ShapeDtypeStruct((b, t, d), BF16),
        scratch_shapes=[pltpu.VMEM((2 * LANES, 2 * tq), BF16), pltpu.VMEM((1, 2 * tq), F32),
                        pltpu.VMEM((1, 2 * tq), F32), pltpu.VMEM((LANES, 2 * tq), F32)],
        compiler_params=_cparams(("parallel", "parallel", "arbitrary")),
        name="attn_prompt",
    )(qt, kb, cp, vt)


def _cumsum_lanes_kernel(x_ref, tri_ref, o_ref, *, nblk):
    def body(j, carry):
        sl = pl.ds(pl.multiple_of(j * LANES, LANES), LANES)
        cs = sum(jnp.dot(p, tri_ref[...], preferred_element_type=F32) for p in _split3(x_ref[:, sl])) + carry
        o_ref[:, sl] = cs * LOG2E
        return cs[:, LANES - 1:LANES]

    lax.fori_loop(0, nblk, body, jnp.zeros((x_ref.shape[0], 1), F32))


def _cumsum_lanes(x):
    r, t = x.shape
    tr = min(64, r)
    ii = lax.broadcasted_iota(jnp.int32, (LANES, LANES), 0)
    jj = lax.broadcasted_iota(jnp.int32, (LANES, LANES), 1)
    tri = (ii <= jj).astype(BF16)
    return pl.pallas_call(
        functools.partial(_cumsum_lanes_kernel, nblk=t // LANES),
        grid=(r // tr,),
        in_specs=[pl.BlockSpec((tr, t), lambda i: (i, 0)), _const_spec((LANES, LANES))],
        out_specs=pl.BlockSpec((tr, t), lambda i: (i, 0)),
        out_shape=jax.ShapeDtypeStruct((r, t), F32),
        compiler_params=_cparams(("parallel",)),
        name="cumsum_lanes",
    )(x, tri)


CACHED_HEADS = 4


def _attn_cached_kernel(q_ref, kt_ref, vt_ref, ch_ref, kn_ref, vn_ref, cn_ref, o_ref):
    for h in range(q_ref.shape[0]):
        q = q_ref[h]
        s_hist = jnp.dot(q, kt_ref[h].astype(BF16), preferred_element_type=F32) - ch_ref[h:h + 1, :]
        s_new = lax.dot_general(q, kn_ref[h], (((1,), (1,)), ((), ())), preferred_element_type=F32) - cn_ref[h:h + 1, :]
        qry = lax.broadcasted_iota(jnp.int32, s_new.shape, 0)
        key = lax.broadcasted_iota(jnp.int32, s_new.shape, 1)
        s_new = jnp.where(key <= qry, s_new, -jnp.inf)
        m = jnp.maximum(jnp.max(s_hist, axis=-1, keepdims=True), jnp.max(s_new, axis=-1, keepdims=True))
        p_hist = jnp.exp2(s_hist - m)
        p_new = jnp.exp2(s_new - m)
        denom = jnp.sum(p_hist, axis=-1, keepdims=True) + jnp.sum(p_new, axis=-1, keepdims=True)
        acc = (lax.dot_general(p_hist.astype(BF16), vt_ref[h].astype(BF16), (((1,), (1,)), ((), ())),
                               preferred_element_type=F32)
               + jnp.dot(p_new.astype(BF16), vn_ref[h], preferred_element_type=F32))
        o_ref[h] = (acc / denom).astype(o_ref.dtype)


def _attn_cached(q, kn, vn, cache_kt, cache_vt, layer, c_hist, c_new):
    b, h, t, dh = q.shape
    past = cache_kt.shape[-1]
    g = CACHED_HEADS
    new_spec = pl.BlockSpec((None, g, t, dh), lambda bi, hi: (bi, hi, 0, 0))
    hist_spec = pl.BlockSpec((None, None, g, dh, past), lambda bi, hi: (layer, bi, hi, 0, 0))
    return pl.pallas_call(
        _attn_cached_kernel,
        grid=(b, h // g),
        in_specs=[new_spec, hist_spec, hist_spec, pl.BlockSpec((None, None, g, past), lambda bi, hi: (bi, hi, 0, 0)),
                  new_spec, new_spec, pl.BlockSpec((None, None, g, t), lambda bi, hi: (bi, hi, 0, 0))],
        out_specs=new_spec,
        out_shape=jax.ShapeDtypeStruct((b, h, t, dh), BF16),
        compiler_params=_cparams(("parallel", "parallel")),
        name="attn_cached",
    )(q, cache_kt, cache_vt, c_hist, kn, vn, c_new)


def _attn_sample_kernel(wq_ref, kh_ref, vh_ref, cp_ref, kn_ref, vnt_ref, o_ref, *, past, t):
    wq = wq_ref[...]
    s_hist = jnp.dot(jnp.concatenate([kh_ref[...].astype(BF16), cp_ref[0:past, :]], axis=1), wq, preferred_element_type=F32)
    s_new = jnp.dot(jnp.concatenate([kn_ref[...], cp_ref[past:past + t, :]], axis=1), wq, preferred_element_type=F32)
    key = lax.broadcasted_iota(jnp.int32, s_new.shape, 0)
    qry = lax.broadcasted_iota(jnp.int32, s_new.shape, 1)
    s_new = jnp.where(key <= jnp.where(qry >= t, qry - t, qry), s_new, -jnp.inf)
    m = jnp.maximum(jnp.max(s_hist, axis=0, keepdims=True), jnp.max(s_new, axis=0, keepdims=True))
    p_hist = jnp.exp2(s_hist - m)
    p_new = jnp.exp2(s_new - m)
    denom = jnp.sum(p_hist, axis=0, keepdims=True) + jnp.sum(p_new, axis=0, keepdims=True)
    acc = (jnp.dot(vh_ref[...].astype(BF16).T, p_hist.astype(BF16), preferred_element_type=F32)
           + jnp.dot(vnt_ref[...], p_new.astype(BF16), preferred_element_type=F32))
    ot = (acc / denom).T
    lane = lax.broadcasted_iota(jnp.int32, (t, LANES), 1)
    o_ref[...] = jnp.where(lane < HEAD_DIM, ot[:t], ot[t:]).astype(o_ref.dtype)


def _attn_sample(wq, kn, vnt, hist_k, hist_v, layer, cp):
    b, t, d = kn.shape
    past = hist_k.shape[2]
    pairs = d // LANES
    hist_spec = pl.BlockSpec((None, None, past, LANES), lambda bi, p: (layer, bi, 0, p))
    return pl.pallas_call(
        functools.partial(_attn_sample_kernel, past=past, t=t),
        grid=(b, pairs),
        in_specs=[pl.BlockSpec((None, None, 2 * LANES, 2 * t), lambda bi, p: (bi, p, 0, 0)), hist_spec, hist_spec,
                  pl.BlockSpec((None, cp.shape[1], LANES), lambda bi, p: (bi, 0, 0)),
                  pl.BlockSpec((None, t, LANES), lambda bi, p: (bi, 0, p)),
                  pl.BlockSpec((None, None, LANES, t), lambda bi, p: (bi, p, 0, 0))],
        out_specs=pl.BlockSpec((None, t, LANES), lambda bi, p: (bi, 0, p)),
        out_shape=jax.ShapeDtypeStruct((b, t, d), BF16),
        compiler_params=_cparams(("parallel", "parallel")),
        name="attn_sample",
    )(wq, hist_k, hist_v, cp, kn, vnt)


def _sample_query_weights(qb, b, t):
    pairs = D_MODEL // LANES
    q = jnp.transpose(qb.reshape(b, t, pairs, 2, HEAD_DIM), (0, 2, 3, 4, 1))
    zero = jnp.zeros_like(q[:, :, 0])
    top = jnp.concatenate([jnp.concatenate([q[:, :, 0], zero], axis=-1),
                           jnp.concatenate([zero, q[:, :, 1]], axis=-1)], axis=-2)
    ones = jnp.broadcast_to(jnp.stack([_ones_rows(t, p) for p in range(pairs)]), (b, pairs, LANES, 2 * t))
    return jnp.concatenate([top, ones], axis=-2)


def _outproj_kernel(attn_ref, cact_ref, gate_ref, x_ref, wco_ref, wao_ref, wo_ref, g_ref, wpq_ref, keys_ref,
                    x1_ref, h2_ref, st_ref):
    conv_out = jnp.dot(cact_ref[...], wco_ref[...], preferred_element_type=F32)
    attn_out = jnp.dot(attn_ref[...], wao_ref[...], preferred_element_type=F32)
    merged = gate_ref[:, :D_MODEL] * conv_out + gate_ref[:, D_MODEL:] * attn_out
    x1 = x_ref[...] + jnp.dot(merged.astype(BF16), wo_ref[...], preferred_element_type=F32)
    x1_ref[...] = x1
    h2f = x1 * lax.rsqrt(jnp.mean(x1 * x1, axis=-1, keepdims=True) + EPS) * g_ref[...]
    h2 = h2f.astype(BF16)
    h2_ref[...] = h2f.T.astype(BF16)
    qp =jnp.dot(h2, wpq_ref[...], preferred_element_type=F32).astype(BF16)
    for hp in range(2 * PEER_HEADS):
        st_ref[hp] = lax.dot_general(keys_ref[hp], qp[:, hp * LANES:(hp + 1) * LANES], (((1,), (1,)), ((), ())),
                                     preferred_element_type=F32)


def _outproj(attn, cact, gate, x, w):
    n = x.shape[0]
    tm = min(256, n)
    tok = lambda c: pl.BlockSpec((tm, c), lambda i: (i, 0))
    nhp = 2 * PEER_HEADS
    return pl.pallas_call(
        _outproj_kernel,
        grid=(n // tm,),
        in_specs=[tok(D_MODEL), tok(D_CONV), tok(2 * D_MODEL), tok(D_MODEL),
                  _const_spec((D_CONV, D_MODEL)), _const_spec((D_MODEL, D_MODEL)), _const_spec((D_MODEL, D_MODEL)),
                  _const_spec((1, D_MODEL)), _const_spec((D_MODEL, nhp * LANES)),
                  _const_spec((nhp, PEER_KEYS, LANES))],
        out_specs=(tok(D_MODEL), pl.BlockSpec((D_MODEL, tm), lambda i: (0, i)),
                   pl.BlockSpec((nhp, PEER_KEYS, tm), lambda i: (0, 0, i))),
        out_shape=(jax.ShapeDtypeStruct((n, D_MODEL), F32), jax.ShapeDtypeStruct((D_MODEL, n), BF16),
                   jax.ShapeDtypeStruct((nhp, PEER_KEYS, n), F32)),
        compiler_params=_cparams(("parallel",)),
        name="outproj",
    )(attn, cact, gate, x, w["w_conv_out"], w["w_attn_out"], w["w_out"], w["norm_ffn"], w["peer_w_q"], w["peer_keys"])


def _top_ranks(s, break_ties, want_rank=True):
    nrow = s.shape[0]
    iota = lax.broadcasted_iota(jnp.int32, s.shape, 0).astype(F32)
    rank = jnp.full(s.shape, float(PEER_TOPK), F32) if want_rank else None
    vals = []
    for r in range(PEER_TOPK):
        m = jnp.max(s, axis=0, keepdims=True)
        sel = s == m
        if break_ties:
            sel = iota == jnp.min(jnp.where(sel, iota, float(nrow)), axis=0, keepdims=True)
        if want_rank:
            rank = jnp.where(sel, float(r), rank)
        s = jnp.where(sel, -jnp.inf, s)
        vals.append(m)
    count = jnp.sum(jnp.where(s == -jnp.inf, 1.0, 0.0), axis=0, keepdims=True)
    return rank, vals, count


def _staircase_counts(sv1, sv2):
    top = float(PEER_TOPK)
    iota = lax.broadcasted_iota(jnp.int32, sv1.shape, 0).astype(F32)
    n = jnp.zeros(sv1.shape, F32)
    head = sv1 + sv2[0:1, :]
    m0 = head[0:1, :]
    z = jnp.zeros(m0.shape, F32)
    for _ in range(PEER_TOPK):
        m = jnp.max(head, axis=0, keepdims=True)
        a_star = jnp.min(jnp.where(head == m, iota, top), axis=0, keepdims=True)
        onehot = iota == a_star
        z = z + jnp.exp(m - m0)
        n = n + jnp.where(onehot, 1.0, 0.0)
        n_star = jnp.sum(jnp.where(onehot, n, 0.0), axis=0, keepdims=True)
        nxt = jnp.sum(jnp.where(iota == n_star, sv2, 0.0), axis=0, keepdims=True)
        nxt = jnp.where(n_star >= top, -jnp.inf, nxt)
        a_val = jnp.sum(jnp.where(onehot, sv1, 0.0), axis=0, keepdims=True)
        head = jnp.where(onehot, a_val + nxt, head)
    return n, z


def _topk_head(st_ref, r2_ref, nn_ref, e2_ref, cc_ref, h, break_ties):
    top = float(PEER_TOPK)
    s1 = st_ref[2 * h]
    s2 = st_ref[2 * h + 1]
    rank1, v1, count1 = _top_ranks(s1, break_ties, want_rank=break_ties)
    rank2, v2, count2 = _top_ranks(s2, break_ties)
    n, z = _staircase_counts(jnp.concatenate(v1, axis=0), jnp.concatenate(v2, axis=0))
    nn = jnp.zeros(s1.shape, F32)
    for a in range(PEER_TOPK):
        is_a = rank1 == float(a) if break_ties else s1 == v1[a]
        nn = jnp.where(is_a, n[a:a + 1, :], nn)
    taken1 = rank1 < top if break_ties else s1 >= v1[PEER_TOPK - 1]
    r2_ref[h] = rank2.astype(BF16)
    nn_ref[h] = nn
    e2_ref[h] = jnp.where(rank2 < top, jnp.exp(s2 - v2[0]), 0.0).astype(BF16)
    cc_ref[h] = jnp.where(taken1, jnp.exp(s1 - v1[0]), 0.0) / z
    return jnp.max(jnp.abs(count1 - top) + jnp.abs(count2 - top)) > 0.0


def _topk_kernel(st_ref, r2_ref, nn_ref, e2_ref, cc_ref):
    def body(i, carry):
        tied_a = _topk_head(st_ref, r2_ref, nn_ref, e2_ref, cc_ref, 2 * i, False)
        tied_b = _topk_head(st_ref, r2_ref, nn_ref, e2_ref, cc_ref, 2 * i + 1, False)

        @pl.when(jnp.logical_or(tied_a, tied_b))
        def _():
            _topk_head(st_ref, r2_ref, nn_ref, e2_ref, cc_ref, 2 * i, True)
            _topk_head(st_ref, r2_ref, nn_ref, e2_ref, cc_ref, 2 * i + 1, True)

        return carry

    lax.fori_loop(0, PEER_HEADS // 2, body, 0)


def _topk(st):
    nhp, nkeys, n = st.shape
    tl = min(2 * LANES, n)
    out_spec = pl.BlockSpec((PEER_HEADS, nkeys, tl), lambda i: (0, 0, i))
    sds = lambda dt: jax.ShapeDtypeStruct((PEER_HEADS, nkeys, n), dt)
    return pl.pallas_call(
        _topk_kernel,
        grid=(n // tl,),
        in_specs=[pl.BlockSpec((nhp, nkeys, tl), lambda i: (0, 0, i))],
        out_specs=(out_spec, out_spec, out_spec, out_spec),
        out_shape=(sds(BF16), sds(F32), sds(BF16), sds(F32)),
        compiler_params=_cparams(("parallel",)),
        name="peer_topk",
    )(st)


GELU_A = -2.0 * 0.7978845608028654 * LOG2E
GELU_B = GELU_A * 0.044715


def _gelu_tanh(x):
    return x / (1.0 + jnp.exp2(x * (GELU_A + GELU_B * (x * x))))


def _peer_kernel(h2t_ref, x1_ref, u_ref, vt_ref, r2_ref, e2_ref, nn_ref, cc_ref, o_ref, acc_ref, a_ref, gw_ref, w_ref,
                 *, ni, ne, steps):
    s = pl.program_id(0)

    @pl.when(s == 0)
    def _():
        a_ref[...] = jnp.zeros(a_ref.shape, F32)
        gw_ref[...] = jnp.zeros(gw_ref.shape, BF16)

    @pl.when(jnp.logical_or(s < 2, lax.rem(jnp.maximum(s - 2, 0), ne) == 0))
    def _():
        acc_ref[...] = jnp.zeros(acc_ref.shape, F32)

    slot_new = lax.rem(s, 2)
    slot_mid = 1 - slot_new
    tm = h2t_ref.shape[1]
    tiles_per_block = nn_ref.shape[1] // ni
    row0 = lax.rem(lax.rem(jnp.clip(s - 1, 0, steps - 1), ne), tiles_per_block) * ni

    def rows_bf16(ref, h, i):
        tile = jnp.broadcast_to(ref[h, pl.ds(row0 + i, 1), :], (16, tm)).astype(BF16)
        return jnp.concatenate([tile] * (PEER_KEYS // 16), axis=0)

    for i in range(ni):
        w = None
        for h in range(PEER_HEADS):
            keep = r2_ref[h] < rows_bf16(nn_ref, h, i)
            term = jnp.where(keep, e2_ref[h] * rows_bf16(cc_ref, h, i), jnp.zeros((), BF16))
            w = term if w is None else w + term
        w_ref[i * PEER_KEYS:(i + 1) * PEER_KEYS, :] = w
    a_ref[slot_new] = jnp.dot(u_ref[...], h2t_ref[...], preferred_element_type=F32)
    acc_ref[...] += jnp.dot(vt_ref[...], gw_ref[slot_new], preferred_element_type=F32)
    for i in range(ni):
        rows = slice(i * PEER_KEYS, (i + 1) * PEER_KEYS)
        gw_ref[slot_mid, rows, :] = _gelu_tanh(a_ref[slot_mid, rows, :]).astype(BF16) * w_ref[rows, :]

    @pl.when(jnp.logical_and(s >= 2, lax.rem(jnp.maximum(s - 2, 0), ne) == ne - 1))
    def _():
        o_ref[...] = x1_ref[...] + acc_ref[...].T


def _peer(h2t, x1, r2, nn, e2, cc, u_tab, vt_tab):
    d, n = h2t.shape
    tm = min(1024, n)
    te = 512
    ne = u_tab.shape[0] // te
    ni = te // PEER_KEYS
    steps = (n // tm) * ne
    pair = lambda s, lag: jnp.clip(s - lag, 0, steps - 1)
    tile = lambda s, lag: pair(s, lag) // ne
    expert = lambda s, lag: lax.rem(pair(s, lag), ne)
    row_block = max(ni, SUBLANE_TILE)
    key_spec = pl.BlockSpec((PEER_HEADS, PEER_KEYS, tm), lambda s: (0, 0, tile(s, 1)))
    row_spec = pl.BlockSpec((PEER_HEADS, row_block, tm), lambda s: (0, expert(s, 1) // (row_block // ni), tile(s, 1)))
    out_spec = pl.BlockSpec((tm, d), lambda s: (tile(s, 2), 0))
    return pl.pallas_call(
        functools.partial(_peer_kernel, ni=ni, ne=ne, steps=steps),
        grid=(steps + 2,),
        in_specs=[pl.BlockSpec((d, tm), lambda s: (0, tile(s, 0))), out_spec,
                  pl.BlockSpec((te, d), lambda s: (expert(s, 0), 0)),
                  pl.BlockSpec((d, te), lambda s: (0, expert(s, 2))), key_spec, key_spec, row_spec, row_spec],
        out_specs=out_spec,
        out_shape=jax.ShapeDtypeStruct((n, d), F32),
        scratch_shapes=[pltpu.VMEM((d, tm), F32), pltpu.VMEM((2, te, tm), F32), pltpu.VMEM((2, te, tm), BF16),
                        pltpu.VMEM((te, tm), BF16)],
        compiler_params=_cparams(("arbitrary",)),
        name="peer_experts",
    )(h2t, x1, u_tab, vt_tab, r2, e2, nn, cc)


def _prep_weights(l, norm_mix, w_in, conv_w, conv_b, conv_norm_g, conv_norm_b, w_conv_out, forget_bias, q_norm,
                  k_norm, w_attn_out, w_out, norm_ffn, peer_w_q, peer_keys, peer_u, peer_v):
    o1 = 2 * D_CONV
    o2 = o1 + 3 * D_MODEL
    o3 = o2 + N_HEADS
    wl = w_in[l]
    head_of_col = jnp.arange(D_MODEL) // HEAD_DIM
    seg = (head_of_col[:, None] == jnp.arange(LANES)[None, :]).astype(BF16)
    row = lambda a: a.reshape(1, -1).astype(F32)
    return {
        "norm_mix": row(norm_mix[l]),
        "w_glu": wl[:, :o1].astype(BF16),
        "w_qkv": wl[:, o1:o2].astype(BF16),
        "w_f": jnp.pad(wl[:, o2:o3], ((0, 0), (0, LANES - N_HEADS))).astype(BF16),
        "w_gate": wl[:, o3:].astype(BF16),
        "f_bias": jnp.pad(row(forget_bias[l]), ((0, 0), (0, LANES - N_HEADS))),
        "q_gain": row(jnp.tile(q_norm[l], N_HEADS) * (HEAD_DIM ** -0.5 * LOG2E)),
        "k_gain": row(jnp.tile(k_norm[l], N_HEADS)),
        "seg": seg,
        "segt": seg.T,
        "conv_w": jnp.pad(conv_w[l], ((0, HIST_ROWS - CONV_WIDTH), (0, 0))),
        "conv_b": row(conv_b[l]),
        "ln_g": row(conv_norm_g[l]),
        "ln_b": row(conv_norm_b[l]),
        "w_conv_out": w_conv_out[l].astype(BF16),
        "w_attn_out": w_attn_out[l].astype(BF16),
        "w_out": w_out[l].astype(BF16),
        "norm_ffn": row(norm_ffn[l]),
        "peer_w_q": peer_w_q[l].astype(BF16),
        "peer_keys": jnp.pad(peer_keys[l].reshape(2 * PEER_HEADS, PEER_KEYS, -1).astype(BF16),
                             ((0, 0), (0, 0), (0, LANES - peer_keys.shape[-1]))),
        "peer_u": peer_u[l].astype(BF16),
        "peer_vt": peer_v[l].T.astype(BF16),
    }


def _layer_group(x3, conv_state, cache, w):
    b, t, d = x3.shape
    n = b * t
    assert t >= CONV_STATE
    x = x3.reshape(n, d)
    u, qb, qt, k, v, kb, vb, vt, logf, gate = _inproj(x, w)
    u3 = u.reshape(b, t, D_CONV)
    if conv_state is None:
        buf = jnp.zeros((b, HIST_ROWS, D_CONV), F32)
    else:
        buf = jnp.pad(conv_state, ((0, 0), (HIST_ROWS - CONV_STATE, 0), (0, 0)))
    cact = _conv(u3, buf, w)

    logf3 = logf.reshape(b, t, N_HEADS)
    if cache is None:
        cp = _cumsum_pieces(logf3)
        attn = _attn_prompt(qt, kb.reshape(b, t, d), cp, vt, b, t)
    else:
        cache_kt, cache_vt, cache_logf, layer = cache
        past = cache_kt.shape[-1]
        total = past + t
        padded = -(-total // LANES) * LANES
        lf = jnp.concatenate([jnp.transpose(cache_logf[layer].astype(F32), (0, 2, 1)),
                              jnp.transpose(logf3, (0, 2, 1))], axis=2)
        c = _cumsum_lanes(jnp.pad(lf, ((0, 0), (0, 0), (0, padded - total))).reshape(b * N_HEADS, padded))
        c = c.reshape(b, N_HEADS // CACHED_HEADS, CACHED_HEADS, padded)
        heads = lambda a: jnp.transpose(a.reshape(b, t, N_HEADS, HEAD_DIM), (0, 2, 1, 3))
        attn = _attn_cached(heads(qb), heads(kb), heads(vb), cache_kt, cache_vt, layer,
                            c[..., :past], c[..., past:total])
        attn = jnp.transpose(attn, (0, 2, 1, 3))

    x1, h2, st = _outproj(attn.reshape(n, d), cact.reshape(n, D_CONV), gate, x, w)
    r2, nn, e2, cc = _topk(st)
    y = _peer(h2, x1, r2, nn, e2, cc, w["peer_u"], w["peer_vt"])
    return (y.reshape(b, t, d), k.reshape(b, t, N_HEADS, HEAD_DIM), v.reshape(b, t, N_HEADS, HEAD_DIM),
            logf3, u3[:, t - CONV_STATE:, :])


def kernel(x_prompt, x_sample, cache_k, cache_v, cache_logf, state_conv, norm_mix, w_in, conv_w, conv_b, conv_norm_g, conv_norm_b, w_conv_out, forget_bias, q_norm, k_norm, w_attn_out, w_out, norm_ffn, peer_w_q, peer_keys, peer_u, peer_v):
    depth = w_in.shape[0]
    yp, ys = x_prompt, x_sample
    outs_p, outs_s = [], []
    flat = cache_k.shape[:3] + (D_MODEL,)
    hist_k = jnp.transpose(cache_k, (0, 1, 3, 4, 2))
    hist_v = jnp.transpose(cache_v, (0, 1, 3, 4, 2))
    for l in range(depth):
        w = _prep_weights(l, norm_mix, w_in, conv_w, conv_b, conv_norm_g, conv_norm_b, w_conv_out, forget_bias,
                          q_norm, k_norm, w_attn_out, w_out, norm_ffn, peer_w_q, peer_keys, peer_u, peer_v)
        yp, *rest_p = _layer_group(yp, None, None, w)
        ys, *rest_s = _layer_group(ys, state_conv[l], (hist_k, hist_v, cache_logf, l), w)
        outs_p.append(rest_p)
        outs_s.append(rest_s)
    stack = lambda outs, i: jnp.stack([o[i] for o in outs])
    return (yp, ys, stack(outs_p, 0), stack(outs_p, 1), stack(outs_p, 2), stack(outs_p, 3),
            stack(outs_s, 0), stack(outs_s, 1), stack(outs_s, 2), stack(outs_s, 3))
```
